```python
import jax, jax.numpy as jnp
from jax import lax
import numpy as np

D_MODEL = 1024
BATCH = 2
SEQ = 8192
DEPTH = 2

D_CONV = D_MODEL
CONV_GROUPS = 16
CONV_WIDTH = 3
POOL_WINDOWS = (2, 4, 8, 16)
N_POOL_GROUPS = len(POOL_WINDOWS)
POOL_GROUP_DIM = D_MODEL // 8
D_POOL = N_POOL_GROUPS * POOL_GROUP_DIM
D_IN = 3 * D_CONV + D_POOL + 2 * D_MODEL
N_EXPERTS = 64
TOP_K = 8
N_EXPERT_GROUPS = 8
TOPK_GROUPS = 4
D_EXPERT = D_MODEL // 4
D_SHARED = D_EXPERT
ROUTED_SCALE = 2.5
EPS = 1e-6

kernel_name = "hybrid_conv_pool_moe_block"


def rms_norm(x, g):
    xf = x.astype(jnp.float32)
    y = xf * lax.rsqrt(jnp.mean(xf * xf, axis=-1, keepdims=True) + EPS)
    return (y * g.astype(jnp.float32)).astype(x.dtype)


def causal_depthwise_conv(u, w):
    n_ch = u.shape[-1]
    return lax.conv_general_dilated(
        u, w[:, None, :].astype(u.dtype), window_strides=(1,),
        padding=[(CONV_WIDTH - 1, 0)], dimension_numbers=("NWC", "WIO", "NWC"),
        feature_group_count=n_ch)


def causal_window_mean(u, w):
    s = u.shape[1]
    cs = jnp.cumsum(u.astype(jnp.float32), axis=1)
    lagged = jnp.pad(cs, ((0, 0), (w, 0), (0, 0)))[:, :s]
    cnt = jnp.minimum(jnp.arange(1, s + 1), w).astype(jnp.float32)
    return (cs - lagged) / cnt[None, :, None]


def pool_mixer(u, w_group, scale):
    b, s, _ = u.shape
    ug = u.reshape(b, s, N_POOL_GROUPS, POOL_GROUP_DIM)
    pooled = jnp.stack([causal_window_mean(ug[:, :, g], POOL_WINDOWS[g])
                        for g in range(N_POOL_GROUPS)], axis=2)
    diff = (pooled - ug.astype(jnp.float32)).astype(u.dtype)
    z = jnp.einsum("bsgc,gcd->bsgd", diff, w_group).reshape(b, s, D_POOL)
    return z * scale


def token_mixer(h, w_in, conv_w, w_conv_out, w_pool_group, pool_scale, w_pool_proj, w_o):
    proj = jnp.einsum("bsd,de->bse", h, w_in)
    cuts = np.cumsum([D_CONV, D_CONV, D_CONV, D_POOL, D_MODEL])
    gb, gc, v, u, a_conv, a_pool = jnp.split(proj, cuts, axis=-1)
    y_conv = jnp.einsum("bsc,cd->bsd", gb * causal_depthwise_conv(gc * v, conv_w), w_conv_out)
    y_pool = jnp.einsum("bsc,cd->bsd", pool_mixer(u, w_pool_group, pool_scale), w_pool_proj)
    merged = jax.nn.sigmoid(a_conv) * y_conv + jax.nn.sigmoid(a_pool) * y_pool
    return jnp.einsum("bsd,de->bse", merged, w_o)


def swiglu(h, w_gate, w_up, w_down):
    return (jax.nn.silu(h @ w_gate) * (h @ w_up)) @ w_down


def route(hf, w_router, router_bias):
    t = hf.shape[0]
    s = jax.nn.sigmoid((hf @ w_router).astype(jnp.float32))
    sel = s + router_bias.astype(jnp.float32)
    grouped = sel.reshape(t, N_EXPERT_GROUPS, N_EXPERTS // N_EXPERT_GROUPS)
    group_score = lax.top_k(grouped, 2)[0].sum(-1)
    _, gidx = lax.top_k(group_score, TOPK_GROUPS)
    gmask = jax.nn.one_hot(gidx, N_EXPERT_GROUPS, dtype=jnp.float32).sum(1)
    emask = jnp.repeat(gmask, N_EXPERTS // N_EXPERT_GROUPS, axis=1)
    masked = jnp.where(emask > 0, sel, -jnp.inf)
    _, eidx = lax.top_k(masked, TOP_K)
    wts = jnp.take_along_axis(s, eidx, axis=1)
    wts = wts / jnp.sum(wts, axis=-1, keepdims=True) * ROUTED_SCALE
    return eidx, wts


def moe_ffn(h, w_router, router_bias, w_exp_gate, w_exp_up, w_exp_down, w_sh_gate, w_sh_up, w_sh_down):
    b, s, d = h.shape
    hf = h.reshape(-1, d)
    t = hf.shape[0]
    eidx, wts = route(hf, w_router, router_bias)
    flat_e = eidx.reshape(-1)
    order = jnp.argsort(flat_e)
    tok = order // TOP_K
    xs = hf[tok]
    group_sizes = jnp.bincount(flat_e, length=N_EXPERTS).astype(jnp.int32)
    act = jax.nn.silu(lax.ragged_dot(xs, w_exp_gate, group_sizes)) * lax.ragged_dot(xs, w_exp_up, group_sizes)
    out = lax.ragged_dot(act, w_exp_down, group_sizes)
    out = out * wts.reshape(-1)[order][:, None].astype(out.dtype)
    routed = jax.ops.segment_sum(out, tok, num_segments=t)
    shared = swiglu(hf, w_sh_gate, w_sh_up, w_sh_down)
    return (routed + shared).reshape(b, s, d)


def setup_inputs(seed: int = 0) -> dict:
    key = jax.random.key(seed)
    ks = jax.random.split(key, 24)
    f32 = jnp.float32

    def nrm(k, shape, fan_in, mult=1.0):
        return (jax.random.normal(k, shape, f32) * (mult * fan_in ** -0.5)).astype(f32)

    def gain(k, shape):
        return 1.0 + 0.05 * jax.random.normal(k, shape, f32)

    L, D = DEPTH, D_MODEL
    return {
        "x": jax.random.normal(ks[0], (BATCH, SEQ, D), f32),
        "c": jax.random.normal(ks[1], (BATCH, D), f32),
        "w_ada": nrm(ks[2], (L, D, 6 * D), D, 0.5),
        "b_ada": 0.02 * jax.random.normal(ks[3], (L, 6 * D), f32),
        "g_pre_mix": gain(ks[4], (L, D)),
        "g_post_mix": gain(ks[5], (L, D)),
        "g_pre_ffn": gain(ks[6], (L, D)),
        "g_post_ffn": gain(ks[7], (L, D)),
        "w_in": nrm(ks[8], (L, D, D_IN), D),
        "conv_w": nrm(ks[9], (L, CONV_WIDTH, D_CONV), CONV_WIDTH),
        "w_conv_out": nrm(ks[10], (L, D_CONV, D), D_CONV),
        "w_pool_group": nrm(ks[11], (L, N_POOL_GROUPS, POOL_GROUP_DIM, POOL_GROUP_DIM), POOL_GROUP_DIM),
        "pool_scale": gain(ks[12], (L, D_POOL)),
        "w_pool_proj": nrm(ks[13], (L, D_POOL, D), D_POOL),
        "w_o": nrm(ks[14], (L, D, D), D),
        "w_router": nrm(ks[15], (L, D, N_EXPERTS), D),
        "router_bias": 0.01 * jax.random.normal(ks[16], (L, N_EXPERTS), f32),
        "w_exp_gate": nrm(ks[17], (L, N_EXPERTS, D, D_EXPERT), D),
        "w_exp_up": nrm(ks[18], (L, N_EXPERTS, D, D_EXPERT), D),
        "w_exp_down": nrm(ks[19], (L, N_EXPERTS, D_EXPERT, D), D_EXPERT),
        "w_sh_gate": nrm(ks[20], (L, D, D_SHARED), D),
        "w_sh_up": nrm(ks[21], (L, D, D_SHARED), D),
        "w_sh_down": nrm(ks[22], (L, D_SHARED, D), D_SHARED),
    }


def reference(x, c, w_ada, b_ada, g_pre_mix, g_post_mix, g_pre_ffn, g_post_ffn,
              w_in, conv_w, w_conv_out, w_pool_group, pool_scale, w_pool_proj, w_o,
              w_router, router_bias, w_exp_gate, w_exp_up, w_exp_down,
              w_sh_gate, w_sh_up, w_sh_down):
    cond = jax.nn.silu(c)
    for l in range(DEPTH):
        mod = cond @ w_ada[l] + b_ada[l]
        sh1, sc1, gt1, sh2, sc2, gt2 = [m[:, None, :] for m in jnp.split(mod, 6, axis=-1)]
        h = rms_norm(x, g_pre_mix[l]) * (1.0 + sc1) + sh1
        y = token_mixer(h, w_in[l], conv_w[l], w_conv_out[l], w_pool_group[l],
                        pool_scale[l], w_pool_proj[l], w_o[l])
        x = x + gt1 * rms_norm(y, g_post_mix[l])
        h = rms_norm(x, g_pre_ffn[l]) * (1.0 + sc2) + sh2
        y = moe_ffn(h, w_router[l], router_bias[l], w_exp_gate[l], w_exp_up[l], w_exp_down[l],
                    w_sh_gate[l], w_sh_up[l], w_sh_down[l])
        x = x + gt2 * rms_norm(y, g_post_ffn[l])
    return x
```

```python
import functools

import jax
import jax.numpy as jnp
from jax import lax
from jax.experimental import pallas as pl
from jax.experimental.pallas import tpu as pltpu

F32 = jnp.float32
BF16 = jnp.bfloat16

EPS = 1e-6
POOL_WINDOWS = (2, 4, 8, 16)
POOL_GROUP_DIM = 128
N_EXPERTS = 64
N_GROUPS = 8
GROUP_SIZE = 8
TOPK_GROUPS = 4
TOP_K = 8
ROUTED_SCALE = 2.5

LANES = 128
SUBLANES = 8
CONV_HALO = 8
POOL_HALO = 16
VMEM_LIMIT = 56 * 1024 * 1024

MIX_TILE = 256
ROUTE_SUB = 256
ROUTE_LANES = 512
MOE_BLOCK = 2048
MOE_CHUNK = 128
EPI_TILE = 512


def _silu(v):
    return v * jax.nn.sigmoid(v)


def _rms_scale(v):
    return lax.rsqrt(jnp.mean(v * v, axis=-1, keepdims=True) + EPS)


def _ada_kernel(c_ref, w_ref, b_ref, o_ref):
    cond = _silu(c_ref[...])
    o_ref[0] = jnp.dot(cond, w_ref[0], preferred_element_type=F32,
                       precision=lax.Precision.HIGHEST) + b_ref[0]


def _ada_mod(c, w_ada, b_ada):
    depth, d, d6 = w_ada.shape
    b = c.shape[0]
    n_col = d6 // d
    return pl.pallas_call(
        _ada_kernel,
        grid=(depth, n_col),
        in_specs=[
            pl.BlockSpec((b, d), lambda l, n: (0, 0)),
            pl.BlockSpec((1, d, d), lambda l, n: (l, 0, n)),
            pl.BlockSpec((1, 1, d), lambda l, n: (l, 0, n)),
        ],
        out_specs=pl.BlockSpec((1, b, d), lambda l, n: (l, 0, n)),
        out_shape=jax.ShapeDtypeStruct((depth, b, d6), F32),
    )(c, w_ada, b_ada.reshape(depth, 1, d6))


def _mixer_kernel(x_ref, mod_ref, gpre_ref, gpost_ref, win_ref, convw_ref, wco_ref, wpg_ref,
                  pscale_ref, wpp_ref, wo_ref, o_ref, uext_ref, pext_ref):
    j = pl.program_id(1)
    tm, d = x_ref.shape[1], x_ref.shape[2]
    d_pool = pext_ref.shape[1]

    @pl.when(j == 0)
    def _():
        uext_ref[0:CONV_HALO, :] = jnp.zeros((CONV_HALO, d), F32)
        pext_ref[0:POOL_HALO, :] = jnp.zeros((POOL_HALO, d_pool), F32)

    x = x_ref[0]
    mod = mod_ref[0]
    sh1, sc1, gt1 = mod[0:1], mod[1:2], mod[2:3]
    h = x * _rms_scale(x) * gpre_ref[...] * (1.0 + sc1) + sh1
    hb = h.astype(BF16)

    def proj(lo, hi):
        return jnp.dot(hb, win_ref[:, lo:hi], preferred_element_type=F32)

    u = proj(d, 2 * d) * proj(2 * d, 3 * d)
    uext_ref[CONV_HALO:CONV_HALO + tm, :] = u
    cw = convw_ref[...]
    conv = (cw[2:3] * u
            + cw[1:2] * uext_ref[CONV_HALO - 1:CONV_HALO - 1 + tm, :]
            + cw[0:1] * uext_ref[CONV_HALO - 2:CONV_HALO - 2 + tm, :])
    uext_ref[0:CONV_HALO, :] = u[tm - CONV_HALO:tm, :]
    y_conv = jnp.dot((proj(0, d) * conv).astype(BF16), wco_ref[...], preferred_element_type=F32)

    up = proj(3 * d, 3 * d + d_pool)
    pext_ref[POOL_HALO:POOL_HALO + tm, :] = up
    pos = j * tm + lax.broadcasted_iota(jnp.int32, (tm, 1), 0)
    zs = []
    for g, w in enumerate(POOL_WINDOWS):
        c0 = g * POOL_GROUP_DIM
        ug = up[:, c0:c0 + POOL_GROUP_DIM]
        acc = ug
        for k in range(1, w):
            acc = acc + pext_ref[POOL_HALO - k:POOL_HALO - k + tm, c0:c0 + POOL_GROUP_DIM]
        inv_cnt = 1.0 / jnp.minimum(pos + 1, w).astype(F32)
        diff = acc * inv_cnt - ug
        zs.append(jnp.dot(diff.astype(BF16), wpg_ref[g], preferred_element_type=F32))
    pext_ref[0:POOL_HALO, :] = up[tm - POOL_HALO:tm, :]
    z = jnp.concatenate(zs, axis=1) * pscale_ref[...]
    y_pool = jnp.dot(z.astype(BF16), wpp_ref[...], preferred_element_type=F32)

    a_conv = proj(3 * d + d_pool, 4 * d + d_pool)
    a_pool = proj(4 * d + d_pool, 5 * d + d_pool)
    merged = jax.nn.sigmoid(a_conv) * y_conv + jax.nn.sigmoid(a_pool) * y_pool
    y = jnp.dot(merged.astype(BF16), wo_ref[...], preferred_element_type=F32)
    o_ref[0] = x + gt1 * (y * _rms_scale(y) * gpost_ref[...])


def _token_mixer(x, mod, g_pre, g_post, w_in, conv_w, w_conv_out, w_pool_group, pool_scale,
                 w_pool_proj, w_o):
    b, s, d = x.shape
    d_in = w_in.shape[1]
    d_pool = w_pool_proj.shape[0]
    tm = min(MIX_TILE, s)
    const2 = lambda bi, j: (0, 0)
    const3 = lambda bi, j: (0, 0, 0)
    return pl.pallas_call(
        _mixer_kernel,
        grid=(b, s // tm),
        in_specs=[
            pl.BlockSpec((1, tm, d), lambda bi, j: (bi, j, 0)),
            pl.BlockSpec((1, 6, d), lambda bi, j: (bi, 0, 0)),
            pl.BlockSpec((1, d), const2),
            pl.BlockSpec((1, d), const2),
            pl.BlockSpec((d, d_in), const2),
            pl.BlockSpec((3, d), const2),
            pl.BlockSpec((d, d), const2),
            pl.BlockSpec(w_pool_group.shape, const3),
            pl.BlockSpec((1, d_pool), const2),
            pl.BlockSpec((d_pool, d), const2),
            pl.BlockSpec((d, d), const2),
        ],
        out_specs=pl.BlockSpec((1, tm, d), lambda bi, j: (bi, j, 0)),
        out_shape=jax.ShapeDtypeStruct(x.shape, F32),
        scratch_shapes=[
            pltpu.VMEM((CONV_HALO + tm, d), F32),
            pltpu.VMEM((POOL_HALO + tm, d_pool), F32),
        ],
        compiler_params=pltpu.CompilerParams(
            dimension_semantics=("arbitrary", "arbitrary"), vmem_limit_bytes=VMEM_LIMIT),
    )(x, mod, g_pre.reshape(1, d), g_post.reshape(1, d), w_in.astype(BF16), conv_w,
      w_conv_out.astype(BF16), w_pool_group.astype(BF16), pool_scale.reshape(1, d_pool),
      w_pool_proj.astype(BF16), w_o.astype(BF16))


def _wins(other, v, tie_i):
    return jnp.where(other > v, 1, 0) + jnp.where(other == v, tie_i, 0)


def _route_kernel(x_ref, mod_ref, g_ref, wrt_ref, bias_ref, wsg_ref, wsu_ref, wsd_ref,
                  h2g_ref, shared_ref, dest_ref, wk_ref, cnt_ref, offs_ref,
                  lg_ref, pre_ref, rho_ref, wd_ref):
    tb, d = x_ref.shape
    n_chunk = d // LANES
    mod = mod_ref[0]
    sh2, sc2 = mod[3:4], mod[4:5]

    for i in range(tb // ROUTE_SUB):
        r0 = i * ROUTE_SUB
        x = x_ref[r0:r0 + ROUTE_SUB, :]
        h = x * _rms_scale(x) * g_ref[...] * (1.0 + sc2) + sh2
        for c in range(n_chunk):
            h2g_ref[pl.ds(r0 * n_chunk + c, ROUTE_SUB, stride=n_chunk), :] = (
                h[:, c * LANES:(c + 1) * LANES])
        hb = h.astype(BF16)
        act = (_silu(jnp.dot(hb, wsg_ref[...], preferred_element_type=F32))
               * jnp.dot(hb, wsu_ref[...], preferred_element_type=F32))
        shared_ref[r0:r0 + ROUTE_SUB, :] = jnp.dot(act.astype(BF16), wsd_ref[...],
                                                   preferred_element_type=F32)
        lg_ref[:, r0:r0 + ROUTE_SUB] = lax.dot_general(
            wrt_ref[...], h, (((1,), (1,)), ((), ())), preferred_element_type=F32,
            precision=lax.Precision.HIGHEST)

    lc = min(ROUTE_LANES, tb)
    gidx = lax.broadcasted_iota(jnp.int32, (N_GROUPS, lc), 0)
    tie = [None] + [jnp.where(gidx >= r, 1, 0) for r in range(1, N_GROUPS)]
    tri = (lax.broadcasted_iota(jnp.int32, (lc, lc), 0)
           < lax.broadcasted_iota(jnp.int32, (lc, lc), 1)).astype(BF16)
    carry = jnp.zeros((N_EXPERTS, 1), F32)
    neg_inf = jnp.float32(-jnp.inf)
    for ci in range(tb // lc):
        c0 = ci * lc
        s_all = jax.nn.sigmoid(lg_ref[:, c0:c0 + lc])
        aff = [s_all[GROUP_SIZE * jj:GROUP_SIZE * (jj + 1), :] for jj in range(GROUP_SIZE)]
        sel = [aff[jj] + bias_ref[GROUP_SIZE * jj:GROUP_SIZE * (jj + 1), :]
               for jj in range(GROUP_SIZE)]
        m1, m2 = sel[0], jnp.full_like(sel[0], neg_inf)
        for jj in range(1, GROUP_SIZE):
            m2 = jnp.maximum(m2, jnp.minimum(m1, sel[jj]))
            m1 = jnp.maximum(m1, sel[jj])
        gs = m1 + m2
        beaten = jnp.zeros((N_GROUPS, lc), jnp.int32)
        for r in range(1, N_GROUPS):
            other = pltpu.roll(gs, r, axis=0)
            beaten = beaten + _wins(other, gs, tie[r])
        gmask = beaten < TOPK_GROUPS
        masked = [jnp.where(gmask, sel[jj], neg_inf) for jj in range(GROUP_SIZE)]
        rolled = [[masked[jj]] + [pltpu.roll(masked[jj], r, axis=0) for r in range(1, N_GROUPS)]
                  for jj in range(GROUP_SIZE)]
        rho = []
        for jj in range(GROUP_SIZE):
            v = masked[jj]
            cnt = jnp.zeros((N_GROUPS, lc), jnp.int32)
            for j2 in range(GROUP_SIZE):
                for r in range(N_GROUPS):
                    if r == 0 and j2 == jj:
                        continue
                    other = rolled[j2][r]
                    if r == 0:
                        wins = (other >= v) if j2 < jj else (other > v)
                        cnt = cnt + jnp.where(wins, 1, 0)
                    else:
                        cnt = cnt + _wins(other, v, tie[r])
            rho.append(cnt)
        chosen = [rho[jj] < TOP_K for jj in range(GROUP_SIZE)]
        ssum = jnp.zeros((N_GROUPS, lc), F32)
        for jj in range(GROUP_SIZE):
            ssum = ssum + jnp.where(chosen[jj], aff[jj], 0.0)
        ssum = jnp.sum(ssum, axis=0, keepdims=True)
        wdense = [jnp.where(chosen[jj], aff[jj] / ssum * ROUTED_SCALE, 0.0)
                  for jj in range(GROUP_SIZE)]
        chosen_f = jnp.concatenate([c.astype(F32) for c in chosen], axis=0)
        prefix = jnp.dot(chosen_f.astype(BF16), tri, preferred_element_type=F32) + carry
        carry = carry + jnp.sum(chosen_f, axis=1, keepdims=True)
        pre_ref[:, c0:c0 + lc] = prefix
        rho_ref[:, c0:c0 + lc] = jnp.concatenate(rho, axis=0)
        wd_ref[:, c0:c0 + lc] = jnp.concatenate(wdense, axis=0)

    cnt_b = jnp.broadcast_to(carry, (N_EXPERTS, LANES))
    lower = (lax.broadcasted_iota(jnp.int32, (N_EXPERTS, N_EXPERTS), 1)
             < lax.broadcasted_iota(jnp.int32, (N_EXPERTS, N_EXPERTS), 0)).astype(F32)
    offs_b = jnp.dot(lower, cnt_b, preferred_element_type=F32, precision=lax.Precision.HIGHEST)
    cnt_ref[0] = cnt_b.astype(jnp.int32)
    offs_ref[0] = offs_b.astype(jnp.int32)
    offs_col = offs_b[:, 0:1]

    for ci in range(tb // lc):
        c0 = ci * lc
        dest_dense = pre_ref[:, c0:c0 + lc] + offs_col
        rho_c = rho_ref[:, c0:c0 + lc]
        w_c = wd_ref[:, c0:c0 + lc]
        for k in range(TOP_K):
            hit = rho_c == k
            dk = jnp.sum(jnp.where(hit, dest_dense, 0.0), axis=0, keepdims=True)
            wk = jnp.sum(jnp.where(hit, w_c, 0.0), axis=0, keepdims=True)
            dest_ref[0, k:k + 1, c0:c0 + lc] = dk.astype(jnp.int32)
            wk_ref[0, k:k + 1, c0:c0 + lc] = wk


def _route(x1, mod, g_pre, w_router, router_bias, w_sh_gate, w_sh_up, w_sh_down, seq, tb):
    t, d = x1.shape
    nb = t // tb
    n_chunk = d // LANES
    d_sh = w_sh_gate.shape[1]
    perm = jnp.arange(N_EXPERTS).reshape(N_GROUPS, GROUP_SIZE).T.reshape(-1)
    wrt = w_router.T[perm]
    bias = router_bias[perm].reshape(N_EXPERTS, 1)
    const2 = lambda i: (0, 0)
    outs = pl.pallas_call(
        _route_kernel,
        grid=(nb,),
        in_specs=[
            pl.BlockSpec((tb, d), lambda i: (i, 0)),
            pl.BlockSpec((1, 6, d), lambda i: (i * tb // seq, 0, 0)),
            pl.BlockSpec((1, d), const2),
            pl.BlockSpec((N_EXPERTS, d), const2),
            pl.BlockSpec((N_EXPERTS, 1), const2),
            pl.BlockSpec((d, d_sh), const2),
            pl.BlockSpec((d, d_sh), const2),
            pl.BlockSpec((d_sh, d), const2),
        ],
        out_specs=[
            pl.BlockSpec((tb * n_chunk, LANES), lambda i: (i, 0)),
            pl.BlockSpec((tb, d), lambda i: (i, 0)),
            pl.BlockSpec((1, TOP_K, tb), lambda i: (i, 0, 0)),
            pl.BlockSpec((1, TOP_K, tb), lambda i: (i, 0, 0)),
            pl.BlockSpec((1, N_EXPERTS, LANES), lambda i: (i, 0, 0)),
            pl.BlockSpec((1, N_EXPERTS, LANES), lambda i: (i, 0, 0)),
        ],
        out_shape=[
            jax.ShapeDtypeStruct((t * n_chunk, LANES), F32),
            jax.ShapeDtypeStruct((t, d), F32),
            jax.ShapeDtypeStruct((nb, TOP_K, tb), jnp.int32),
            jax.ShapeDtypeStruct((nb, TOP_K, tb), F32),
            jax.ShapeDtypeStruct((nb, N_EXPERTS, LANES), jnp.int32),
            jax.ShapeDtypeStruct((nb, N_EXPERTS, LANES), jnp.int32),
        ],
        scratch_shapes=[
            pltpu.VMEM((N_EXPERTS, tb), F32),
            pltpu.VMEM((N_EXPERTS, tb), F32),
            pltpu.VMEM((N_EXPERTS, tb), jnp.int32),
            pltpu.VMEM((N_EXPERTS, tb), F32),
        ],
        compiler_params=pltpu.CompilerParams(
            dimension_semantics=("arbitrary",), vmem_limit_bytes=VMEM_LIMIT),
    )(x1, mod, g_pre.reshape(1, d), wrt, bias, w_sh_gate.astype(BF16), w_sh_up.astype(BF16),
      w_sh_down.astype(BF16))
    return outs


def _moe_kernel(cnt_sm, offs_sm, h2g_ref, dest_hbm, wk_hbm, wg_ref, wu_ref, wd_ref, o_ref,
                acc_ref, xt_ref, yt_ref, list_sm, dest_sm, wk_sm, sems):
    bi = pl.program_id(0)
    r = pl.program_id(1)
    tb, d = o_ref.shape
    n_chunk = d // LANES
    n_assign = TOP_K * tb
    m = MOE_CHUNK
    stride = m + SUBLANES

    def meta_copies():
        return (pltpu.make_async_copy(dest_hbm.at[bi], dest_sm, sems.at[0]),
                pltpu.make_async_copy(wk_hbm.at[bi], wk_sm, sems.at[1]))

    @pl.when(r == 0)
    def _():
        for cp in meta_copies():
            cp.start()
        acc_ref[...] = jnp.zeros(acc_ref.shape, F32)
        for cp in meta_copies():
            cp.wait()
        unroll = 16

        def plan(g, carry):
            for q in range(unroll):
                a = g * unroll + q
                list_sm[dest_sm[a]] = a
            return carry
        lax.fori_loop(0, n_assign // unroll, plan, 0)
        for q in range(m):
            list_sm[n_assign + q] = 0

    n_rows = cnt_sm[bi * N_EXPERTS + r]
    seg = offs_sm[bi * N_EXPERTS + r]
    wg = wg_ref[0]
    wu = wu_ref[0]
    wd = wd_ref[0]

    def chunk(ci, carry):
        base = seg + ci * m
        for mi in range(m):
            tok = list_sm[base + mi] & (tb - 1)
            xt_ref[pl.ds(mi, n_chunk, stride=stride), :] = (
                h2g_ref[pl.ds(pl.multiple_of(tok * n_chunk, n_chunk), n_chunk), :])
        xs = jnp.concatenate([xt_ref[c * stride:c * stride + m, :] for c in range(n_chunk)],
                             axis=1).astype(BF16)
        act = (_silu(jnp.dot(xs, wg, preferred_element_type=F32))
               * jnp.dot(xs, wu, preferred_element_type=F32))
        y = jnp.dot(act.astype(BF16), wd, preferred_element_type=F32)
        for c in range(n_chunk):
            yt_ref[c * stride:c * stride + m, :] = y[:, c * LANES:(c + 1) * LANES]
        valid_rows = n_rows - ci * m
        for g0 in range(0, m, SUBLANES):
            rows, vals = [], []
            for mi in range(g0, g0 + SUBLANES):
                a = list_sm[base + mi]
                ok = mi < valid_rows
                tok = jnp.where(ok, a & (tb - 1), tb)
                wgt = jnp.where(ok, wk_sm[a], 0.0)
                row = pl.multiple_of(tok * n_chunk, n_chunk)
                rows.append(row)
                vals.append(acc_ref[pl.ds(row, n_chunk), :]
                            + wgt * yt_ref[pl.ds(mi, n_chunk, stride=stride), :])
            for row, val in zip(rows, vals):
                acc_ref[pl.ds(row, n_chunk), :] = val
        return carry

    lax.fori_loop(0, (n_rows + m - 1) // m, chunk, 0)

    @pl.when(r == N_EXPERTS - 1)
    def _():
        for c in range(n_chunk):
            o_ref[:, c * LANES:(c + 1) * LANES] = acc_ref[pl.ds(c, tb, stride=n_chunk), :]


def _moe(h2g, dest, wk, cnt, offs, w_gate, w_up, w_down, t, d, tb):
    nb = t // tb
    n_chunk = d // LANES
    d_e = w_gate.shape[2]
    m = MOE_CHUNK
    stride = m + SUBLANES

    def expert_of(r):
        return (r % N_GROUPS) * GROUP_SIZE + r // N_GROUPS

    grid_spec = pltpu.PrefetchScalarGridSpec(
        num_scalar_prefetch=2,
        grid=(nb, N_EXPERTS),
        in_specs=[
            pl.BlockSpec((tb * n_chunk, LANES), lambda bi, r, c, o: (bi, 0)),
            pl.BlockSpec(memory_space=pl.ANY),
            pl.BlockSpec(memory_space=pl.ANY),
            pl.BlockSpec((1, d, d_e), lambda bi, r, c, o: (expert_of(r), 0, 0)),
            pl.BlockSpec((1, d, d_e), lambda bi, r, c, o: (expert_of(r), 0, 0)),
            pl.BlockSpec((1, d_e, d), lambda bi, r, c, o: (expert_of(r), 0, 0)),
        ],
        out_specs=pl.BlockSpec((tb, d), lambda bi, r, c, o: (bi, 0)),
        scratch_shapes=[
            pltpu.VMEM(((tb + 1) * n_chunk, LANES), F32),
            pltpu.VMEM((n_chunk * stride, LANES), F32),
            pltpu.VMEM((n_chunk * stride, LANES), F32),
            pltpu.SMEM((TOP_K * tb + m,), jnp.int32),
            pltpu.SMEM((TOP_K * tb,), jnp.int32),
            pltpu.SMEM((TOP_K * tb,), F32),
            pltpu.SemaphoreType.DMA((2,)),
        ],
    )
    return pl.pallas_call(
        _moe_kernel,
        grid_spec=grid_spec,
        out_shape=jax.ShapeDtypeStruct((t, d), F32),
        compiler_params=pltpu.CompilerParams(
            dimension_semantics=("arbitrary", "arbitrary"), vmem_limit_bytes=VMEM_LIMIT),
    )(cnt, offs, h2g, dest, wk, w_gate, w_up, w_down)


def _epilogue_kernel(x_ref, routed_ref, shared_ref, mod_ref, g_ref, o_ref):
    gt2 = mod_ref[0][5:6]
    y = routed_ref[...] + shared_ref[...]
    o_ref[...] = x_ref[...] + gt2 * (y * _rms_scale(y) * g_ref[...])


def _epilogue(x1, routed, shared, mod, g_post, seq):
    t, d = x1.shape
    te = min(EPI_TILE, seq)
    row = pl.BlockSpec((te, d), lambda i: (i, 0))
    return pl.pallas_call(
        _epilogue_kernel,
        grid=(t // te,),
        in_specs=[row, row, row,
                  pl.BlockSpec((1, 6, d), lambda i: (i * te // seq, 0, 0)),
                  pl.BlockSpec((1, d), lambda i: (0, 0))],
        out_specs=row,
        out_shape=jax.ShapeDtypeStruct((t, d), F32),
    )(x1, routed, shared, mod, g_post.reshape(1, d))


def kernel(x, c, w_ada, b_ada, g_pre_mix, g_post_mix, g_pre_ffn, g_post_ffn, w_in, conv_w,
           w_conv_out, w_pool_group, pool_scale, w_pool_proj, w_o, w_router, router_bias,
           w_exp_gate, w_exp_up, w_exp_down, w_sh_gate, w_sh_up, w_sh_down):
    b, s, d = x.shape
    depth = w_ada.shape[0]
    t = b * s
    tb = min(MOE_BLOCK, s)
    nb = t // tb
    mods = _ada_mod(c, w_ada, b_ada).reshape(depth, b, 6, d)
    for l in range(depth):
        mod = mods[l]
        x = _token_mixer(x, mod, g_pre_mix[l], g_post_mix[l], w_in[l], conv_w[l], w_conv_out[l],
                         w_pool_group[l], pool_scale[l], w_pool_proj[l], w_o[l])
        x1 = x.reshape(t, d)
        h2g, shared, dest, wk, cnt, offs = _route(
            x1, mod, g_pre_ffn[l], w_router[l], router_bias[l], w_sh_gate[l], w_sh_up[l],
            w_sh_down[l], s, tb)
        routed = _moe(h2g, dest.reshape(nb, TOP_K * tb), wk.reshape(nb, TOP_K * tb),
                      cnt[:, :, 0].reshape(-1), offs[:, :, 0].reshape(-1),
                      w_exp_gate[l].astype(BF16), w_exp_up[l].astype(BF16),
                      w_exp_down[l].astype(BF16), t, d, tb)
        x = _epilogue(x1, routed, shared, mod, g_post_ffn[l], s).reshape(b, s, d)
    return x
```

```python
import jax
import jax.numpy as jnp
from jax import lax
from jax.experimental import pallas as pl
from jax.experimental.pallas import tpu as pltpu

F32 = jnp.float32
BF16 = jnp.bfloat16

EPS = 1e-6
POOL_WINDOWS = (2, 4, 8, 16)
POOL_GROUP_DIM = 128
N_EXPERTS = 64
N_GROUPS = 8
GROUP_SIZE = 8
TOPK_GROUPS = 4
TOP_K = 8
ROUTED_SCALE = 2.5

LANES = 128
SUBLANES = 8
CONV_HALO = 8
POOL_HALO = 16
VMEM_LIMIT = 56 * 1024 * 1024

MIX_TILE = 256
ROUTE_LANES = 512
MOE_BLOCK = 2048
MOE_CHUNK = 128
EPI_TILE = 512


def _silu(v):
    return v * jax.nn.sigmoid(v)


def _rms_scale(v):
    return lax.rsqrt(jnp.mean(v * v, axis=-1, keepdims=True) + EPS)


def _ada_kernel(c_ref, w_ref, b_ref, o_ref):
    cond = _silu(c_ref[...])
    o_ref[0] = jnp.dot(cond, w_ref[0], preferred_element_type=F32,
                       precision=lax.Precision.HIGHEST) + b_ref[0]


def _ada_mod(c, w_ada, b_ada):
    depth, d, d6 = w_ada.shape
    b = c.shape[0]
    n_col = d6 // d
    return pl.pallas_call(
        _ada_kernel,
        grid=(depth, n_col),
        in_specs=[
            pl.BlockSpec((b, d), lambda l, n: (0, 0)),
            pl.BlockSpec((1, d, d), lambda l, n: (l, 0, n)),
            pl.BlockSpec((1, 1, d), lambda l, n: (l, 0, n)),
        ],
        out_specs=pl.BlockSpec((1, b, d), lambda l, n: (l, 0, n)),
        out_shape=jax.ShapeDtypeStruct((depth, b, d6), F32),
    )(c, w_ada, b_ada.reshape(depth, 1, d6))


def _mixer_kernel(x_ref, mod_ref, gpre_ref, gpost_ref, win_ref, convw_ref, wco_ref, wpg_ref,
                  pscale_ref, wpp_ref, wo_ref, o_ref, uext_ref, pext_ref):
    j = pl.program_id(1)
    tm, d = x_ref.shape[1], x_ref.shape[2]
    d_pool = pext_ref.shape[1]

    @pl.when(j == 0)
    def _():
        uext_ref[0:CONV_HALO, :] = jnp.zeros((CONV_HALO, d), F32)
        pext_ref[0:POOL_HALO, :] = jnp.zeros((POOL_HALO, d_pool), F32)

    x = x_ref[0]
    mod = mod_ref[0]
    sh1, sc1, gt1 = mod[0:1], mod[1:2], mod[2:3]
    h = x * _rms_scale(x) * gpre_ref[...] * (1.0 + sc1) + sh1
    hb = h.astype(BF16)

    def proj(lo, hi):
        return jnp.dot(hb, win_ref[:, lo:hi], preferred_element_type=F32)

    u = proj(d, 2 * d) * proj(2 * d, 3 * d)
    uext_ref[CONV_HALO:CONV_HALO + tm, :] = u
    cw = convw_ref[...]
    conv = (cw[2:3] * u
            + cw[1:2] * uext_ref[CONV_HALO - 1:CONV_HALO - 1 + tm, :]
            + cw[0:1] * uext_ref[CONV_HALO - 2:CONV_HALO - 2 + tm, :])
    uext_ref[0:CONV_HALO, :] = u[tm - CONV_HALO:tm, :]
    y_conv = jnp.dot((proj(0, d) * conv).astype(BF16), wco_ref[...], preferred_element_type=F32)

    up = proj(3 * d, 3 * d + d_pool)
    pext_ref[POOL_HALO:POOL_HALO + tm, :] = up
    pos = j * tm + lax.broadcasted_iota(jnp.int32, (tm, 1), 0)
    zs = []
    for g, w in enumerate(POOL_WINDOWS):
        c0 = g * POOL_GROUP_DIM
        ug = up[:, c0:c0 + POOL_GROUP_DIM]
        acc = ug
        for k in range(1, w):
            acc = acc + pext_ref[POOL_HALO - k:POOL_HALO - k + tm, c0:c0 + POOL_GROUP_DIM]
        inv_cnt = 1.0 / jnp.minimum(pos + 1, w).astype(F32)
        diff = acc * inv_cnt - ug
        zs.append(jnp.dot(diff.astype(BF16), wpg_ref[g], preferred_element_type=F32))
    pext_ref[0:POOL_HALO, :] = up[tm - POOL_HALO:tm, :]
    z = jnp.concatenate(zs, axis=1) * pscale_ref[...]
    y_pool = jnp.dot(z.astype(BF16), wpp_ref[...], preferred_element_type=F32)

    a_conv = proj(3 * d + d_pool, 4 * d + d_pool)
    a_pool = proj(4 * d + d_pool, 5 * d + d_pool)
    merged = jax.nn.sigmoid(a_conv) * y_conv + jax.nn.sigmoid(a_pool) * y_pool
    y = jnp.dot(merged.astype(BF16), wo_ref[...], preferred_element_type=F32)
    o_ref[0] = x + gt1 * (y * _rms_scale(y) * gpost_ref[...])


def _token_mixer(x, mod, g_pre, g_post, w_in, conv_w, w_conv_out, w_pool_group, pool_scale,
                 w_pool_proj, w_o):
    b, s, d = x.shape
    d_in = w_in.shape[1]
    d_pool = w_pool_proj.shape[0]
    tm = min(MIX_TILE, s)
    const2 = lambda bi, j: (0, 0)
    const3 = lambda bi, j: (0, 0, 0)
    return pl.pallas_call(
        _mixer_kernel,
        grid=(b, s // tm),
        in_specs=[
            pl.BlockSpec((1, tm, d), lambda bi, j: (bi, j, 0)),
            pl.BlockSpec((1, 6, d), lambda bi, j: (bi, 0, 0)),
            pl.BlockSpec((1, d), const2),
            pl.BlockSpec((1, d), const2),
            pl.BlockSpec((d, d_in), const2),
            pl.BlockSpec((3, d), const2),
            pl.BlockSpec((d, d), const2),
            pl.BlockSpec(w_pool_group.shape, const3),
            pl.BlockSpec((1, d_pool), const2),
            pl.BlockSpec((d_pool, d), const2),
            pl.BlockSpec((d, d), const2),
        ],
        out_specs=pl.BlockSpec((1, tm, d), lambda bi, j: (bi, j, 0)),
        out_shape=jax.ShapeDtypeStruct(x.shape, F32),
        scratch_shapes=[
            pltpu.VMEM((CONV_HALO + tm, d), F32),
            pltpu.VMEM((POOL_HALO + tm, d_pool), F32),
        ],
        compiler_params=pltpu.CompilerParams(
            dimension_semantics=("arbitrary", "arbitrary"), vmem_limit_bytes=VMEM_LIMIT),
    )(x, mod, g_pre.reshape(1, d), g_post.reshape(1, d), w_in.astype(BF16), conv_w,
      w_conv_out.astype(BF16), w_pool_group.astype(BF16), pool_scale.reshape(1, d_pool),
      w_pool_proj.astype(BF16), w_o.astype(BF16))


def _wins(other, v, tie_i):
    return jnp.where(other > v, 1, 0) + jnp.where(other == v, tie_i, 0)


def _route_kernel(x_ref, mod_ref, g_ref, wrt_ref, bias_ref, wsg_ref, wsu_ref, wsd_ref,
                  h2g_ref, shared_ref, rowl_ref, wl_ref, nch_ref, q0_ref,
                  lg_ref, pre_ref, rho_ref, wd_ref):
    i = pl.program_id(1)
    lc, d = x_ref.shape
    n_sub = lg_ref.shape[0]
    n_chunk = d // LANES

    @pl.when(i < n_sub)
    def _():
        mod = mod_ref[0]
        sh2, sc2 = mod[3:4], mod[4:5]

        @pl.when(i == 0)
        def _():
            h2g_ref[0, 0:n_chunk, :] = jnp.zeros((n_chunk, LANES), F32)

        x = x_ref[...]
        h = x * _rms_scale(x) * g_ref[...] * (1.0 + sc2) + sh2
        for c in range(n_chunk):
            h2g_ref[0, pl.ds((i * lc + 1) * n_chunk + c, lc, stride=n_chunk), :] = (
                h[:, c * LANES:(c + 1) * LANES])
        hb = h.astype(BF16)
        act = (_silu(jnp.dot(hb, wsg_ref[...], preferred_element_type=F32))
               * jnp.dot(hb, wsu_ref[...], preferred_element_type=F32))
        shared_ref[...] = jnp.dot(act.astype(BF16), wsd_ref[...], preferred_element_type=F32)
        lg_ref[i] = lax.dot_general(
            wrt_ref[...], h, (((1,), (1,)), ((), ())), preferred_element_type=F32,
            precision=lax.Precision.HIGHEST)

    @pl.when(i == n_sub)
    def _():
        _route_plan(bias_ref, rowl_ref, wl_ref, nch_ref, q0_ref, lg_ref, pre_ref, rho_ref, wd_ref,
                    lc, n_chunk)


def _route_plan(bias_ref, rowl_ref, wl_ref, nch_ref, q0_ref, lg_ref, pre_ref, rho_ref, wd_ref,
                lc, n_chunk):
    n_sub = lg_ref.shape[0]
    tb = n_sub * lc
    gidx = lax.broadcasted_iota(jnp.int32, (N_GROUPS, lc), 0)
    tie = [None] + [jnp.where(gidx >= r, 1, 0) for r in range(1, N_GROUPS)]
    tri = (lax.broadcasted_iota(jnp.int32, (lc, lc), 0)
           < lax.broadcasted_iota(jnp.int32, (lc, lc), 1)).astype(BF16)
    carry = jnp.zeros((N_EXPERTS, 1), F32)
    neg_inf = jnp.float32(-jnp.inf)
    for ci in range(tb // lc):
        c0 = ci * lc
        s_all = jax.nn.sigmoid(lg_ref[ci])
        aff = [s_all[GROUP_SIZE * jj:GROUP_SIZE * (jj + 1), :] for jj in range(GROUP_SIZE)]
        sel = [aff[jj] + bias_ref[GROUP_SIZE * jj:GROUP_SIZE * (jj + 1), :]
               for jj in range(GROUP_SIZE)]
        m1, m2 = sel[0], jnp.full_like(sel[0], neg_inf)
        for jj in range(1, GROUP_SIZE):
            m2 = jnp.maximum(m2, jnp.minimum(m1, sel[jj]))
            m1 = jnp.maximum(m1, sel[jj])
        gs = m1 + m2
        beaten = jnp.zeros((N_GROUPS, lc), jnp.int32)
        for r in range(1, N_GROUPS):
            other = pltpu.roll(gs, r, axis=0)
            beaten = beaten + _wins(other, gs, tie[r])
        gmask = beaten < TOPK_GROUPS
        masked = [jnp.where(gmask, sel[jj], neg_inf) for jj in range(GROUP_SIZE)]
        rolled = [[masked[jj]] + [pltpu.roll(masked[jj], r, axis=0) for r in range(1, N_GROUPS)]
                  for jj in range(GROUP_SIZE)]
        rho = []
        for jj in range(GROUP_SIZE):
            v = masked[jj]
            cnt = jnp.zeros((N_GROUPS, lc), jnp.int32)
            for j2 in range(GROUP_SIZE):
                for r in range(N_GROUPS):
                    if r == 0 and j2 == jj:
                        continue
                    other = rolled[j2][r]
                    if r == 0:
                        wins = (other >= v) if j2 < jj else (other > v)
                        cnt = cnt + jnp.where(wins, 1, 0)
                    else:
                        cnt = cnt + _wins(other, v, tie[r])
            rho.append(cnt)
        chosen = [rho[jj] < TOP_K for jj in range(GROUP_SIZE)]
        ssum = jnp.zeros((N_GROUPS, lc), F32)
        for jj in range(GROUP_SIZE):
            ssum = ssum + jnp.where(chosen[jj], aff[jj], 0.0)
        ssum = jnp.sum(ssum, axis=0, keepdims=True)
        wdense = [jnp.where(chosen[jj], aff[jj] / ssum * ROUTED_SCALE, 0.0)
                  for jj in range(GROUP_SIZE)]
        chosen_f = jnp.concatenate([c.astype(F32) for c in chosen], axis=0)
        prefix = jnp.dot(chosen_f.astype(BF16), tri, preferred_element_type=F32) + carry
        carry = carry + jnp.sum(chosen_f, axis=1, keepdims=True)
        pre_ref[:, c0:c0 + lc] = prefix
        rho_ref[:, c0:c0 + lc] = jnp.concatenate(rho, axis=0)
        wd_ref[:, c0:c0 + lc] = jnp.concatenate(wdense, axis=0)

    m = MOE_CHUNK
    nch_b = jnp.broadcast_to(jnp.floor((carry + (m - 1)) * (1.0 / m)), (N_EXPERTS, LANES))
    lower = (lax.broadcasted_iota(jnp.int32, (N_EXPERTS, N_EXPERTS), 1)
             < lax.broadcasted_iota(jnp.int32, (N_EXPERTS, N_EXPERTS), 0)).astype(F32)
    q0_b = jnp.dot(lower, nch_b, preferred_element_type=F32, precision=lax.Precision.HIGHEST)
    nch_ref[0] = nch_b.astype(jnp.int32)
    q0_ref[0] = q0_b.astype(jnp.int32)
    offs_col = q0_b[:, 0:1] * m

    nq = rowl_ref.shape[1]
    iota_q = lax.broadcasted_iota(jnp.int32, (nq, lc), 0).astype(F32)
    iota_r = lax.broadcasted_iota(jnp.int32, (m, lc), 0).astype(F32)
    lists = jnp.zeros((nq, 5 * m), F32)
    for ci in range(tb // lc):
        c0 = ci * lc
        dest_dense = pre_ref[:, c0:c0 + lc] + offs_col
        rho_c = rho_ref[:, c0:c0 + lc]
        w_c = wd_ref[:, c0:c0 + lc]
        tokv = (c0 + 1 + lax.broadcasted_iota(jnp.int32, (1, lc), 1)).astype(F32)
        tok_hi = jnp.floor(tokv * (1.0 / 64))
        tok_lo = tokv - 64.0 * tok_hi
        for k in range(TOP_K):
            hit = rho_c == k
            dk = jnp.sum(jnp.where(hit, dest_dense, 0.0), axis=0, keepdims=True)
            wk = jnp.sum(jnp.where(hit, w_c, 0.0), axis=0, keepdims=True)
            qk = jnp.floor(dk * (1.0 / m))
            rk = dk - m * qk
            w_hi = wk.astype(BF16).astype(F32)
            w_mid = (wk - w_hi).astype(BF16).astype(F32)
            w_lo = wk - w_hi - w_mid
            onehot_q = jnp.where(iota_q == qk + 1.0, 1.0, 0.0).astype(BF16)
            rmask = iota_r == rk
            vals = jnp.concatenate(
                [jnp.where(rmask, piece, 0.0) for piece in (tok_hi, tok_lo, w_hi, w_mid, w_lo)],
                axis=0).astype(BF16)
            lists = lists + lax.dot_general(onehot_q, vals, (((1,), (1,)), ((), ())),
                                            preferred_element_type=F32)
    tok = lists[:, 0:m] * 64.0 + lists[:, m:2 * m]
    rowl_ref[0] = (tok * n_chunk).astype(jnp.int32)
    wl_ref[0] = (lists[:, 2 * m:3 * m] + lists[:, 3 * m:4 * m]) + lists[:, 4 * m:5 * m]


def _route(x1, mod, g_pre, w_router, router_bias, w_sh_gate, w_sh_up, w_sh_down, seq, tb):
    t, d = x1.shape
    nb = t // tb
    n_chunk = d // LANES
    d_sh = w_sh_gate.shape[1]
    perm = jnp.arange(N_EXPERTS).reshape(N_GROUPS, GROUP_SIZE).T.reshape(-1)
    wrt = w_router.T[perm]
    bias = router_bias[perm].reshape(N_EXPERTS, 1)
    nq = _num_list_rows(tb)
    lc = min(ROUTE_LANES, tb)
    n_sub = tb // lc
    const2 = lambda bi, i: (0, 0)
    per_block = lambda bi, i: (bi, 0, 0)
    sub_tile = lambda bi, i: (bi * n_sub + jnp.minimum(i, n_sub - 1), 0)
    outs = pl.pallas_call(
        _route_kernel,
        grid=(nb, n_sub + 1),
        in_specs=[
            pl.BlockSpec((lc, d), sub_tile),
            pl.BlockSpec((1, 6, d), lambda bi, i: (bi * tb // seq, 0, 0)),
            pl.BlockSpec((1, d), const2),
            pl.BlockSpec((N_EXPERTS, d), const2),
            pl.BlockSpec((N_EXPERTS, 1), const2),
            pl.BlockSpec((d, d_sh), const2),
            pl.BlockSpec((d, d_sh), const2),
            pl.BlockSpec((d_sh, d), const2),
        ],
        out_specs=[
            pl.BlockSpec((1, (tb + 1) * n_chunk, LANES), per_block),
            pl.BlockSpec((lc, d), sub_tile),
            pl.BlockSpec((1, nq, MOE_CHUNK), per_block),
            pl.BlockSpec((1, nq, MOE_CHUNK), per_block),
            pl.BlockSpec((1, N_EXPERTS, LANES), per_block),
            pl.BlockSpec((1, N_EXPERTS, LANES), per_block),
        ],
        out_shape=[
            jax.ShapeDtypeStruct((nb, (tb + 1) * n_chunk, LANES), F32),
            jax.ShapeDtypeStruct((t, d), F32),
            jax.ShapeDtypeStruct((nb, nq, MOE_CHUNK), jnp.int32),
            jax.ShapeDtypeStruct((nb, nq, MOE_CHUNK), F32),
            jax.ShapeDtypeStruct((nb, N_EXPERTS, LANES), jnp.int32),
            jax.ShapeDtypeStruct((nb, N_EXPERTS, LANES), jnp.int32),
        ],
        scratch_shapes=[
            pltpu.VMEM((n_sub, N_EXPERTS, lc), F32),
            pltpu.VMEM((N_EXPERTS, tb), F32),
            pltpu.VMEM((N_EXPERTS, tb), jnp.int32),
            pltpu.VMEM((N_EXPERTS, tb), F32),
        ],
        compiler_params=pltpu.CompilerParams(
            dimension_semantics=("arbitrary", "arbitrary"), vmem_limit_bytes=VMEM_LIMIT),
    )(x1, mod, g_pre.reshape(1, d), wrt, bias, w_sh_gate.astype(BF16), w_sh_up.astype(BF16),
      w_sh_down.astype(BF16))
    return outs


def _num_list_rows(tb):
    return TOP_K * tb // MOE_CHUNK + N_EXPERTS + SUBLANES


def _moe_kernel(nch_sm, q0_sm, h2g_ref, rowl_hbm, wl_ref, wg_ref, wu_ref, wd_ref, o_ref,
                acc_ref, xt0_ref, xt1_ref, yt0_ref, yt1_ref, rowl_sm, sem):
    bi = pl.program_id(0)
    r = pl.program_id(1)
    tb, d = o_ref.shape
    n_chunk = d // LANES
    m = MOE_CHUNK
    stride = m + SUBLANES
    eye = (lax.broadcasted_iota(jnp.int32, (m, m), 0)
           == lax.broadcasted_iota(jnp.int32, (m, m), 1))

    def gather(lrow, xt_ref):
        for mi in range(m):
            row = pl.multiple_of(rowl_sm[lrow, mi], n_chunk)
            xt_ref[pl.ds(mi, n_chunk, stride=stride), :] = h2g_ref[0, pl.ds(row, n_chunk), :]

    def scatter(lrow, yt_ref):
        for g0 in range(0, m, SUBLANES):
            rows, vals = [], []
            for mi in range(g0, g0 + SUBLANES):
                row = pl.multiple_of(rowl_sm[lrow, mi], n_chunk)
                rows.append(row)
                vals.append(acc_ref[pl.ds(row, n_chunk), :]
                            + yt_ref[pl.ds(mi, n_chunk, stride=stride), :])
            for row, val in zip(rows, vals):
                acc_ref[pl.ds(row, n_chunk), :] = val

    def experts(lrow, xt_ref, yt_ref):
        xs = jnp.concatenate([xt_ref[c * stride:c * stride + m, :] for c in range(n_chunk)],
                             axis=1).astype(BF16)
        act = (_silu(jnp.dot(xs, wg_ref[0], preferred_element_type=F32))
               * jnp.dot(xs, wu_ref[0], preferred_element_type=F32))
        y = jnp.dot(act.astype(BF16), wd_ref[0], preferred_element_type=F32)
        w_row = wl_ref[0, pl.ds(lrow, 1), :]
        w_col = jnp.sum(jnp.where(eye, w_row, 0.0), axis=1, keepdims=True)
        y = y * w_col
        for c in range(n_chunk):
            yt_ref[c * stride:c * stride + m, :] = y[:, c * LANES:(c + 1) * LANES]

    @pl.when(r == 0)
    def _():
        cp = pltpu.make_async_copy(rowl_hbm.at[bi], rowl_sm, sem)
        cp.start()
        acc_ref[...] = jnp.zeros(acc_ref.shape, F32)
        yt1_ref[...] = jnp.zeros(yt1_ref.shape, F32)
        cp.wait()
        gather(1, xt0_ref)

    n_chunks = nch_sm[bi * N_EXPERTS + r]
    q_first = q0_sm[bi * N_EXPERTS + r]

    def step(q, xt_cur, xt_nxt, yt_cur, yt_prv):
        gather(q + 2, xt_nxt)
        experts(q + 1, xt_cur, yt_cur)
        scatter(q, yt_prv)

    def chunk(ci, carry):
        q = q_first + ci
        even = (q & 1) == 0

        @pl.when(even)
        def _():
            step(q, xt0_ref, xt1_ref, yt0_ref, yt1_ref)

        @pl.when(jnp.logical_not(even))
        def _():
            step(q, xt1_ref, xt0_ref, yt1_ref, yt0_ref)
        return carry

    lax.fori_loop(0, n_chunks, chunk, 0)

    @pl.when(r == N_EXPERTS - 1)
    def _():
        q_last = q_first + n_chunks - 1
        last_even = (q_last & 1) == 0

        @pl.when(last_even)
        def _():
            scatter(q_last + 1, yt0_ref)

        @pl.when(jnp.logical_not(last_even))
        def _():
            scatter(q_last + 1, yt1_ref)

        for c in range(n_chunk):
            o_ref[:, c * LANES:(c + 1) * LANES] = acc_ref[pl.ds(n_chunk + c, tb, stride=n_chunk), :]


def _moe(h2g, rowl, wl, nch, q0, w_gate, w_up, w_down, t, d, tb):
    nb = t // tb
    n_chunk = d // LANES
    d_e = w_gate.shape[2]
    m = MOE_CHUNK
    stride = m + SUBLANES
    nq = _num_list_rows(tb)

    def expert_of(r):
        return (r % N_GROUPS) * GROUP_SIZE + r // N_GROUPS

    grid_spec = pltpu.PrefetchScalarGridSpec(
        num_scalar_prefetch=2,
        grid=(nb, N_EXPERTS),
        in_specs=[
            pl.BlockSpec((1, (tb + 1) * n_chunk, LANES), lambda bi, r, c, o: (bi, 0, 0)),
            pl.BlockSpec(memory_space=pl.ANY),
            pl.BlockSpec((1, nq, m), lambda bi, r, c, o: (bi, 0, 0)),
            pl.BlockSpec((1, d, d_e), lambda bi, r, c, o: (expert_of(r), 0, 0)),
            pl.BlockSpec((1, d, d_e), lambda bi, r, c, o: (expert_of(r), 0, 0)),
            pl.BlockSpec((1, d_e, d), lambda bi, r, c, o: (expert_of(r), 0, 0)),
        ],
        out_specs=pl.BlockSpec((tb, d), lambda bi, r, c, o: (bi, 0)),
        scratch_shapes=[
            pltpu.VMEM(((tb + 1) * n_chunk, LANES), F32),
            pltpu.VMEM((n_chunk * stride, LANES), F32),
            pltpu.VMEM((n_chunk * stride, LANES), F32),
            pltpu.VMEM((n_chunk * stride, LANES), F32),
            pltpu.VMEM((n_chunk * stride, LANES), F32),
            pltpu.SMEM((nq, m), jnp.int32),
            pltpu.SemaphoreType.DMA,
        ],
    )
    return pl.pallas_call(
        _moe_kernel,
        grid_spec=grid_spec,
        out_shape=jax.ShapeDtypeStruct((t, d), F32),
        compiler_params=pltpu.CompilerParams(
            dimension_semantics=("arbitrary", "arbitrary"), vmem_limit_bytes=VMEM_LIMIT),
    )(nch, q0, h2g, rowl, wl, w_gate, w_up, w_down)


def _epilogue_kernel(x_ref, routed_ref, shared_ref, mod_ref, g_ref, o_ref):
    gt2 = mod_ref[0][5:6]
    y = routed_ref[...] + shared_ref[...]
    o_ref[...] = x_ref[...] + gt2 * (y * _rms_scale(y) * g_ref[...])


def _epilogue(x1, routed, shared, mod, g_post, seq):
    t, d = x1.shape
    te = min(EPI_TILE, seq)
    row = pl.BlockSpec((te, d), lambda i: (i, 0))
    return pl.pallas_call(
        _epilogue_kernel,
        grid=(t // te,),
        in_specs=[row, row, row,
                  pl.BlockSpec((1, 6, d), lambda i: (i * te // seq, 0, 0)),
                  pl.BlockSpec((1, d), lambda i: (0, 0))],
        out_specs=row,
        out_shape=jax.ShapeDtypeStruct((t, d), F32),
    )(x1, routed, shared, mod, g_post.reshape(1, d))


def kernel(x, c, w_ada, b_ada, g_pre_mix, g_post_mix, g_pre_ffn, g_post_ffn, w_in, conv_w,
           w_conv_out, w_pool_group, pool_scale, w_pool_proj, w_o, w_router, router_bias,
           w_exp_gate, w_exp_up, w_exp_down, w_sh_gate, w_sh_up, w_sh_down):
    b, s, d = x.shape
    depth = w_ada.shape[0]
    t = b * s
    tb = min(MOE_BLOCK, s)
    mods = _ada_mod(c, w_ada, b_ada).reshape(depth, b, 6, d)
    for l in range(depth):
        mod = mods[l]
        x = _token_mixer(x, mod, g_pre_mix[l], g_post_mix[l], w_in[l], conv_w[l], w_conv_out[l],
                         w_pool_group[l], pool_scale[l], w_pool_proj[l], w_o[l])
        x1 = x.reshape(t, d)
        h2g, shared, rowl, wl, nch, q0 = _route(
            x1, mod, g_pre_ffn[l], w_router[l], router_bias[l], w_sh_gate[l], w_sh_up[l],
            w_sh_down[l], s, tb)
        routed = _moe(h2g, rowl, wl, nch[:, :, 0].reshape(-1), q0[:, :, 0].reshape(-1),
                      w_exp_gate[l].astype(BF16), w_exp_up[l].astype(BF16),
                      w_exp_down[l].astype(BF16), t, d, tb)
        x = _epilogue(x1, routed, shared, mod, g_post_ffn[l], s).reshape(b, s, d)
    return x
```

```python
import jax
import jax.numpy as jnp
from jax import lax
from jax.experimental import pallas as pl
from jax.experimental.pallas import tpu as pltpu

F32 = jnp.float32
BF16 = jnp.bfloat16

EPS = 1e-6
POOL_WINDOWS = (2, 4, 8, 16)
POOL_GROUP_DIM = 128
N_EXPERTS = 64
N_GROUPS = 8
GROUP_SIZE = 8
TOPK_GROUPS = 4
TOP_K = 8
ROUTED_SCALE = 2.5

LANES = 128
SUBLANES = 8
CONV_HALO = 8
POOL_HALO = 16
VMEM_LIMIT = 56 * 1024 * 1024

MIX_TILE = 256
ROUTE_LANES = 512
MOE_BLOCK = 2048
MOE_CHUNK = 128
LIST_LEAD = 2
EPI_TILE = 512


def _silu(v):
    return v * jax.nn.sigmoid(v)


def _rms_scale(v):
    return lax.rsqrt(jnp.mean(v * v, axis=-1, keepdims=True) + EPS)


def _ada_kernel(c_ref, w_ref, b_ref, o_ref):
    cond = _silu(c_ref[...])
    o_ref[0] = jnp.dot(cond, w_ref[0], preferred_element_type=F32,
                       precision=lax.Precision.HIGHEST) + b_ref[0]


def _ada_mod(c, w_ada, b_ada):
    depth, d, d6 = w_ada.shape
    b = c.shape[0]
    n_col = d6 // d
    return pl.pallas_call(
        _ada_kernel,
        grid=(depth, n_col),
        in_specs=[
            pl.BlockSpec((b, d), lambda l, n: (0, 0)),
            pl.BlockSpec((1, d, d), lambda l, n: (l, 0, n)),
            pl.BlockSpec((1, 1, d), lambda l, n: (l, 0, n)),
        ],
        out_specs=pl.BlockSpec((1, b, d), lambda l, n: (l, 0, n)),
        out_shape=jax.ShapeDtypeStruct((depth, b, d6), F32),
    )(c, w_ada, b_ada.reshape(depth, 1, d6))


def _mixer_kernel(x_ref, mod_ref, gpre_ref, gpost_ref, win_ref, convw_ref, wco_ref, wpg_ref,
                  pscale_ref, wpp_ref, wo_ref, o_ref, uext_ref, pext_ref):
    j = pl.program_id(1)
    tm, d = x_ref.shape[1], x_ref.shape[2]
    d_pool = pext_ref.shape[1]

    @pl.when(j == 0)
    def _():
        uext_ref[0:CONV_HALO, :] = jnp.zeros((CONV_HALO, d), F32)
        pext_ref[0:POOL_HALO, :] = jnp.zeros((POOL_HALO, d_pool), F32)

    x = x_ref[0]
    mod = mod_ref[0]
    sh1, sc1, gt1 = mod[0:1], mod[1:2], mod[2:3]
    h = x * _rms_scale(x) * gpre_ref[...] * (1.0 + sc1) + sh1
    hb = h.astype(BF16)

    def proj(lo, hi):
        return jnp.dot(hb, win_ref[:, lo:hi], preferred_element_type=F32)

    u = proj(d, 2 * d) * proj(2 * d, 3 * d)
    uext_ref[CONV_HALO:CONV_HALO + tm, :] = u
    cw = convw_ref[...]
    conv = (cw[2:3] * u
            + cw[1:2] * uext_ref[CONV_HALO - 1:CONV_HALO - 1 + tm, :]
            + cw[0:1] * uext_ref[CONV_HALO - 2:CONV_HALO - 2 + tm, :])
    uext_ref[0:CONV_HALO, :] = u[tm - CONV_HALO:tm, :]
    y_conv = jnp.dot((proj(0, d) * conv).astype(BF16), wco_ref[...], preferred_element_type=F32)

    up = proj(3 * d, 3 * d + d_pool)
    pext_ref[POOL_HALO:POOL_HALO + tm, :] = up
    pos = j * tm + lax.broadcasted_iota(jnp.int32, (tm, 1), 0)
    zs = []
    for g, w in enumerate(POOL_WINDOWS):
        c0 = g * POOL_GROUP_DIM
        ug = up[:, c0:c0 + POOL_GROUP_DIM]
        acc = ug
        for k in range(1, w):
            acc = acc + pext_ref[POOL_HALO - k:POOL_HALO - k + tm, c0:c0 + POOL_GROUP_DIM]
        inv_cnt = 1.0 / jnp.minimum(pos + 1, w).astype(F32)
        diff = acc * inv_cnt - ug
        zs.append(jnp.dot(diff.astype(BF16), wpg_ref[g], preferred_element_type=F32))
    pext_ref[0:POOL_HALO, :] = up[tm - POOL_HALO:tm, :]
    z = jnp.concatenate(zs, axis=1) * pscale_ref[...]
    y_pool = jnp.dot(z.astype(BF16), wpp_ref[...], preferred_element_type=F32)

    a_conv = proj(3 * d + d_pool, 4 * d + d_pool)
    a_pool = proj(4 * d + d_pool, 5 * d + d_pool)
    merged = jax.nn.sigmoid(a_conv) * y_conv + jax.nn.sigmoid(a_pool) * y_pool
    y = jnp.dot(merged.astype(BF16), wo_ref[...], preferred_element_type=F32)
    o_ref[0] = x + gt1 * (y * _rms_scale(y) * gpost_ref[...])


def _token_mixer(x, mod, g_pre, g_post, w_in, conv_w, w_conv_out, w_pool_group, pool_scale,
                 w_pool_proj, w_o):
    b, s, d = x.shape
    d_in = w_in.shape[1]
    d_pool = w_pool_proj.shape[0]
    tm = min(MIX_TILE, s)
    const2 = lambda bi, j: (0, 0)
    const3 = lambda bi, j: (0, 0, 0)
    return pl.pallas_call(
        _mixer_kernel,
        grid=(b, s // tm),
        in_specs=[
            pl.BlockSpec((1, tm, d), lambda bi, j: (bi, j, 0)),
            pl.BlockSpec((1, 6, d), lambda bi, j: (bi, 0, 0)),
            pl.BlockSpec((1, d), const2),
            pl.BlockSpec((1, d), const2),
            pl.BlockSpec((d, d_in), const2),
            pl.BlockSpec((3, d), const2),
            pl.BlockSpec((d, d), const2),
            pl.BlockSpec(w_pool_group.shape, const3),
            pl.BlockSpec((1, d_pool), const2),
            pl.BlockSpec((d_pool, d), const2),
            pl.BlockSpec((d, d), const2),
        ],
        out_specs=pl.BlockSpec((1, tm, d), lambda bi, j: (bi, j, 0)),
        out_shape=jax.ShapeDtypeStruct(x.shape, F32),
        scratch_shapes=[
            pltpu.VMEM((CONV_HALO + tm, d), F32),
            pltpu.VMEM((POOL_HALO + tm, d_pool), F32),
        ],
        compiler_params=pltpu.CompilerParams(
            dimension_semantics=("arbitrary", "arbitrary"), vmem_limit_bytes=VMEM_LIMIT),
    )(x, mod, g_pre.reshape(1, d), g_post.reshape(1, d), w_in.astype(BF16), conv_w,
      w_conv_out.astype(BF16), w_pool_group.astype(BF16), pool_scale.reshape(1, d_pool),
      w_pool_proj.astype(BF16), w_o.astype(BF16))


def _wins(other, v, tie_i):
    return jnp.where(other > v, 1, 0) + jnp.where(other == v, tie_i, 0)


def _route_kernel(x_ref, mod_ref, g_ref, wrt_ref, bias_ref, wsg_ref, wsu_ref, wsd_ref,
                  h2g_ref, shared_ref, rowl_ref, wl_ref, nch_ref, q0_ref,
                  lg_ref, pre_ref, rho_ref, wd_ref):
    i = pl.program_id(1)
    lc, d = x_ref.shape
    n_sub = lg_ref.shape[0]
    n_chunk = d // LANES

    @pl.when(i < n_sub)
    def _():
        mod = mod_ref[0]
        sh2, sc2 = mod[3:4], mod[4:5]

        @pl.when(i == 0)
        def _():
            h2g_ref[0, 0:n_chunk, :] = jnp.zeros((n_chunk, LANES), F32)

        x = x_ref[...]
        h = x * _rms_scale(x) * g_ref[...] * (1.0 + sc2) + sh2
        for c in range(n_chunk):
            h2g_ref[0, pl.ds((i * lc + 1) * n_chunk + c, lc, stride=n_chunk), :] = (
                h[:, c * LANES:(c + 1) * LANES])
        hb = h.astype(BF16)
        act = (_silu(jnp.dot(hb, wsg_ref[...], preferred_element_type=F32))
               * jnp.dot(hb, wsu_ref[...], preferred_element_type=F32))
        shared_ref[...] = jnp.dot(act.astype(BF16), wsd_ref[...], preferred_element_type=F32)
        lg_ref[i] = lax.dot_general(
            wrt_ref[...], h, (((1,), (1,)), ((), ())), preferred_element_type=F32,
            precision=lax.Precision.HIGHEST)

    @pl.when(i == n_sub)
    def _():
        _route_plan(bias_ref, rowl_ref, wl_ref, nch_ref, q0_ref, lg_ref, pre_ref, rho_ref, wd_ref,
                    lc, n_chunk)


def _route_plan(bias_ref, rowl_ref, wl_ref, nch_ref, q0_ref, lg_ref, pre_ref, rho_ref, wd_ref,
                lc, n_chunk):
    n_sub = lg_ref.shape[0]
    tb = n_sub * lc
    gidx = lax.broadcasted_iota(jnp.int32, (N_GROUPS, lc), 0)
    tie = [None] + [jnp.where(gidx >= r, 1, 0) for r in range(1, N_GROUPS)]
    tri = (lax.broadcasted_iota(jnp.int32, (lc, lc), 0)
           < lax.broadcasted_iota(jnp.int32, (lc, lc), 1)).astype(BF16)
    carry = jnp.zeros((N_EXPERTS, 1), F32)
    neg_inf = jnp.float32(-jnp.inf)
    for ci in range(tb // lc):
        c0 = ci * lc
        s_all = jax.nn.sigmoid(lg_ref[ci])
        aff = [s_all[GROUP_SIZE * jj:GROUP_SIZE * (jj + 1), :] for jj in range(GROUP_SIZE)]
        sel = [aff[jj] + bias_ref[GROUP_SIZE * jj:GROUP_SIZE * (jj + 1), :]
               for jj in range(GROUP_SIZE)]
        m1, m2 = sel[0], jnp.full_like(sel[0], neg_inf)
        for jj in range(1, GROUP_SIZE):
            m2 = jnp.maximum(m2, jnp.minimum(m1, sel[jj]))
            m1 = jnp.maximum(m1, sel[jj])
        gs = m1 + m2
        beaten = jnp.zeros((N_GROUPS, lc), jnp.int32)
        for r in range(1, N_GROUPS):
            other = pltpu.roll(gs, r, axis=0)
            beaten = beaten + _wins(other, gs, tie[r])
        gmask = beaten < TOPK_GROUPS
        masked = [jnp.where(gmask, sel[jj], neg_inf) for jj in range(GROUP_SIZE)]
        rolled = [[masked[jj]] + [pltpu.roll(masked[jj], r, axis=0) for r in range(1, N_GROUPS)]
                  for jj in range(GROUP_SIZE)]
        rho = []
        for jj in range(GROUP_SIZE):
            v = masked[jj]
            cnt = jnp.zeros((N_GROUPS, lc), jnp.int32)
            for j2 in range(GROUP_SIZE):
                for r in range(N_GROUPS):
                    if r == 0 and j2 == jj:
                        continue
                    other = rolled[j2][r]
                    if r == 0:
                        wins = (other >= v) if j2 < jj else (other > v)
                        cnt = cnt + jnp.where(wins, 1, 0)
                    else:
                        cnt = cnt + _wins(other, v, tie[r])
            rho.append(cnt)
        chosen = [rho[jj] < TOP_K for jj in range(GROUP_SIZE)]
        ssum = jnp.zeros((N_GROUPS, lc), F32)
        for jj in range(GROUP_SIZE):
            ssum = ssum + jnp.where(chosen[jj], aff[jj], 0.0)
        ssum = jnp.sum(ssum, axis=0, keepdims=True)
        wdense = [jnp.where(chosen[jj], aff[jj] / ssum * ROUTED_SCALE, 0.0)
                  for jj in range(GROUP_SIZE)]
        chosen_f = jnp.concatenate([c.astype(F32) for c in chosen], axis=0)
        prefix = jnp.dot(chosen_f.astype(BF16), tri, preferred_element_type=F32) + carry
        carry = carry + jnp.sum(chosen_f, axis=1, keepdims=True)
        pre_ref[:, c0:c0 + lc] = prefix
        rho_ref[:, c0:c0 + lc] = jnp.concatenate(rho, axis=0)
        wd_ref[:, c0:c0 + lc] = jnp.concatenate(wdense, axis=0)

    m = MOE_CHUNK
    nch_b = jnp.broadcast_to(jnp.floor((carry + (m - 1)) * (1.0 / m)), (N_EXPERTS, LANES))
    lower = (lax.broadcasted_iota(jnp.int32, (N_EXPERTS, N_EXPERTS), 1)
             < lax.broadcasted_iota(jnp.int32, (N_EXPERTS, N_EXPERTS), 0)).astype(F32)
    q0_b = jnp.dot(lower, nch_b, preferred_element_type=F32, precision=lax.Precision.HIGHEST)
    nch_ref[0] = nch_b.astype(jnp.int32)
    q0_ref[0] = q0_b.astype(jnp.int32)
    offs_col = q0_b[:, 0:1] * m

    nq = rowl_ref.shape[1]
    iota_q = lax.broadcasted_iota(jnp.int32, (nq, lc), 0).astype(F32)
    iota_r = lax.broadcasted_iota(jnp.int32, (m, lc), 0).astype(F32)
    lists = jnp.zeros((nq, 5 * m), F32)
    for ci in range(tb // lc):
        c0 = ci * lc
        dest_dense = pre_ref[:, c0:c0 + lc] + offs_col
        rho_c = rho_ref[:, c0:c0 + lc]
        w_c = wd_ref[:, c0:c0 + lc]
        tokv = (c0 + 1 + lax.broadcasted_iota(jnp.int32, (1, lc), 1)).astype(F32)
        tok_hi = jnp.floor(tokv * (1.0 / 64))
        tok_lo = tokv - 64.0 * tok_hi
        for k in range(TOP_K):
            hit = rho_c == k
            dk = jnp.sum(jnp.where(hit, dest_dense, 0.0), axis=0, keepdims=True)
            wk = jnp.sum(jnp.where(hit, w_c, 0.0), axis=0, keepdims=True)
            qk = jnp.floor(dk * (1.0 / m))
            rk = dk - m * qk
            w_hi = wk.astype(BF16).astype(F32)
            w_mid = (wk - w_hi).astype(BF16).astype(F32)
            w_lo = wk - w_hi - w_mid
            onehot_q = jnp.where(iota_q == qk + LIST_LEAD, 1.0, 0.0).astype(BF16)
            rmask = iota_r == rk
            vals = jnp.concatenate(
                [jnp.where(rmask, piece, 0.0) for piece in (tok_hi, tok_lo, w_hi, w_mid, w_lo)],
                axis=0).astype(BF16)
            lists = lists + lax.dot_general(onehot_q, vals, (((1,), (1,)), ((), ())),
                                            preferred_element_type=F32)
    tok = lists[:, 0:m] * 64.0 + lists[:, m:2 * m]
    rowl_ref[0] = (tok * n_chunk).astype(jnp.int32)
    wl_ref[0] = (lists[:, 2 * m:3 * m] + lists[:, 3 * m:4 * m]) + lists[:, 4 * m:5 * m]


def _route(x1, mod, g_pre, w_router, router_bias, w_sh_gate, w_sh_up, w_sh_down, seq, tb):
    t, d = x1.shape
    nb = t // tb
    n_chunk = d // LANES
    d_sh = w_sh_gate.shape[1]
    perm = jnp.arange(N_EXPERTS).reshape(N_GROUPS, GROUP_SIZE).T.reshape(-1)
    wrt = w_router.T[perm]
    bias = router_bias[perm].reshape(N_EXPERTS, 1)
    nq = _num_list_rows(tb)
    lc = min(ROUTE_LANES, tb)
    n_sub = tb // lc
    const2 = lambda bi, i: (0, 0)
    per_block = lambda bi, i: (bi, 0, 0)
    sub_tile = lambda bi, i: (bi * n_sub + jnp.minimum(i, n_sub - 1), 0)
    outs = pl.pallas_call(
        _route_kernel,
        grid=(nb, n_sub + 1),
        in_specs=[
            pl.BlockSpec((lc, d), sub_tile),
            pl.BlockSpec((1, 6, d), lambda bi, i: (bi * tb // seq, 0, 0)),
            pl.BlockSpec((1, d), const2),
            pl.BlockSpec((N_EXPERTS, d), const2),
            pl.BlockSpec((N_EXPERTS, 1), const2),
            pl.BlockSpec((d, d_sh), const2),
            pl.BlockSpec((d, d_sh), const2),
            pl.BlockSpec((d_sh, d), const2),
        ],
        out_specs=[
            pl.BlockSpec((1, (tb + 1) * n_chunk, LANES), per_block),
            pl.BlockSpec((lc, d), sub_tile),
            pl.BlockSpec((1, nq, MOE_CHUNK), per_block),
            pl.BlockSpec((1, nq, MOE_CHUNK), per_block),
            pl.BlockSpec((1, N_EXPERTS, LANES), per_block),
            pl.BlockSpec((1, N_EXPERTS, LANES), per_block),
        ],
        out_shape=[
            jax.ShapeDtypeStruct((nb, (tb + 1) * n_chunk, LANES), F32),
            jax.ShapeDtypeStruct((t, d), F32),
            jax.ShapeDtypeStruct((nb, nq, MOE_CHUNK), jnp.int32),
            jax.ShapeDtypeStruct((nb, nq, MOE_CHUNK), F32),
            jax.ShapeDtypeStruct((nb, N_EXPERTS, LANES), jnp.int32),
            jax.ShapeDtypeStruct((nb, N_EXPERTS, LANES), jnp.int32),
        ],
        scratch_shapes=[
            pltpu.VMEM((n_sub, N_EXPERTS, lc), F32),
            pltpu.VMEM((N_EXPERTS, tb), F32),
            pltpu.VMEM((N_EXPERTS, tb), jnp.int32),
            pltpu.VMEM((N_EXPERTS, tb), F32),
        ],
        compiler_params=pltpu.CompilerParams(
            dimension_semantics=("arbitrary", "arbitrary"), vmem_limit_bytes=VMEM_LIMIT),
    )(x1, mod, g_pre.reshape(1, d), wrt, bias, w_sh_gate.astype(BF16), w_sh_up.astype(BF16),
      w_sh_down.astype(BF16))
    return outs


def _num_list_rows(tb):
    return TOP_K * tb // MOE_CHUNK + N_EXPERTS + SUBLANES


def _moe_kernel(nch_sm, q0_sm, prev_sm, h2g_ref, rowl_hbm, wl_ref, wg_ref, wu_ref, wd_ref,
                wdp_ref, o_ref, acc_ref, xt0_ref, xt1_ref, yt0_ref, yt1_ref, act0_ref, act1_ref,
                rowl_sm, sem):
    bi = pl.program_id(0)
    r = pl.program_id(1)
    tb, d = o_ref.shape
    n_chunk = d // LANES
    m = MOE_CHUNK
    stride = m + SUBLANES
    eye = (lax.broadcasted_iota(jnp.int32, (m, m), 0)
           == lax.broadcasted_iota(jnp.int32, (m, m), 1))
    xts, yts, acts = (xt0_ref, xt1_ref), (yt0_ref, yt1_ref), (act0_ref, act1_ref)

    def gather(lrow, xt_ref):
        for mi in range(m):
            row = pl.multiple_of(rowl_sm[lrow, mi], n_chunk)
            xt_ref[pl.ds(mi, n_chunk, stride=stride), :] = h2g_ref[0, pl.ds(row, n_chunk), :]

    def scatter(lrow, yt_ref):
        for g0 in range(0, m, SUBLANES):
            rows, vals = [], []
            for mi in range(g0, g0 + SUBLANES):
                row = pl.multiple_of(rowl_sm[lrow, mi], n_chunk)
                rows.append(row)
                vals.append(acc_ref[pl.ds(row, n_chunk), :]
                            + yt_ref[pl.ds(mi, n_chunk, stride=stride), :])
            for row, val in zip(rows, vals):
                acc_ref[pl.ds(row, n_chunk), :] = val

    def gate_up(xt_ref, act_ref):
        xs = jnp.concatenate([xt_ref[c * stride:c * stride + m, :] for c in range(n_chunk)],
                             axis=1).astype(BF16)
        act = (_silu(jnp.dot(xs, wg_ref[0], preferred_element_type=F32))
               * jnp.dot(xs, wu_ref[0], preferred_element_type=F32))
        act_ref[...] = act.astype(BF16)

    def down(lrow, act_ref, w_ref, yt_ref):
        y = jnp.dot(act_ref[...], w_ref[0], preferred_element_type=F32)
        w_row = wl_ref[0, pl.ds(lrow, 1), :]
        w_col = jnp.sum(jnp.where(eye, w_row, 0.0), axis=1, keepdims=True)
        y = y * w_col
        for c in range(n_chunk):
            yt_ref[c * stride:c * stride + m, :] = y[:, c * LANES:(c + 1) * LANES]

    def step(q, par, w_down_ref):
        cur, oth = par, 1 - par
        gather(q + 1 + LIST_LEAD, xts[oth])
        gate_up(xts[cur], acts[cur])
        down(q - 1 + LIST_LEAD, acts[oth], w_down_ref, yts[oth])
        scatter(q - 2 + LIST_LEAD, yts[cur])

    def by_parity(q, fn):
        for par in (0, 1):
            @pl.when((q & 1) == par)
            def _():
                fn(par)

    @pl.when(r == 0)
    def _():
        cp = pltpu.make_async_copy(rowl_hbm.at[bi], rowl_sm, sem)
        cp.start()
        acc_ref[...] = jnp.zeros(acc_ref.shape, F32)
        act1_ref[...] = jnp.zeros(act1_ref.shape, BF16)
        yt0_ref[...] = jnp.zeros(yt0_ref.shape, F32)
        cp.wait()
        gather(LIST_LEAD, xt0_ref)

    n_chunks = nch_sm[bi * N_EXPERTS + r]
    q_first = q0_sm[bi * N_EXPERTS + r]

    @pl.when(n_chunks > 0)
    def _():
        by_parity(q_first, lambda par: step(q_first, par, wdp_ref))

    def chunk(ci, carry):
        q = q_first + ci
        by_parity(q, lambda par: step(q, par, wd_ref))
        return carry

    lax.fori_loop(1, n_chunks, chunk, 0)

    @pl.when(r == N_EXPERTS - 1)
    def _():
        q_last = q_first + n_chunks - 1

        def drain(par, w_down_ref):
            scatter(q_last - 1 + LIST_LEAD, yts[1 - par])
            down(q_last + LIST_LEAD, acts[par], w_down_ref, yts[par])
            scatter(q_last + LIST_LEAD, yts[par])

        @pl.when(n_chunks > 0)
        def _():
            by_parity(q_last, lambda par: drain(par, wd_ref))

        @pl.when(n_chunks == 0)
        def _():
            by_parity(q_last, lambda par: drain(par, wdp_ref))

        for c in range(n_chunk):
            o_ref[:, c * LANES:(c + 1) * LANES] = acc_ref[pl.ds(n_chunk + c, tb, stride=n_chunk), :]


def _moe(h2g, rowl, wl, nch, q0, w_gate, w_up, w_down, t, d, tb):
    nb = t // tb
    n_chunk = d // LANES
    d_e = w_gate.shape[2]
    m = MOE_CHUNK
    stride = m + SUBLANES
    nq = _num_list_rows(tb)

    def expert_of(r):
        return (r % N_GROUPS) * GROUP_SIZE + r // N_GROUPS

    rows = jnp.where(nch > 0, jnp.arange(N_EXPERTS, dtype=jnp.int32)[None, :], -1)
    last = lax.cummax(rows, axis=1)
    prev = jnp.maximum(jnp.concatenate([jnp.full((nb, 1), -1, jnp.int32), last[:, :-1]], axis=1), 0)

    cur_w = lambda bi, r, c, o, p: (expert_of(r), 0, 0)
    prev_w = lambda bi, r, c, o, p: (expert_of(p[bi * N_EXPERTS + r]), 0, 0)
    per_block = lambda bi, r, c, o, p: (bi, 0, 0)
    staging = pltpu.VMEM((n_chunk * stride, LANES), F32)
    grid_spec = pltpu.PrefetchScalarGridSpec(
        num_scalar_prefetch=3,
        grid=(nb, N_EXPERTS),
        in_specs=[
            pl.BlockSpec((1, (tb + 1) * n_chunk, LANES), per_block),
            pl.BlockSpec(memory_space=pl.ANY),
            pl.BlockSpec((1, nq, m), per_block),
            pl.BlockSpec((1, d, d_e), cur_w),
            pl.BlockSpec((1, d, d_e), cur_w),
            pl.BlockSpec((1, d_e, d), cur_w),
            pl.BlockSpec((1, d_e, d), prev_w),
        ],
        out_specs=pl.BlockSpec((tb, d), lambda bi, r, c, o, p: (bi, 0)),
        scratch_shapes=[
            pltpu.VMEM(((tb + 1) * n_chunk, LANES), F32),
            staging, staging, staging, staging,
            pltpu.VMEM((m, d_e), BF16),
            pltpu.VMEM((m, d_e), BF16),
            pltpu.SMEM((nq, m), jnp.int32),
            pltpu.SemaphoreType.DMA,
        ],
    )
    return pl.pallas_call(
        _moe_kernel,
        grid_spec=grid_spec,
        out_shape=jax.ShapeDtypeStruct((t, d), F32),
        compiler_params=pltpu.CompilerParams(
            dimension_semantics=("arbitrary", "arbitrary"), vmem_limit_bytes=VMEM_LIMIT),
    )(nch.reshape(-1), q0.reshape(-1), prev.reshape(-1), h2g, rowl, wl, w_gate, w_up, w_down,
      w_down)


def _epilogue_kernel(x_ref, routed_ref, shared_ref, mod_ref, g_ref, o_ref):
    gt2 = mod_ref[0][5:6]
    y = routed_ref[...] + shared_ref[...]
    o_ref[...] = x_ref[...] + gt2 * (y * _rms_scale(y) * g_ref[...])


def _epilogue(x1, routed, shared, mod, g_post, seq):
    t, d = x1.shape
    te = min(EPI_TILE, seq)
    row = pl.BlockSpec((te, d), lambda i: (i, 0))
    return pl.pallas_call(
        _epilogue_kernel,
        grid=(t // te,),
        in_specs=[row, row, row,
                  pl.BlockSpec((1, 6, d), lambda i: (i * te // seq, 0, 0)),
                  pl.BlockSpec((1, d), lambda i: (0, 0))],
        out_specs=row,
        out_shape=jax.ShapeDtypeStruct((t, d), F32),
    )(x1, routed, shared, mod, g_post.reshape(1, d))


def kernel(x, c, w_ada, b_ada, g_pre_mix, g_post_mix, g_pre_ffn, g_post_ffn, w_in, conv_w,
           w_conv_out, w_pool_group, pool_scale, w_pool_proj, w_o, w_router, router_bias,
           w_exp_gate, w_exp_up, w_exp_down, w_sh_gate, w_sh_up, w_sh_down):
    b, s, d = x.shape
    depth = w_ada.shape[0]
    t = b * s
    tb = min(MOE_BLOCK, s)
    mods = _ada_mod(c, w_ada, b_ada).reshape(depth, b, 6, d)
    for l in range(depth):
        mod = mods[l]
        x = _token_mixer(x, mod, g_pre_mix[l], g_post_mix[l], w_in[l], conv_w[l], w_conv_out[l],
                         w_pool_group[l], pool_scale[l], w_pool_proj[l], w_o[l])
        x1 = x.reshape(t, d)
        h2g, shared, rowl, wl, nch, q0 = _route(
            x1, mod, g_pre_ffn[l], w_router[l], router_bias[l], w_sh_gate[l], w_sh_up[l],
            w_sh_down[l], s, tb)
        routed = _moe(h2g, rowl, wl, nch[:, :, 0], q0[:, :, 0],
                      w_exp_gate[l].astype(BF16), w_exp_up[l].astype(BF16),
                      w_exp_down[l].astype(BF16), t, d, tb)
        x = _epilogue(x1, routed, shared, mod, g_post_ffn[l], s).reshape(b, s, d)
    return x
```

```python
import jax
import jax.numpy as jnp
from jax import lax
from jax.experimental import pallas as pl
from jax.experimental.pallas import tpu as pltpu

F32 = jnp.float32
BF16 = jnp.bfloat16

EPS = 1e-6
POOL_WINDOWS = (2, 4, 8, 16)
POOL_GROUP_DIM = 128
N_EXPERTS = 64
N_GROUPS = 8
GROUP_SIZE = 8
TOPK_GROUPS = 4
TOP_K = 8
ROUTED_SCALE = 2.5

LANES = 128
SUBLANES = 8
CONV_HALO = 8
POOL_HALO = 16
VMEM_LIMIT = 56 * 1024 * 1024

MIX_TILE = 256
ROUTE_LANES = 512
MOE_BLOCK = 4096
MOE_CHUNK = 128
LIST_LEAD = 2
EPI_TILE = 512


def _silu(v):
    return v * jax.nn.sigmoid(v)


def _rms_scale(v):
    return lax.rsqrt(jnp.mean(v * v, axis=-1, keepdims=True) + EPS)


def _ada_kernel(c_ref, w_ref, b_ref, o_ref):
    cond = _silu(c_ref[...])
    o_ref[0] = jnp.dot(cond, w_ref[0], preferred_element_type=F32,
                       precision=lax.Precision.HIGHEST) + b_ref[0]


def _ada_mod(c, w_ada, b_ada):
    depth, d, d6 = w_ada.shape
    b = c.shape[0]
    n_col = d6 // d
    return pl.pallas_call(
        _ada_kernel,
        grid=(depth, n_col),
        in_specs=[
            pl.BlockSpec((b, d), lambda l, n: (0, 0)),
            pl.BlockSpec((1, d, d), lambda l, n: (l, 0, n)),
            pl.BlockSpec((1, 1, d), lambda l, n: (l, 0, n)),
        ],
        out_specs=pl.BlockSpec((1, b, d), lambda l, n: (l, 0, n)),
        out_shape=jax.ShapeDtypeStruct((depth, b, d6), F32),
    )(c, w_ada, b_ada.reshape(depth, 1, d6))


def _mixer_kernel(x_ref, mod_ref, gpre_ref, gpost_ref, win_ref, convw_ref, wco_ref, wpg_ref,
                  pscale_ref, wpp_ref, wo_ref, o_ref, uext_ref, pext_ref):
    j = pl.program_id(1)
    tm, d = x_ref.shape[1], x_ref.shape[2]
    d_pool = pext_ref.shape[1]

    @pl.when(j == 0)
    def _():
        uext_ref[0:CONV_HALO, :] = jnp.zeros((CONV_HALO, d), F32)
        pext_ref[0:POOL_HALO, :] = jnp.zeros((POOL_HALO, d_pool), F32)

    x = x_ref[0]
    mod = mod_ref[0]
    sh1, sc1, gt1 = mod[0:1], mod[1:2], mod[2:3]
    h = x * _rms_scale(x) * gpre_ref[...] * (1.0 + sc1) + sh1
    hb = h.astype(BF16)

    def proj(lo, hi):
        return jnp.dot(hb, win_ref[:, lo:hi], preferred_element_type=F32)

    u = proj(d, 2 * d) * proj(2 * d, 3 * d)
    uext_ref[CONV_HALO:CONV_HALO + tm, :] = u
    cw = convw_ref[...]
    conv = (cw[2:3] * u
            + cw[1:2] * uext_ref[CONV_HALO - 1:CONV_HALO - 1 + tm, :]
            + cw[0:1] * uext_ref[CONV_HALO - 2:CONV_HALO - 2 + tm, :])
    uext_ref[0:CONV_HALO, :] = u[tm - CONV_HALO:tm, :]
    y_conv = jnp.dot((proj(0, d) * conv).astype(BF16), wco_ref[...], preferred_element_type=F32)

    up = proj(3 * d, 3 * d + d_pool)
    pext_ref[POOL_HALO:POOL_HALO + tm, :] = up
    pos = j * tm + lax.broadcasted_iota(jnp.int32, (tm, 1), 0)
    zs = []
    for g, w in enumerate(POOL_WINDOWS):
        c0 = g * POOL_GROUP_DIM
        ug = up[:, c0:c0 + POOL_GROUP_DIM]
        acc = ug
        for k in range(1, w):
            acc = acc + pext_ref[POOL_HALO - k:POOL_HALO - k + tm, c0:c0 + POOL_GROUP_DIM]
        inv_cnt = 1.0 / jnp.minimum(pos + 1, w).astype(F32)
        diff = acc * inv_cnt - ug
        zs.append(jnp.dot(diff.astype(BF16), wpg_ref[g], preferred_element_type=F32))
    pext_ref[0:POOL_HALO, :] = up[tm - POOL_HALO:tm, :]
    z = jnp.concatenate(zs, axis=1) * pscale_ref[...]
    y_pool = jnp.dot(z.astype(BF16), wpp_ref[...], preferred_element_type=F32)

    a_conv = proj(3 * d + d_pool, 4 * d + d_pool)
    a_pool = proj(4 * d + d_pool, 5 * d + d_pool)
    merged = jax.nn.sigmoid(a_conv) * y_conv + jax.nn.sigmoid(a_pool) * y_pool
    y = jnp.dot(merged.astype(BF16), wo_ref[...], preferred_element_type=F32)
    o_ref[0] = x + gt1 * (y * _rms_scale(y) * gpost_ref[...])


def _token_mixer(x, mod, g_pre, g_post, w_in, conv_w, w_conv_out, w_pool_group, pool_scale,
                 w_pool_proj, w_o):
    b, s, d = x.shape
    d_in = w_in.shape[1]
    d_pool = w_pool_proj.shape[0]
    tm = min(MIX_TILE, s)
    const2 = lambda bi, j: (0, 0)
    const3 = lambda bi, j: (0, 0, 0)
    return pl.pallas_call(
        _mixer_kernel,
        grid=(b, s // tm),
        in_specs=[
            pl.BlockSpec((1, tm, d), lambda bi, j: (bi, j, 0)),
            pl.BlockSpec((1, 6, d), lambda bi, j: (bi, 0, 0)),
            pl.BlockSpec((1, d), const2),
            pl.BlockSpec((1, d), const2),
            pl.BlockSpec((d, d_in), const2),
            pl.BlockSpec((3, d), const2),
            pl.BlockSpec((d, d), const2),
            pl.BlockSpec(w_pool_group.shape, const3),
            pl.BlockSpec((1, d_pool), const2),
            pl.BlockSpec((d_pool, d), const2),
            pl.BlockSpec((d, d), const2),
        ],
        out_specs=pl.BlockSpec((1, tm, d), lambda bi, j: (bi, j, 0)),
        out_shape=jax.ShapeDtypeStruct(x.shape, F32),
        scratch_shapes=[
            pltpu.VMEM((CONV_HALO + tm, d), F32),
            pltpu.VMEM((POOL_HALO + tm, d_pool), F32),
        ],
        compiler_params=pltpu.CompilerParams(
            dimension_semantics=("arbitrary", "arbitrary"), vmem_limit_bytes=VMEM_LIMIT),
    )(x, mod, g_pre.reshape(1, d), g_post.reshape(1, d), w_in.astype(BF16), conv_w,
      w_conv_out.astype(BF16), w_pool_group.astype(BF16), pool_scale.reshape(1, d_pool),
      w_pool_proj.astype(BF16), w_o.astype(BF16))


def _wins(other, v, tie_i):
    return jnp.where(other > v, 1, 0) + jnp.where(other == v, tie_i, 0)


def _route_kernel(x_ref, mod_ref, g_ref, wrt_ref, bias_ref, wsg_ref, wsu_ref, wsd_ref,
                  h2g_ref, shared_ref, rowl_ref, wl_ref, nch_ref, q0_ref,
                  lg_ref, pre_ref, rho_ref, wd_ref):
    i = pl.program_id(1)
    lc, d = x_ref.shape
    n_sub = lg_ref.shape[0]
    n_chunk = d // LANES

    @pl.when(i < n_sub)
    def _():
        mod = mod_ref[0]
        sh2, sc2 = mod[3:4], mod[4:5]

        @pl.when(i == 0)
        def _():
            spare = n_sub * lc * n_chunk
            h2g_ref[0, spare:spare + n_chunk, :] = jnp.zeros((n_chunk, LANES), F32)

        x = x_ref[...]
        h = x * _rms_scale(x) * g_ref[...] * (1.0 + sc2) + sh2
        for c in range(n_chunk):
            h2g_ref[0, pl.ds(i * lc * n_chunk + c, lc, stride=n_chunk), :] = (
                h[:, c * LANES:(c + 1) * LANES])
        hb = h.astype(BF16)
        act = (_silu(jnp.dot(hb, wsg_ref[...], preferred_element_type=F32))
               * jnp.dot(hb, wsu_ref[...], preferred_element_type=F32))
        shared_ref[...] = jnp.dot(act.astype(BF16), wsd_ref[...], preferred_element_type=F32)
        lg_ref[i] = lax.dot_general(
            wrt_ref[...], h, (((1,), (1,)), ((), ())), preferred_element_type=F32,
            precision=lax.Precision.HIGHEST)

    @pl.when(i == n_sub)
    def _():
        _route_plan(bias_ref, rowl_ref, wl_ref, nch_ref, q0_ref, lg_ref, pre_ref, rho_ref, wd_ref,
                    lc, n_chunk)


def _route_plan(bias_ref, rowl_ref, wl_ref, nch_ref, q0_ref, lg_ref, pre_ref, rho_ref, wd_ref,
                lc, n_chunk):
    n_sub = lg_ref.shape[0]
    tb = n_sub * lc
    gidx = lax.broadcasted_iota(jnp.int32, (N_GROUPS, lc), 0)
    tie = [None] + [jnp.where(gidx >= r, 1, 0) for r in range(1, N_GROUPS)]
    tri = (lax.broadcasted_iota(jnp.int32, (lc, lc), 0)
           < lax.broadcasted_iota(jnp.int32, (lc, lc), 1)).astype(BF16)
    carry = jnp.zeros((N_EXPERTS, 1), F32)
    neg_inf = jnp.float32(-jnp.inf)
    for ci in range(tb // lc):
        c0 = ci * lc
        s_all = jax.nn.sigmoid(lg_ref[ci])
        aff = [s_all[GROUP_SIZE * jj:GROUP_SIZE * (jj + 1), :] for jj in range(GROUP_SIZE)]
        sel = [aff[jj] + bias_ref[GROUP_SIZE * jj:GROUP_SIZE * (jj + 1), :]
               for jj in range(GROUP_SIZE)]
        m1, m2 = sel[0], jnp.full_like(sel[0], neg_inf)
        for jj in range(1, GROUP_SIZE):
            m2 = jnp.maximum(m2, jnp.minimum(m1, sel[jj]))
            m1 = jnp.maximum(m1, sel[jj])
        gs = m1 + m2
        beaten = jnp.zeros((N_GROUPS, lc), jnp.int32)
        for r in range(1, N_GROUPS):
            other = pltpu.roll(gs, r, axis=0)
            beaten = beaten + _wins(other, gs, tie[r])
        gmask = beaten < TOPK_GROUPS
        masked = [jnp.where(gmask, sel[jj], neg_inf) for jj in range(GROUP_SIZE)]
        rolled = [[masked[jj]] + [pltpu.roll(masked[jj], r, axis=0) for r in range(1, N_GROUPS)]
                  for jj in range(GROUP_SIZE)]
        rho = []
        for jj in range(GROUP_SIZE):
            v = masked[jj]
            cnt = jnp.zeros((N_GROUPS, lc), jnp.int32)
            for j2 in range(GROUP_SIZE):
                for r in range(N_GROUPS):
                    if r == 0 and j2 == jj:
                        continue
                    other = rolled[j2][r]
                    if r == 0:
                        wins = (other >= v) if j2 < jj else (other > v)
                        cnt = cnt + jnp.where(wins, 1, 0)
                    else:
                        cnt = cnt + _wins(other, v, tie[r])
            rho.append(cnt)
        chosen = [rho[jj] < TOP_K for jj in range(GROUP_SIZE)]
        ssum = jnp.zeros((N_GROUPS, lc), F32)
        for jj in range(GROUP_SIZE):
            ssum = ssum + jnp.where(chosen[jj], aff[jj], 0.0)
        ssum = jnp.sum(ssum, axis=0, keepdims=True)
        wdense = [jnp.where(chosen[jj], aff[jj] / ssum * ROUTED_SCALE, 0.0)
                  for jj in range(GROUP_SIZE)]
        chosen_f = jnp.concatenate([c.astype(F32) for c in chosen], axis=0)
        prefix = jnp.dot(chosen_f.astype(BF16), tri, preferred_element_type=F32) + carry
        carry = carry + jnp.sum(chosen_f, axis=1, keepdims=True)
        pre_ref[:, c0:c0 + lc] = prefix
        rho_ref[:, c0:c0 + lc] = jnp.concatenate(rho, axis=0)
        wd_ref[:, c0:c0 + lc] = jnp.concatenate(wdense, axis=0)

    m = MOE_CHUNK
    nch_b = jnp.broadcast_to(jnp.floor((carry + (m - 1)) * (1.0 / m)), (N_EXPERTS, LANES))
    lower = (lax.broadcasted_iota(jnp.int32, (N_EXPERTS, N_EXPERTS), 1)
             < lax.broadcasted_iota(jnp.int32, (N_EXPERTS, N_EXPERTS), 0)).astype(F32)
    q0_b = jnp.dot(lower, nch_b, preferred_element_type=F32, precision=lax.Precision.HIGHEST)
    nch_ref[0] = nch_b.astype(jnp.int32)
    q0_ref[0] = q0_b.astype(jnp.int32)
    offs_col = q0_b[:, 0:1] * m

    nq = rowl_ref.shape[1]
    iota_q = lax.broadcasted_iota(jnp.int32, (nq, lc), 0).astype(F32)
    iota_r = lax.broadcasted_iota(jnp.int32, (m, lc), 0).astype(F32)
    lists = jnp.zeros((nq, 5 * m), F32)
    for ci in range(tb // lc):
        c0 = ci * lc
        dest_dense = pre_ref[:, c0:c0 + lc] + offs_col
        rho_c = rho_ref[:, c0:c0 + lc]
        w_c = wd_ref[:, c0:c0 + lc]
        tokv = (c0 + 1 + lax.broadcasted_iota(jnp.int32, (1, lc), 1)).astype(F32)
        tok_hi = jnp.floor(tokv * (1.0 / 64))
        tok_lo = tokv - 64.0 * tok_hi
        for k in range(TOP_K):
            hit = rho_c == k
            dk = jnp.sum(jnp.where(hit, dest_dense, 0.0), axis=0, keepdims=True)
            wk = jnp.sum(jnp.where(hit, w_c, 0.0), axis=0, keepdims=True)
            qk = jnp.floor(dk * (1.0 / m))
            rk = dk - m * qk
            w_hi = wk.astype(BF16).astype(F32)
            w_mid = (wk - w_hi).astype(BF16).astype(F32)
            w_lo = wk - w_hi - w_mid
            onehot_q = jnp.where(iota_q == qk + LIST_LEAD, 1.0, 0.0).astype(BF16)
            rmask = iota_r == rk
            vals = jnp.concatenate(
                [jnp.where(rmask, piece, 0.0) for piece in (tok_hi, tok_lo, w_hi, w_mid, w_lo)],
                axis=0).astype(BF16)
            lists = lists + lax.dot_general(onehot_q, vals, (((1,), (1,)), ((), ())),
                                            preferred_element_type=F32)
    tok = lists[:, 0:m] * 64.0 + lists[:, m:2 * m]
    tile = jnp.where(tok == 0.0, float(tb), tok - 1.0)
    rowl_ref[0] = (tile * n_chunk).astype(jnp.int32)
    wl_ref[0] = (lists[:, 2 * m:3 * m] + lists[:, 3 * m:4 * m]) + lists[:, 4 * m:5 * m]


def _route(x1, mod, g_pre, w_router, router_bias, w_sh_gate, w_sh_up, w_sh_down, seq, tb):
    t, d = x1.shape
    nb = t // tb
    n_chunk = d // LANES
    d_sh = w_sh_gate.shape[1]
    perm = jnp.arange(N_EXPERTS).reshape(N_GROUPS, GROUP_SIZE).T.reshape(-1)
    wrt = w_router.T[perm]
    bias = router_bias[perm].reshape(N_EXPERTS, 1)
    nq = _num_list_rows(tb)
    lc = min(ROUTE_LANES, tb)
    n_sub = tb // lc
    const2 = lambda bi, i: (0, 0)
    per_block = lambda bi, i: (bi, 0, 0)
    sub_tile = lambda bi, i: (bi * n_sub + jnp.minimum(i, n_sub - 1), 0)
    outs = pl.pallas_call(
        _route_kernel,
        grid=(nb, n_sub + 1),
        in_specs=[
            pl.BlockSpec((lc, d), sub_tile),
            pl.BlockSpec((1, 6, d), lambda bi, i: (bi * tb // seq, 0, 0)),
            pl.BlockSpec((1, d), const2),
            pl.BlockSpec((N_EXPERTS, d), const2),
            pl.BlockSpec((N_EXPERTS, 1), const2),
            pl.BlockSpec((d, d_sh), const2),
            pl.BlockSpec((d, d_sh), const2),
            pl.BlockSpec((d_sh, d), const2),
        ],
        out_specs=[
            pl.BlockSpec((1, (tb + 1) * n_chunk, LANES), per_block),
            pl.BlockSpec((lc, d), sub_tile),
            pl.BlockSpec((1, nq, MOE_CHUNK), per_block),
            pl.BlockSpec((1, nq, MOE_CHUNK), per_block),
            pl.BlockSpec((1, N_EXPERTS, LANES), per_block),
            pl.BlockSpec((1, N_EXPERTS, LANES), per_block),
        ],
        out_shape=[
            jax.ShapeDtypeStruct((nb, (tb + 1) * n_chunk, LANES), F32),
            jax.ShapeDtypeStruct((t, d), F32),
            jax.ShapeDtypeStruct((nb, nq, MOE_CHUNK), jnp.int32),
            jax.ShapeDtypeStruct((nb, nq, MOE_CHUNK), F32),
            jax.ShapeDtypeStruct((nb, N_EXPERTS, LANES), jnp.int32),
            jax.ShapeDtypeStruct((nb, N_EXPERTS, LANES), jnp.int32),
        ],
        scratch_shapes=[
            pltpu.VMEM((n_sub, N_EXPERTS, lc), F32),
            pltpu.VMEM((N_EXPERTS, tb), F32),
            pltpu.VMEM((N_EXPERTS, tb), jnp.int32),
            pltpu.VMEM((N_EXPERTS, tb), F32),
        ],
        compiler_params=pltpu.CompilerParams(
            dimension_semantics=("arbitrary", "arbitrary"), vmem_limit_bytes=VMEM_LIMIT),
    )(x1, mod, g_pre.reshape(1, d), wrt, bias, w_sh_gate.astype(BF16), w_sh_up.astype(BF16),
      w_sh_down.astype(BF16))
    return outs


def _num_list_rows(tb):
    return TOP_K * tb // MOE_CHUNK + N_EXPERTS + SUBLANES


def _moe_kernel(nch_sm, q0_sm, prev_sm, h2g_hbm, rowl_hbm, wl_ref, wg_ref, wu_ref, wd_ref,
                wdp_ref, o_hbm, h2g_ref, acc_ref, xt0_ref, xt1_ref, yt0_ref, yt1_ref, act0_ref,
                act1_ref, rowl_sm, sems):
    bi = pl.program_id(0)
    r = pl.program_id(1)
    n_chunk = xt0_ref.shape[0] // (MOE_CHUNK + SUBLANES)
    m = MOE_CHUNK
    stride = m + SUBLANES
    eye = (lax.broadcasted_iota(jnp.int32, (m, m), 0)
           == lax.broadcasted_iota(jnp.int32, (m, m), 1))
    xts, yts, acts = (xt0_ref, xt1_ref), (yt0_ref, yt1_ref), (act0_ref, act1_ref)

    def gather(lrow, xt_ref):
        for mi in range(m):
            row = pl.multiple_of(rowl_sm[lrow, mi], n_chunk)
            xt_ref[pl.ds(mi, n_chunk, stride=stride), :] = h2g_ref[pl.ds(row, n_chunk), :]

    def scatter(lrow, yt_ref):
        for g0 in range(0, m, SUBLANES):
            rows, vals = [], []
            for mi in range(g0, g0 + SUBLANES):
                row = pl.multiple_of(rowl_sm[lrow, mi], n_chunk)
                rows.append(row)
                vals.append(acc_ref[pl.ds(row, n_chunk), :]
                            + yt_ref[pl.ds(mi, n_chunk, stride=stride), :])
            for row, val in zip(rows, vals):
                acc_ref[pl.ds(row, n_chunk), :] = val

    def gate_up(xt_ref, act_ref):
        xs = jnp.concatenate([xt_ref[c * stride:c * stride + m, :] for c in range(n_chunk)],
                             axis=1).astype(BF16)
        act = (_silu(jnp.dot(xs, wg_ref[0, 0], preferred_element_type=F32))
               * jnp.dot(xs, wu_ref[0, 0], preferred_element_type=F32))
        act_ref[...] = act.astype(BF16)

    def down(lrow, act_ref, w_ref, yt_ref):
        y = jnp.dot(act_ref[...], w_ref[0, 0], preferred_element_type=F32)
        w_row = wl_ref[0, pl.ds(lrow, 1), :]
        w_col = jnp.sum(jnp.where(eye, w_row, 0.0), axis=1, keepdims=True)
        y = y * w_col
        for c in range(n_chunk):
            yt_ref[c * stride:c * stride + m, :] = y[:, c * LANES:(c + 1) * LANES]

    def step(q, par, w_down_ref):
        cur, oth = par, 1 - par
        gather(q + 1 + LIST_LEAD, xts[oth])
        gate_up(xts[cur], acts[cur])
        down(q - 1 + LIST_LEAD, acts[oth], w_down_ref, yts[oth])
        scatter(q - 2 + LIST_LEAD, yts[cur])

    def by_parity(q, fn):
        for par in (0, 1):
            @pl.when((q & 1) == par)
            def _():
                fn(par)

    def block_loads():
        return (pltpu.make_async_copy(rowl_hbm.at[bi], rowl_sm, sems.at[0]),
                pltpu.make_async_copy(h2g_hbm.at[bi], h2g_ref, sems.at[1]))

    def block_store():
        return pltpu.make_async_copy(acc_ref, o_hbm.at[bi], sems.at[2])

    @pl.when(r == 0)
    def _():
        for cp in block_loads():
            cp.start()
        acc_ref[...] = jnp.zeros(acc_ref.shape, F32)
        act1_ref[...] = jnp.zeros(act1_ref.shape, BF16)
        yt0_ref[...] = jnp.zeros(yt0_ref.shape, F32)
        for cp in block_loads():
            cp.wait()
        gather(LIST_LEAD, xt0_ref)

    n_chunks = nch_sm[bi * N_EXPERTS + r]
    q_first = q0_sm[bi * N_EXPERTS + r]

    @pl.when(n_chunks > 0)
    def _():
        by_parity(q_first, lambda par: step(q_first, par, wdp_ref))

    def chunk(ci, carry):
        q = q_first + ci
        by_parity(q, lambda par: step(q, par, wd_ref))
        return carry

    lax.fori_loop(1, n_chunks, chunk, 0)

    @pl.when(r == N_EXPERTS - 1)
    def _():
        q_last = q_first + n_chunks - 1

        def drain(par, w_down_ref):
            scatter(q_last - 1 + LIST_LEAD, yts[1 - par])
            down(q_last + LIST_LEAD, acts[par], w_down_ref, yts[par])
            scatter(q_last + LIST_LEAD, yts[par])

        @pl.when(n_chunks > 0)
        def _():
            by_parity(q_last, lambda par: drain(par, wd_ref))

        @pl.when(n_chunks == 0)
        def _():
            by_parity(q_last, lambda par: drain(par, wdp_ref))

        block_store().start()
        block_store().wait()


def _moe(h2g, rowl, wl, nch, q0, w_gate, w_up, w_down, layer, t, d, tb):
    nb = t // tb
    n_chunk = d // LANES
    d_e = w_gate.shape[3]
    m = MOE_CHUNK
    stride = m + SUBLANES
    nq = _num_list_rows(tb)

    def expert_of(r):
        return (r % N_GROUPS) * GROUP_SIZE + r // N_GROUPS

    rows = jnp.where(nch > 0, jnp.arange(N_EXPERTS, dtype=jnp.int32)[None, :], -1)
    last = lax.cummax(rows, axis=1)
    prev = jnp.maximum(jnp.concatenate([jnp.full((nb, 1), -1, jnp.int32), last[:, :-1]], axis=1), 0)

    cur_w = lambda bi, r, c, o, p: (layer, expert_of(r), 0, 0)
    prev_w = lambda bi, r, c, o, p: (layer, expert_of(p[bi * N_EXPERTS + r]), 0, 0)
    per_block = lambda bi, r, c, o, p: (bi, 0, 0)
    staging = pltpu.VMEM((n_chunk * stride, LANES), F32)
    grid_spec = pltpu.PrefetchScalarGridSpec(
        num_scalar_prefetch=3,
        grid=(nb, N_EXPERTS),
        in_specs=[
            pl.BlockSpec(memory_space=pl.ANY),
            pl.BlockSpec(memory_space=pl.ANY),
            pl.BlockSpec((1, nq, m), per_block),
            pl.BlockSpec((1, 1, d, d_e), cur_w),
            pl.BlockSpec((1, 1, d, d_e), cur_w),
            pl.BlockSpec((1, 1, d_e, d), cur_w),
            pl.BlockSpec((1, 1, d_e, d), prev_w),
        ],
        out_specs=pl.BlockSpec(memory_space=pl.ANY),
        scratch_shapes=[
            pltpu.VMEM(((tb + 1) * n_chunk, LANES), F32),
            pltpu.VMEM(((tb + 1) * n_chunk, LANES), F32),
            staging, staging, staging, staging,
            pltpu.VMEM((m, d_e), BF16),
            pltpu.VMEM((m, d_e), BF16),
            pltpu.SMEM((nq, m), jnp.int32),
            pltpu.SemaphoreType.DMA((3,)),
        ],
    )
    return pl.pallas_call(
        _moe_kernel,
        grid_spec=grid_spec,
        out_shape=jax.ShapeDtypeStruct((nb, (tb + 1) * n_chunk, LANES), F32),
        compiler_params=pltpu.CompilerParams(
            dimension_semantics=("arbitrary", "arbitrary"), vmem_limit_bytes=VMEM_LIMIT),
    )(nch.reshape(-1), q0.reshape(-1), prev.reshape(-1), h2g, rowl, wl, w_gate, w_up, w_down,
      w_down)


def _epilogue_kernel(x_ref, routed_ref, shared_ref, mod_ref, g_ref, o_ref):
    te, d = x_ref.shape
    n_chunk = d // LANES
    gt2 = mod_ref[0][5:6]
    routed = jnp.concatenate(
        [routed_ref[0, pl.ds(c, te, stride=n_chunk), :] for c in range(n_chunk)], axis=1)
    y = routed + shared_ref[...]
    o_ref[...] = x_ref[...] + gt2 * (y * _rms_scale(y) * g_ref[...])


def _epilogue(x1, routed, shared, mod, g_post, seq, tb):
    t, d = x1.shape
    n_chunk = d // LANES
    te = min(EPI_TILE, tb)
    n_sub = tb // te
    row = pl.BlockSpec((te, d), lambda bi, i: (bi * n_sub + i, 0))
    return pl.pallas_call(
        _epilogue_kernel,
        grid=(t // tb, n_sub),
        in_specs=[row,
                  pl.BlockSpec((1, te * n_chunk, LANES), lambda bi, i: (bi, i, 0)),
                  row,
                  pl.BlockSpec((1, 6, d), lambda bi, i: (bi * tb // seq, 0, 0)),
                  pl.BlockSpec((1, d), lambda bi, i: (0, 0))],
        out_specs=row,
        out_shape=jax.ShapeDtypeStruct((t, d), F32),
    )(x1, routed, shared, mod, g_post.reshape(1, d))


def kernel(x, c, w_ada, b_ada, g_pre_mix, g_post_mix, g_pre_ffn, g_post_ffn, w_in, conv_w,
           w_conv_out, w_pool_group, pool_scale, w_pool_proj, w_o, w_router, router_bias,
           w_exp_gate, w_exp_up, w_exp_down, w_sh_gate, w_sh_up, w_sh_down):
    b, s, d = x.shape
    depth = w_ada.shape[0]
    t = b * s
    tb = min(MOE_BLOCK, s)
    mods = _ada_mod(c, w_ada, b_ada).reshape(depth, b, 6, d)
    w_exp_gate, w_exp_up, w_exp_down = (w.astype(BF16) for w in (w_exp_gate, w_exp_up, w_exp_down))
    for l in range(depth):
        mod = mods[l]
        x = _token_mixer(x, mod, g_pre_mix[l], g_post_mix[l], w_in[l], conv_w[l], w_conv_out[l],
                         w_pool_group[l], pool_scale[l], w_pool_proj[l], w_o[l])
        x1 = x.reshape(t, d)
        h2g, shared, rowl, wl, nch, q0 = _route(
            x1, mod, g_pre_ffn[l], w_router[l], router_bias[l], w_sh_gate[l], w_sh_up[l],
            w_sh_down[l], s, tb)
        routed = _moe(h2g, rowl, wl, nch[:, :, 0], q0[:, :, 0],
                      w_exp_gate, w_exp_up, w_exp_down, l, t, d, tb)
        x = _epilogue(x1, routed, shared, mod, g_post_ffn[l], s, tb).reshape(b, s, d)
    return x
```

```python
import jax
import jax.numpy as jnp
from jax import lax
from jax.experimental import pallas as pl
from jax.experimental.pallas import tpu as pltpu

F32 = jnp.float32
BF16 = jnp.bfloat16

EPS = 1e-6
POOL_WINDOWS = (2, 4, 8, 16)
POOL_GROUP_DIM = 128
N_EXPERTS = 64
N_GROUPS = 8
GROUP_SIZE = 8
TOPK_GROUPS = 4
TOP_K = 8
ROUTED_SCALE = 2.5

LANES = 128
SUBLANES = 8
CONV_HALO = 8
POOL_HALO = 16
VMEM_LIMIT = 56 * 1024 * 1024

MIX_TILE = 256
ROUTE_LANES = 512
MOE_BLOCK = 4096
MOE_CHUNK = 128
MOE_GROUP = 5
LIST_LEAD = MOE_GROUP
EPI_TILE = 512


def _silu(v):
    return v * jax.nn.sigmoid(v)


def _rms_scale(v):
    return lax.rsqrt(jnp.mean(v * v, axis=-1, keepdims=True) + EPS)


def _ada_kernel(c_ref, w_ref, b_ref, o_ref):
    cond = _silu(c_ref[...])
    o_ref[0] = jnp.dot(cond, w_ref[0], preferred_element_type=F32,
                       precision=lax.Precision.HIGHEST) + b_ref[0]


def _ada_mod(c, w_ada, b_ada):
    depth, d, d6 = w_ada.shape
    b = c.shape[0]
    n_col = d6 // d
    return pl.pallas_call(
        _ada_kernel,
        grid=(depth, n_col),
        in_specs=[
            pl.BlockSpec((b, d), lambda l, n: (0, 0)),
            pl.BlockSpec((1, d, d), lambda l, n: (l, 0, n)),
            pl.BlockSpec((1, 1, d), lambda l, n: (l, 0, n)),
        ],
        out_specs=pl.BlockSpec((1, b, d), lambda l, n: (l, 0, n)),
        out_shape=jax.ShapeDtypeStruct((depth, b, d6), F32),
    )(c, w_ada, b_ada.reshape(depth, 1, d6))


def _mixer_kernel(x_ref, mod_ref, gpre_ref, gpost_ref, win_ref, convw_ref, wco_ref, wpg_ref,
                  pscale_ref, wpp_ref, wo_ref, o_ref, uext_ref, pext_ref):
    j = pl.program_id(1)
    tm, d = x_ref.shape[1], x_ref.shape[2]
    d_pool = pext_ref.shape[1]

    @pl.when(j == 0)
    def _():
        uext_ref[0:CONV_HALO, :] = jnp.zeros((CONV_HALO, d), F32)
        pext_ref[0:POOL_HALO, :] = jnp.zeros((POOL_HALO, d_pool), F32)

    x = x_ref[0]
    mod = mod_ref[0]
    sh1, sc1, gt1 = mod[0:1], mod[1:2], mod[2:3]
    h = x * _rms_scale(x) * gpre_ref[...] * (1.0 + sc1) + sh1
    hb = h.astype(BF16)

    def proj(lo, hi):
        return jnp.dot(hb, win_ref[:, lo:hi], preferred_element_type=F32)

    u = proj(d, 2 * d) * proj(2 * d, 3 * d)
    uext_ref[CONV_HALO:CONV_HALO + tm, :] = u
    cw = convw_ref[...]
    conv = (cw[2:3] * u
            + cw[1:2] * uext_ref[CONV_HALO - 1:CONV_HALO - 1 + tm, :]
            + cw[0:1] * uext_ref[CONV_HALO - 2:CONV_HALO - 2 + tm, :])
    uext_ref[0:CONV_HALO, :] = u[tm - CONV_HALO:tm, :]
    y_conv = jnp.dot((proj(0, d) * conv).astype(BF16), wco_ref[...], preferred_element_type=F32)

    up = proj(3 * d, 3 * d + d_pool)
    pext_ref[POOL_HALO:POOL_HALO + tm, :] = up
    pos = j * tm + lax.broadcasted_iota(jnp.int32, (tm, 1), 0)
    zs = []
    for g, w in enumerate(POOL_WINDOWS):
        c0 = g * POOL_GROUP_DIM
        ug = up[:, c0:c0 + POOL_GROUP_DIM]
        acc = ug
        for k in range(1, w):
            acc = acc + pext_ref[POOL_HALO - k:POOL_HALO - k + tm, c0:c0 + POOL_GROUP_DIM]
        inv_cnt = 1.0 / jnp.minimum(pos + 1, w).astype(F32)
        diff = acc * inv_cnt - ug
        zs.append(jnp.dot(diff.astype(BF16), wpg_ref[g], preferred_element_type=F32))
    pext_ref[0:POOL_HALO, :] = up[tm - POOL_HALO:tm, :]
    z = jnp.concatenate(zs, axis=1) * pscale_ref[...]
    y_pool = jnp.dot(z.astype(BF16), wpp_ref[...], preferred_element_type=F32)

    a_conv = proj(3 * d + d_pool, 4 * d + d_pool)
    a_pool = proj(4 * d + d_pool, 5 * d + d_pool)
    merged = jax.nn.sigmoid(a_conv) * y_conv + jax.nn.sigmoid(a_pool) * y_pool
    y = jnp.dot(merged.astype(BF16), wo_ref[...], preferred_element_type=F32)
    o_ref[0] = x + gt1 * (y * _rms_scale(y) * gpost_ref[...])


def _token_mixer(x, mod, g_pre, g_post, w_in, conv_w, w_conv_out, w_pool_group, pool_scale,
                 w_pool_proj, w_o):
    b, s, d = x.shape
    d_in = w_in.shape[1]
    d_pool = w_pool_proj.shape[0]
    tm = min(MIX_TILE, s)
    const2 = lambda bi, j: (0, 0)
    const3 = lambda bi, j: (0, 0, 0)
    return pl.pallas_call(
        _mixer_kernel,
        grid=(b, s // tm),
        in_specs=[
            pl.BlockSpec((1, tm, d), lambda bi, j: (bi, j, 0)),
            pl.BlockSpec((1, 6, d), lambda bi, j: (bi, 0, 0)),
            pl.BlockSpec((1, d), const2),
            pl.BlockSpec((1, d), const2),
            pl.BlockSpec((d, d_in), const2),
            pl.BlockSpec((3, d), const2),
            pl.BlockSpec((d, d), const2),
            pl.BlockSpec(w_pool_group.shape, const3),
            pl.BlockSpec((1, d_pool), const2),
            pl.BlockSpec((d_pool, d), const2),
            pl.BlockSpec((d, d), const2),
        ],
        out_specs=pl.BlockSpec((1, tm, d), lambda bi, j: (bi, j, 0)),
        out_shape=jax.ShapeDtypeStruct(x.shape, F32),
        scratch_shapes=[
            pltpu.VMEM((CONV_HALO + tm, d), F32),
            pltpu.VMEM((POOL_HALO + tm, d_pool), F32),
        ],
        compiler_params=pltpu.CompilerParams(
            dimension_semantics=("arbitrary", "arbitrary"), vmem_limit_bytes=VMEM_LIMIT),
    )(x, mod, g_pre.reshape(1, d), g_post.reshape(1, d), w_in.astype(BF16), conv_w,
      w_conv_out.astype(BF16), w_pool_group.astype(BF16), pool_scale.reshape(1, d_pool),
      w_pool_proj.astype(BF16), w_o.astype(BF16))


def _wins(other, v, tie_i):
    return jnp.where(other > v, 1, 0) + jnp.where(other == v, tie_i, 0)


def _route_kernel(x_ref, mod_ref, g_ref, wrt_ref, bias_ref, wsg_ref, wsu_ref, wsd_ref,
                  h2g_ref, shared_ref, rowl_ref, wl_ref, nch_ref, q0_ref,
                  lg_ref, pre_ref, rho_ref, wd_ref):
    i = pl.program_id(1)
    lc, d = x_ref.shape
    n_sub = lg_ref.shape[0]
    n_chunk = d // LANES

    @pl.when(i < n_sub)
    def _():
        mod = mod_ref[0]
        sh2, sc2 = mod[3:4], mod[4:5]

        @pl.when(i == 0)
        def _():
            spare = n_sub * lc * n_chunk
            h2g_ref[0, spare:spare + n_chunk, :] = jnp.zeros((n_chunk, LANES), F32)

        x = x_ref[...]
        h = x * _rms_scale(x) * g_ref[...] * (1.0 + sc2) + sh2
        for c in range(n_chunk):
            h2g_ref[0, pl.ds(i * lc * n_chunk + c, lc, stride=n_chunk), :] = (
                h[:, c * LANES:(c + 1) * LANES])
        hb = h.astype(BF16)
        act = (_silu(jnp.dot(hb, wsg_ref[...], preferred_element_type=F32))
               * jnp.dot(hb, wsu_ref[...], preferred_element_type=F32))
        shared_ref[...] = jnp.dot(act.astype(BF16), wsd_ref[...], preferred_element_type=F32)
        lg_ref[i] = lax.dot_general(
            wrt_ref[...], h, (((1,), (1,)), ((), ())), preferred_element_type=F32,
            precision=lax.Precision.HIGHEST)

    @pl.when(i == n_sub)
    def _():
        _route_plan(bias_ref, rowl_ref, wl_ref, nch_ref, q0_ref, lg_ref, pre_ref, rho_ref, wd_ref,
                    lc, n_chunk)


def _route_plan(bias_ref, rowl_ref, wl_ref, nch_ref, q0_ref, lg_ref, pre_ref, rho_ref, wd_ref,
                lc, n_chunk):
    n_sub = lg_ref.shape[0]
    tb = n_sub * lc
    gidx = lax.broadcasted_iota(jnp.int32, (N_GROUPS, lc), 0)
    tie = [None] + [jnp.where(gidx >= r, 1, 0) for r in range(1, N_GROUPS)]
    tri = (lax.broadcasted_iota(jnp.int32, (lc, lc), 0)
           < lax.broadcasted_iota(jnp.int32, (lc, lc), 1)).astype(BF16)
    carry = jnp.zeros((N_EXPERTS, 1), F32)
    neg_inf = jnp.float32(-jnp.inf)
    for ci in range(tb // lc):
        c0 = ci * lc
        s_all = jax.nn.sigmoid(lg_ref[ci])
        aff = [s_all[GROUP_SIZE * jj:GROUP_SIZE * (jj + 1), :] for jj in range(GROUP_SIZE)]
        sel = [aff[jj] + bias_ref[GROUP_SIZE * jj:GROUP_SIZE * (jj + 1), :]
               for jj in range(GROUP_SIZE)]
        m1, m2 = sel[0], jnp.full_like(sel[0], neg_inf)
        for jj in range(1, GROUP_SIZE):
            m2 = jnp.maximum(m2, jnp.minimum(m1, sel[jj]))
            m1 = jnp.maximum(m1, sel[jj])
        gs = m1 + m2
        beaten = jnp.zeros((N_GROUPS, lc), jnp.int32)
        for r in range(1, N_GROUPS):
            other = pltpu.roll(gs, r, axis=0)
            beaten = beaten + _wins(other, gs, tie[r])
        gmask = beaten < TOPK_GROUPS
        masked = [jnp.where(gmask, sel[jj], neg_inf) for jj in range(GROUP_SIZE)]
        rolled = [[masked[jj]] + [pltpu.roll(masked[jj], r, axis=0) for r in range(1, N_GROUPS)]
                  for jj in range(GROUP_SIZE)]
        rho = []
        for jj in range(GROUP_SIZE):
            v = masked[jj]
            cnt = jnp.zeros((N_GROUPS, lc), jnp.int32)
            for j2 in range(GROUP_SIZE):
                for r in range(N_GROUPS):
                    if r == 0 and j2 == jj:
                        continue
                    other = rolled[j2][r]
                    if r == 0:
                        wins = (other >= v) if j2 < jj else (other > v)
                        cnt = cnt + jnp.where(wins, 1, 0)
                    else:
                        cnt = cnt + _wins(other, v, tie[r])
            rho.append(cnt)
        chosen = [rho[jj] < TOP_K for jj in range(GROUP_SIZE)]
        ssum = jnp.zeros((N_GROUPS, lc), F32)
        for jj in range(GROUP_SIZE):
            ssum = ssum + jnp.where(chosen[jj], aff[jj], 0.0)
        ssum = jnp.sum(ssum, axis=0, keepdims=True)
        wdense = [jnp.where(chosen[jj], aff[jj] / ssum * ROUTED_SCALE, 0.0)
                  for jj in range(GROUP_SIZE)]
        chosen_f = jnp.concatenate([c.astype(F32) for c in chosen], axis=0)
        prefix = jnp.dot(chosen_f.astype(BF16), tri, preferred_element_type=F32) + carry
        carry = carry + jnp.sum(chosen_f, axis=1, keepdims=True)
        pre_ref[:, c0:c0 + lc] = prefix
        rho_ref[:, c0:c0 + lc] = jnp.concatenate(rho, axis=0)
        wd_ref[:, c0:c0 + lc] = jnp.concatenate(wdense, axis=0)

    m = MOE_CHUNK
    nch_b = jnp.broadcast_to(jnp.floor((carry + (m - 1)) * (1.0 / m)), (N_EXPERTS, LANES))
    lower = (lax.broadcasted_iota(jnp.int32, (N_EXPERTS, N_EXPERTS), 1)
             < lax.broadcasted_iota(jnp.int32, (N_EXPERTS, N_EXPERTS), 0)).astype(F32)
    q0_b = jnp.dot(lower, nch_b, preferred_element_type=F32, precision=lax.Precision.HIGHEST)
    nch_ref[0] = nch_b.astype(jnp.int32)
    q0_ref[0] = q0_b.astype(jnp.int32)
    offs_col = q0_b[:, 0:1] * m

    nq = rowl_ref.shape[1]
    iota_q = lax.broadcasted_iota(jnp.int32, (nq, lc), 0).astype(F32)
    iota_r = lax.broadcasted_iota(jnp.int32, (m, lc), 0).astype(F32)
    lists = jnp.zeros((nq, 5 * m), F32)
    for ci in range(tb // lc):
        c0 = ci * lc
        dest_dense = pre_ref[:, c0:c0 + lc] + offs_col
        rho_c = rho_ref[:, c0:c0 + lc]
        w_c = wd_ref[:, c0:c0 + lc]
        tokv = (c0 + 1 + lax.broadcasted_iota(jnp.int32, (1, lc), 1)).astype(F32)
        tok_hi = jnp.floor(tokv * (1.0 / 64))
        tok_lo = tokv - 64.0 * tok_hi
        for k in range(TOP_K):
            hit = rho_c == k
            dk = jnp.sum(jnp.where(hit, dest_dense, 0.0), axis=0, keepdims=True)
            wk = jnp.sum(jnp.where(hit, w_c, 0.0), axis=0, keepdims=True)
            qk = jnp.floor(dk * (1.0 / m))
            rk = dk - m * qk
            w_hi = wk.astype(BF16).astype(F32)
            w_mid = (wk - w_hi).astype(BF16).astype(F32)
            w_lo = wk - w_hi - w_mid
            onehot_q = jnp.where(iota_q == qk + LIST_LEAD, 1.0, 0.0).astype(BF16)
            rmask = iota_r == rk
            vals = jnp.concatenate(
                [jnp.where(rmask, piece, 0.0) for piece in (tok_hi, tok_lo, w_hi, w_mid, w_lo)],
                axis=0).astype(BF16)
            lists = lists + lax.dot_general(onehot_q, vals, (((1,), (1,)), ((), ())),
                                            preferred_element_type=F32)
    tok = lists[:, 0:m] * 64.0 + lists[:, m:2 * m]
    tile = jnp.where(tok == 0.0, float(tb), tok - 1.0)
    rowl_ref[0] = (tile * n_chunk).astype(jnp.int32)
    wl_ref[0] = (lists[:, 2 * m:3 * m] + lists[:, 3 * m:4 * m]) + lists[:, 4 * m:5 * m]


def _route(x1, mod, g_pre, w_router, router_bias, w_sh_gate, w_sh_up, w_sh_down, seq, tb):
    t, d = x1.shape
    nb = t // tb
    n_chunk = d // LANES
    d_sh = w_sh_gate.shape[1]
    perm = jnp.arange(N_EXPERTS).reshape(N_GROUPS, GROUP_SIZE).T.reshape(-1)
    wrt = w_router.T[perm]
    bias = router_bias[perm].reshape(N_EXPERTS, 1)
    nq = _num_list_rows(tb)
    lc = min(ROUTE_LANES, tb)
    n_sub = tb // lc
    const2 = lambda bi, i: (0, 0)
    per_block = lambda bi, i: (bi, 0, 0)
    sub_tile = lambda bi, i: (bi * n_sub + jnp.minimum(i, n_sub - 1), 0)
    outs = pl.pallas_call(
        _route_kernel,
        grid=(nb, n_sub + 1),
        in_specs=[
            pl.BlockSpec((lc, d), sub_tile),
            pl.BlockSpec((1, 6, d), lambda bi, i: (bi * tb // seq, 0, 0)),
            pl.BlockSpec((1, d), const2),
            pl.BlockSpec((N_EXPERTS, d), const2),
            pl.BlockSpec((N_EXPERTS, 1), const2),
            pl.BlockSpec((d, d_sh), const2),
            pl.BlockSpec((d, d_sh), const2),
            pl.BlockSpec((d_sh, d), const2),
        ],
        out_specs=[
            pl.BlockSpec((1, (tb + 1) * n_chunk, LANES), per_block),
            pl.BlockSpec((lc, d), sub_tile),
            pl.BlockSpec((1, nq, MOE_CHUNK), per_block),
            pl.BlockSpec((1, nq, MOE_CHUNK), per_block),
            pl.BlockSpec((1, N_EXPERTS, LANES), per_block),
            pl.BlockSpec((1, N_EXPERTS, LANES), per_block),
        ],
        out_shape=[
            jax.ShapeDtypeStruct((nb, (tb + 1) * n_chunk, LANES), F32),
            jax.ShapeDtypeStruct((t, d), F32),
            jax.ShapeDtypeStruct((nb, nq, MOE_CHUNK), jnp.int32),
            jax.ShapeDtypeStruct((nb, nq, MOE_CHUNK), F32),
            jax.ShapeDtypeStruct((nb, N_EXPERTS, LANES), jnp.int32),
            jax.ShapeDtypeStruct((nb, N_EXPERTS, LANES), jnp.int32),
        ],
        scratch_shapes=[
            pltpu.VMEM((n_sub, N_EXPERTS, lc), F32),
            pltpu.VMEM((N_EXPERTS, tb), F32),
            pltpu.VMEM((N_EXPERTS, tb), jnp.int32),
            pltpu.VMEM((N_EXPERTS, tb), F32),
        ],
        compiler_params=pltpu.CompilerParams(
            dimension_semantics=("arbitrary", "arbitrary"), vmem_limit_bytes=VMEM_LIMIT),
    )(x1, mod, g_pre.reshape(1, d), wrt, bias, w_sh_gate.astype(BF16), w_sh_up.astype(BF16),
      w_sh_down.astype(BF16))
    return outs


def _num_list_rows(tb):
    rows = TOP_K * tb // MOE_CHUNK + N_EXPERTS + LIST_LEAD + MOE_GROUP
    return -(-rows // SUBLANES) * SUBLANES


ST_PREV_START, ST_PREV_VALID, ST_PPREV_START, ST_GROUPS_DONE, ST_PREV_SLOT, ST_EXPERTS_DONE = range(6)


def _moe_kernel(nch_sm, q0_sm, h2g_hbm, rowl_hbm, wl_ref, wg_ref, wu_ref, wd_ref, o_hbm,
                h2g_ref, acc_ref, xt0_ref, xt1_ref, yt0_ref, yt1_ref, act0_ref, act1_ref,
                wdh_ref, rowl_sm, st_sm, sems):
    bi = pl.program_id(0)
    r = pl.program_id(1)
    m = MOE_CHUNK
    mg = MOE_GROUP * m
    stride = mg + SUBLANES
    n_chunk = xt0_ref.shape[0] // stride
    eye = (lax.broadcasted_iota(jnp.int32, (m, m), 0)
           == lax.broadcasted_iota(jnp.int32, (m, m), 1))
    xts, yts, acts = (xt0_ref, xt1_ref), (yt0_ref, yt1_ref), (act0_ref, act1_ref)

    def gather(lrow, xt_ref):
        for g in range(MOE_GROUP):
            for mi in range(m):
                row = pl.multiple_of(rowl_sm[lrow + g, mi], n_chunk)
                xt_ref[pl.ds(g * m + mi, n_chunk, stride=stride), :] = (
                    h2g_ref[pl.ds(row, n_chunk), :])

    def scatter(lrow, yt_ref):
        for g in range(MOE_GROUP):
            for g0 in range(0, m, SUBLANES):
                rows, vals = [], []
                for mi in range(g0, g0 + SUBLANES):
                    row = pl.multiple_of(rowl_sm[lrow + g, mi], n_chunk)
                    rows.append(row)
                    vals.append(acc_ref[pl.ds(row, n_chunk), :]
                                + yt_ref[pl.ds(g * m + mi, n_chunk, stride=stride), :])
                for row, val in zip(rows, vals):
                    acc_ref[pl.ds(row, n_chunk), :] = val

    def gate_up(xt_ref, act_ref):
        xs = jnp.concatenate([xt_ref[c * stride:c * stride + mg, :] for c in range(n_chunk)],
                             axis=1).astype(BF16)
        act = (_silu(jnp.dot(xs, wg_ref[0, 0], preferred_element_type=F32))
               * jnp.dot(xs, wu_ref[0, 0], preferred_element_type=F32))
        act_ref[...] = act.astype(BF16)

    def down(lrow, valid, act_ref, slot, yt_ref):
        cols = []
        for g in range(MOE_GROUP):
            w_row = jnp.where(g < valid, wl_ref[0, pl.ds(lrow + g, 1), :], 0.0)
            cols.append(jnp.sum(jnp.where(eye, w_row, 0.0), axis=1, keepdims=True))
        w_col = jnp.concatenate(cols, axis=0)
        act = act_ref[...]
        for c in range(0, n_chunk, 2):
            y = jnp.dot(act, wdh_ref[slot, :, c * LANES:(c + 2) * LANES],
                        preferred_element_type=F32) * w_col
            yt_ref[c * stride:c * stride + mg, :] = y[:, 0:LANES]
            yt_ref[(c + 1) * stride:(c + 1) * stride + mg, :] = y[:, LANES:2 * LANES]

    def by_parity(count, fn):
        for par in (0, 1):
            @pl.when((count & 1) == par)
            def _():
                fn(par)

    def block_loads():
        return (pltpu.make_async_copy(rowl_hbm.at[bi], rowl_sm, sems.at[0]),
                pltpu.make_async_copy(h2g_hbm.at[bi], h2g_ref, sems.at[1]))

    def block_store():
        return pltpu.make_async_copy(acc_ref, o_hbm.at[bi], sems.at[2])

    @pl.when(r == 0)
    def _():
        for cp in block_loads():
            cp.start()
        acc_ref[...] = jnp.zeros(acc_ref.shape, F32)
        act1_ref[...] = jnp.zeros(act1_ref.shape, BF16)
        yt0_ref[...] = jnp.zeros(yt0_ref.shape, F32)
        wdh_ref[...] = jnp.zeros(wdh_ref.shape, BF16)
        for k in (ST_PREV_START, ST_PREV_VALID, ST_PPREV_START, ST_GROUPS_DONE, ST_PREV_SLOT,
                  ST_EXPERTS_DONE):
            st_sm[k] = 0
        for cp in block_loads():
            cp.wait()
        gather(LIST_LEAD, xt0_ref)

    n_chunks = nch_sm[bi * N_EXPERTS + r]
    first_row = q0_sm[bi * N_EXPERTS + r] + LIST_LEAD

    @pl.when(n_chunks > 0)
    def _():
        slot = st_sm[ST_EXPERTS_DONE] & 1
        wdh_ref[slot] = wd_ref[0, 0]

        def group(j, carry):
            start = first_row + j * MOE_GROUP
            valid = jnp.minimum(MOE_GROUP, n_chunks - j * MOE_GROUP)
            nxt = jnp.minimum(start + MOE_GROUP, first_row + n_chunks)
            done = st_sm[ST_GROUPS_DONE]

            def body(par):
                gather(nxt, xts[1 - par])
                gate_up(xts[par], acts[par])
                down(st_sm[ST_PREV_START], st_sm[ST_PREV_VALID], acts[1 - par],
                     st_sm[ST_PREV_SLOT], yts[1 - par])
                scatter(st_sm[ST_PPREV_START], yts[par])

            by_parity(done, body)
            st_sm[ST_PPREV_START] = st_sm[ST_PREV_START]
            st_sm[ST_PREV_START] = start
            st_sm[ST_PREV_VALID] = valid
            st_sm[ST_PREV_SLOT] = slot
            st_sm[ST_GROUPS_DONE] = done + 1
            return carry

        lax.fori_loop(0, (n_chunks + MOE_GROUP - 1) // MOE_GROUP, group, 0)
        st_sm[ST_EXPERTS_DONE] = st_sm[ST_EXPERTS_DONE] + 1

    @pl.when(r == N_EXPERTS - 1)
    def _():
        def drain(par):
            scatter(st_sm[ST_PPREV_START], yts[1 - par])
            down(st_sm[ST_PREV_START], st_sm[ST_PREV_VALID], acts[par], st_sm[ST_PREV_SLOT],
                 yts[par])
            scatter(st_sm[ST_PREV_START], yts[par])

        by_parity(st_sm[ST_GROUPS_DONE] - 1, drain)
        block_store().start()
        block_store().wait()


def _moe(h2g, rowl, wl, nch, q0, w_gate, w_up, w_down, layer, t, d, tb):
    nb = t // tb
    n_chunk = d // LANES
    d_e = w_gate.shape[3]
    m = MOE_CHUNK
    stride = MOE_GROUP * m + SUBLANES
    nq = _num_list_rows(tb)

    def expert_of(r):
        return (r % N_GROUPS) * GROUP_SIZE + r // N_GROUPS

    cur_w = lambda bi, r, c, o: (layer, expert_of(r), 0, 0)
    per_block = lambda bi, r, c, o: (bi, 0, 0)
    staging = pltpu.VMEM((n_chunk * stride, LANES), F32)
    grid_spec = pltpu.PrefetchScalarGridSpec(
        num_scalar_prefetch=2,
        grid=(nb, N_EXPERTS),
        in_specs=[
            pl.BlockSpec(memory_space=pl.ANY),
            pl.BlockSpec(memory_space=pl.ANY),
            pl.BlockSpec((1, nq, m), per_block),
            pl.BlockSpec((1, 1, d, d_e), cur_w),
            pl.BlockSpec((1, 1, d, d_e), cur_w),
            pl.BlockSpec((1, 1, d_e, d), cur_w),
        ],
        out_specs=pl.BlockSpec(memory_space=pl.ANY),
        scratch_shapes=[
            pltpu.VMEM(((tb + 1) * n_chunk, LANES), F32),
            pltpu.VMEM(((tb + 1) * n_chunk, LANES), F32),
            staging, staging, staging, staging,
            pltpu.VMEM((MOE_GROUP * m, d_e), BF16),
            pltpu.VMEM((MOE_GROUP * m, d_e), BF16),
            pltpu.VMEM((2, d_e, d), BF16),
            pltpu.SMEM((nq, m), jnp.int32),
            pltpu.SMEM((SUBLANES,), jnp.int32),
            pltpu.SemaphoreType.DMA((3,)),
        ],
    )
    return pl.pallas_call(
        _moe_kernel,
        grid_spec=grid_spec,
        out_shape=jax.ShapeDtypeStruct((nb, (tb + 1) * n_chunk, LANES), F32),
        compiler_params=pltpu.CompilerParams(
            dimension_semantics=("arbitrary", "arbitrary"), vmem_limit_bytes=VMEM_LIMIT),
    )(nch.reshape(-1), q0.reshape(-1), h2g, rowl, wl, w_gate, w_up, w_down)


def _epilogue_kernel(x_ref, routed_ref, shared_ref, mod_ref, g_ref, o_ref):
    te, d = x_ref.shape
    n_chunk = d // LANES
    gt2 = mod_ref[0][5:6]
    routed = jnp.concatenate(
        [routed_ref[0, pl.ds(c, te, stride=n_chunk), :] for c in range(n_chunk)], axis=1)
    y = routed + shared_ref[...]
    o_ref[...] = x_ref[...] + gt2 * (y * _rms_scale(y) * g_ref[...])


def _epilogue(x1, routed, shared, mod, g_post, seq, tb):
    t, d = x1.shape
    n_chunk = d // LANES
    te = min(EPI_TILE, tb)
    n_sub = tb // te
    row = pl.BlockSpec((te, d), lambda bi, i: (bi * n_sub + i, 0))
    return pl.pallas_call(
        _epilogue_kernel,
        grid=(t // tb, n_sub),
        in_specs=[row,
                  pl.BlockSpec((1, te * n_chunk, LANES), lambda bi, i: (bi, i, 0)),
                  row,
                  pl.BlockSpec((1, 6, d), lambda bi, i: (bi * tb // seq, 0, 0)),
                  pl.BlockSpec((1, d), lambda bi, i: (0, 0))],
        out_specs=row,
        out_shape=jax.ShapeDtypeStruct((t, d), F32),
    )(x1, routed, shared, mod, g_post.reshape(1, d))


def kernel(x, c, w_ada, b_ada, g_pre_mix, g_post_mix, g_pre_ffn, g_post_ffn, w_in, conv_w,
           w_conv_out, w_pool_group, pool_scale, w_pool_proj, w_o, w_router, router_bias,
           w_exp_gate, w_exp_up, w_exp_down, w_sh_gate, w_sh_up, w_sh_down):
    b, s, d = x.shape
    depth = w_ada.shape[0]
    t = b * s
    tb = min(MOE_BLOCK, s)
    mods = _ada_mod(c, w_ada, b_ada).reshape(depth, b, 6, d)
    w_exp_gate, w_exp_up, w_exp_down = (w.astype(BF16) for w in (w_exp_gate, w_exp_up, w_exp_down))
    for l in range(depth):
        mod = mods[l]
        x = _token_mixer(x, mod, g_pre_mix[l], g_post_mix[l], w_in[l], conv_w[l], w_conv_out[l],
                         w_pool_group[l], pool_scale[l], w_pool_proj[l], w_o[l])
        x1 = x.reshape(t, d)
        h2g, shared, rowl, wl, nch, q0 = _route(
            x1, mod, g_pre_ffn[l], w_router[l], router_bias[l], w_sh_gate[l], w_sh_up[l],
            w_sh_down[l], s, tb)
        routed = _moe(h2g, rowl, wl, nch[:, :, 0], q0[:, :, 0],
                      w_exp_gate, w_exp_up, w_exp_down, l, t, d, tb)
        x = _epilogue(x1, routed, shared, mod, g_post_ffn[l], s, tb).reshape(b, s, d)
    return x
```

```python
import functools

import jax
import jax.numpy as jnp
from jax import lax
from jax.experimental import pallas as pl
from jax.experimental.pallas import tpu as pltpu

F32 = jnp.float32
BF16 = jnp.bfloat16

EPS = 1e-6
POOL_WINDOWS = (2, 4, 8, 16)
POOL_GROUP_DIM = 128
N_EXPERTS = 64
N_GROUPS = 8
GROUP_SIZE = 8
TOPK_GROUPS = 4
TOP_K = 8
ROUTED_SCALE = 2.5

LANES = 128
SUBLANES = 8
CONV_HALO = 8
POOL_HALO = 16
VMEM_LIMIT = 56 * 1024 * 1024

MIX_TILE = 256
ROUTE_LANES = 512
MOE_BLOCK = 4096
MOE_CHUNK = 128
LIST_LEAD = 2
EPI_TILE = 512


def _silu(v):
    return v * jax.nn.sigmoid(v)


def _rms_scale(v):
    return lax.rsqrt(jnp.mean(v * v, axis=-1, keepdims=True) + EPS)


def _ada_kernel(c_ref, w_ref, b_ref, o_ref):
    cond = _silu(c_ref[...])
    o_ref[0] = jnp.dot(cond, w_ref[0], preferred_element_type=F32,
                       precision=lax.Precision.HIGHEST) + b_ref[0]


def _ada_mod(c, w_ada, b_ada):
    depth, d, d6 = w_ada.shape
    b = c.shape[0]
    n_col = d6 // d
    return pl.pallas_call(
        _ada_kernel,
        grid=(depth, n_col),
        in_specs=[
            pl.BlockSpec((b, d), lambda l, n: (0, 0)),
            pl.BlockSpec((1, d, d), lambda l, n: (l, 0, n)),
            pl.BlockSpec((1, 1, d), lambda l, n: (l, 0, n)),
        ],
        out_specs=pl.BlockSpec((1, b, d), lambda l, n: (l, 0, n)),
        out_shape=jax.ShapeDtypeStruct((depth, b, d6), F32),
    )(c, w_ada, b_ada.reshape(depth, 1, d6))


def _ffn_residual(x, routed_ref, shared, gt2, g_post):
    rows, d = x.shape
    n_chunk = d // LANES
    routed = jnp.concatenate(
        [routed_ref[0, pl.ds(c, rows, stride=n_chunk), :] for c in range(n_chunk)], axis=1)
    y = routed + shared
    return x + gt2 * (y * _rms_scale(y) * g_post)


def _mixer_kernel(*refs, after_ffn):
    if after_ffn:
        (x_ref, routed_ref, shared_ref, modp_ref, gpp_ref), refs = refs[:5], refs[5:]
    else:
        x_ref, refs = refs[0], refs[1:]
    (mod_ref, gpre_ref, gpost_ref, win_ref, convw_ref, wco_ref, wpg_ref, pscale_ref, wpp_ref,
     wo_ref, o_ref, uext_ref, pext_ref) = refs
    j = pl.program_id(1)
    tm, d = x_ref.shape[1], x_ref.shape[2]
    d_pool = pext_ref.shape[1]

    @pl.when(j == 0)
    def _():
        uext_ref[0:CONV_HALO, :] = jnp.zeros((CONV_HALO, d), F32)
        pext_ref[0:POOL_HALO, :] = jnp.zeros((POOL_HALO, d_pool), F32)

    x = x_ref[0]
    if after_ffn:
        x = _ffn_residual(x, routed_ref, shared_ref[0], modp_ref[0][5:6], gpp_ref[...])
    mod = mod_ref[0]
    sh1, sc1, gt1 = mod[0:1], mod[1:2], mod[2:3]
    h = x * _rms_scale(x) * gpre_ref[...] * (1.0 + sc1) + sh1
    hb = h.astype(BF16)

    def proj(lo, hi):
        return jnp.dot(hb, win_ref[:, lo:hi], preferred_element_type=F32)

    u = proj(d, 2 * d) * proj(2 * d, 3 * d)
    uext_ref[CONV_HALO:CONV_HALO + tm, :] = u
    cw = convw_ref[...]
    conv = (cw[2:3] * u
            + cw[1:2] * uext_ref[CONV_HALO - 1:CONV_HALO - 1 + tm, :]
            + cw[0:1] * uext_ref[CONV_HALO - 2:CONV_HALO - 2 + tm, :])
    uext_ref[0:CONV_HALO, :] = u[tm - CONV_HALO:tm, :]
    y_conv = jnp.dot((proj(0, d) * conv).astype(BF16), wco_ref[...], preferred_element_type=F32)

    up = proj(3 * d, 3 * d + d_pool)
    pext_ref[POOL_HALO:POOL_HALO + tm, :] = up
    pos = j * tm + lax.broadcasted_iota(jnp.int32, (tm, 1), 0)
    zs = []
    for g, w in enumerate(POOL_WINDOWS):
        c0 = g * POOL_GROUP_DIM
        ug = up[:, c0:c0 + POOL_GROUP_DIM]
        acc = ug
        for k in range(1, w):
            acc = acc + pext_ref[POOL_HALO - k:POOL_HALO - k + tm, c0:c0 + POOL_GROUP_DIM]
        inv_cnt = 1.0 / jnp.minimum(pos + 1, w).astype(F32)
        diff = acc * inv_cnt - ug
        zs.append(jnp.dot(diff.astype(BF16), wpg_ref[g], preferred_element_type=F32))
    pext_ref[0:POOL_HALO, :] = up[tm - POOL_HALO:tm, :]
    z = jnp.concatenate(zs, axis=1) * pscale_ref[...]
    y_pool = jnp.dot(z.astype(BF16), wpp_ref[...], preferred_element_type=F32)

    a_conv = proj(3 * d + d_pool, 4 * d + d_pool)
    a_pool = proj(4 * d + d_pool, 5 * d + d_pool)
    merged = jax.nn.sigmoid(a_conv) * y_conv + jax.nn.sigmoid(a_pool) * y_pool
    y = jnp.dot(merged.astype(BF16), wo_ref[...], preferred_element_type=F32)
    o_ref[0] = x + gt1 * (y * _rms_scale(y) * gpost_ref[...])


def _token_mixer(x, pending_ffn, mod, g_pre, g_post, w_in, conv_w, w_conv_out, w_pool_group,
                 pool_scale, w_pool_proj, w_o):
    b, s, d = x.shape
    d_in = w_in.shape[1]
    d_pool = w_pool_proj.shape[0]
    n_chunk = d // LANES
    tm = min(MIX_TILE, s)
    const2 = lambda bi, j: (0, 0)
    const3 = lambda bi, j: (0, 0, 0)
    rows = pl.BlockSpec((1, tm, d), lambda bi, j: (bi, j, 0))
    per_batch = pl.BlockSpec((1, 6, d), lambda bi, j: (bi, 0, 0))
    prev_specs, prev_args = [], []
    if pending_ffn is not None:
        routed, shared, mod_p, g_post_p, tb = pending_ffn
        tiles = tb // tm
        prev_specs = [
            pl.BlockSpec((1, tm * n_chunk, LANES),
                         lambda bi, j: ((bi * (s // tm) + j) // tiles, (bi * (s // tm) + j) % tiles, 0)),
            rows, per_batch, pl.BlockSpec((1, d), const2)]
        prev_args = [routed, shared.reshape(b, s, d), mod_p, g_post_p.reshape(1, d)]
    return pl.pallas_call(
        functools.partial(_mixer_kernel, after_ffn=pending_ffn is not None),
        grid=(b, s // tm),
        in_specs=[
            rows,
            *prev_specs,
            per_batch,
            pl.BlockSpec((1, d), const2),
            pl.BlockSpec((1, d), const2),
            pl.BlockSpec((d, d_in), const2),
            pl.BlockSpec((3, d), const2),
            pl.BlockSpec((d, d), const2),
            pl.BlockSpec(w_pool_group.shape, const3),
            pl.BlockSpec((1, d_pool), const2),
            pl.BlockSpec((d_pool, d), const2),
            pl.BlockSpec((d, d), const2),
        ],
        out_specs=pl.BlockSpec((1, tm, d), lambda bi, j: (bi, j, 0)),
        out_shape=jax.ShapeDtypeStruct(x.shape, F32),
        scratch_shapes=[
            pltpu.VMEM((CONV_HALO + tm, d), F32),
            pltpu.VMEM((POOL_HALO + tm, d_pool), F32),
        ],
        compiler_params=pltpu.CompilerParams(
            dimension_semantics=("arbitrary", "arbitrary"), vmem_limit_bytes=VMEM_LIMIT),
    )(x, *prev_args, mod, g_pre.reshape(1, d), g_post.reshape(1, d), w_in.astype(BF16), conv_w,
      w_conv_out.astype(BF16), w_pool_group.astype(BF16), pool_scale.reshape(1, d_pool),
      w_pool_proj.astype(BF16), w_o.astype(BF16))


def _wins(other, v, tie_i):
    return jnp.where(other > v, 1, 0) + jnp.where(other == v, tie_i, 0)


def _route_kernel(x_ref, mod_ref, g_ref, wrt_ref, bias_ref, wsg_ref, wsu_ref, wsd_ref,
                  h2g_ref, shared_ref, rowl_ref, wl_ref, nch_ref, q0_ref,
                  lg_ref, pre_ref, rho_ref, wd_ref):
    i = pl.program_id(1)
    lc, d = x_ref.shape
    n_sub = lg_ref.shape[0]
    n_chunk = d // LANES

    @pl.when(i < n_sub)
    def _():
        mod = mod_ref[0]
        sh2, sc2 = mod[3:4], mod[4:5]

        @pl.when(i == 0)
        def _():
            spare = n_sub * lc * n_chunk
            h2g_ref[0, spare:spare + n_chunk, :] = jnp.zeros((n_chunk, LANES), F32)

        x = x_ref[...]
        h = x * _rms_scale(x) * g_ref[...] * (1.0 + sc2) + sh2
        for c in range(n_chunk):
            h2g_ref[0, pl.ds(i * lc * n_chunk + c, lc, stride=n_chunk), :] = (
                h[:, c * LANES:(c + 1) * LANES])
        hb = h.astype(BF16)
        act = (_silu(jnp.dot(hb, wsg_ref[...], preferred_element_type=F32))
               * jnp.dot(hb, wsu_ref[...], preferred_element_type=F32))
        shared_ref[...] = jnp.dot(act.astype(BF16), wsd_ref[...], preferred_element_type=F32)
        lg_ref[i] = lax.dot_general(
            wrt_ref[...], h, (((1,), (1,)), ((), ())), preferred_element_type=F32,
            precision=lax.Precision.HIGHEST)

    @pl.when(i == n_sub)
    def _():
        _route_plan(bias_ref, rowl_ref, wl_ref, nch_ref, q0_ref, lg_ref, pre_ref, rho_ref, wd_ref,
                    lc, n_chunk)


def _route_plan(bias_ref, rowl_ref, wl_ref, nch_ref, q0_ref, lg_ref, pre_ref, rho_ref, wd_ref,
                lc, n_chunk):
    n_sub = lg_ref.shape[0]
    tb = n_sub * lc
    gidx = lax.broadcasted_iota(jnp.int32, (N_GROUPS, lc), 0)
    tie = [None] + [jnp.where(gidx >= r, 1, 0) for r in range(1, N_GROUPS)]
    tri = (lax.broadcasted_iota(jnp.int32, (lc, lc), 0)
           < lax.broadcasted_iota(jnp.int32, (lc, lc), 1)).astype(BF16)
    carry = jnp.zeros((N_EXPERTS, 1), F32)
    neg_inf = jnp.float32(-jnp.inf)
    for ci in range(tb // lc):
        c0 = ci * lc
        s_all = jax.nn.sigmoid(lg_ref[ci])
        aff = [s_all[GROUP_SIZE * jj:GROUP_SIZE * (jj + 1), :] for jj in range(GROUP_SIZE)]
        sel = [aff[jj] + bias_ref[GROUP_SIZE * jj:GROUP_SIZE * (jj + 1), :]
               for jj in range(GROUP_SIZE)]
        m1, m2 = sel[0], jnp.full_like(sel[0], neg_inf)
        for jj in range(1, GROUP_SIZE):
            m2 = jnp.maximum(m2, jnp.minimum(m1, sel[jj]))
            m1 = jnp.maximum(m1, sel[jj])
        gs = m1 + m2
        beaten = jnp.zeros((N_GROUPS, lc), jnp.int32)
        for r in range(1, N_GROUPS):
            other = pltpu.roll(gs, r, axis=0)
            beaten = beaten + _wins(other, gs, tie[r])
        gmask = beaten < TOPK_GROUPS
        masked = [jnp.where(gmask, sel[jj], neg_inf) for jj in range(GROUP_SIZE)]
        rolled = [[masked[jj]] + [pltpu.roll(masked[jj], r, axis=0) for r in range(1, N_GROUPS)]
                  for jj in range(GROUP_SIZE)]
        rho = []
        for jj in range(GROUP_SIZE):
            v = masked[jj]
            cnt = jnp.zeros((N_GROUPS, lc), jnp.int32)
            for j2 in range(GROUP_SIZE):
                for r in range(N_GROUPS):
                    if r == 0 and j2 == jj:
                        continue
                    other = rolled[j2][r]
                    if r == 0:
                        wins = (other >= v) if j2 < jj else (other > v)
                        cnt = cnt + jnp.where(wins, 1, 0)
                    else:
                        cnt = cnt + _wins(other, v, tie[r])
            rho.append(cnt)
        chosen = [rho[jj] < TOP_K for jj in range(GROUP_SIZE)]
        ssum = jnp.zeros((N_GROUPS, lc), F32)
        for jj in range(GROUP_SIZE):
            ssum = ssum + jnp.where(chosen[jj], aff[jj], 0.0)
        ssum = jnp.sum(ssum, axis=0, keepdims=True)
        wdense = [jnp.where(chosen[jj], aff[jj] / ssum * ROUTED_SCALE, 0.0)
                  for jj in range(GROUP_SIZE)]
        chosen_f = jnp.concatenate([c.astype(F32) for c in chosen], axis=0)
        prefix = jnp.dot(chosen_f.astype(BF16), tri, preferred_element_type=F32) + carry
        carry = carry + jnp.sum(chosen_f, axis=1, keepdims=True)
        pre_ref[:, c0:c0 + lc] = prefix
        rho_ref[:, c0:c0 + lc] = jnp.concatenate(rho, axis=0)
        wd_ref[:, c0:c0 + lc] = jnp.concatenate(wdense, axis=0)

    m = MOE_CHUNK
    nch_b = jnp.broadcast_to(jnp.floor((carry + (m - 1)) * (1.0 / m)), (N_EXPERTS, LANES))
    lower = (lax.broadcasted_iota(jnp.int32, (N_EXPERTS, N_EXPERTS), 1)
             < lax.broadcasted_iota(jnp.int32, (N_EXPERTS, N_EXPERTS), 0)).astype(F32)
    q0_b = jnp.dot(lower, nch_b, preferred_element_type=F32, precision=lax.Precision.HIGHEST)
    nch_ref[0] = nch_b.astype(jnp.int32)
    q0_ref[0] = q0_b.astype(jnp.int32)
    offs_col = q0_b[:, 0:1] * m

    nq = rowl_ref.shape[1]
    iota_q = lax.broadcasted_iota(jnp.int32, (nq, lc), 0).astype(F32)
    iota_r = lax.broadcasted_iota(jnp.int32, (m, lc), 0).astype(F32)
    lists = jnp.zeros((nq, 5 * m), F32)
    for ci in range(tb // lc):
        c0 = ci * lc
        dest_dense = pre_ref[:, c0:c0 + lc] + offs_col
        rho_c = rho_ref[:, c0:c0 + lc]
        w_c = wd_ref[:, c0:c0 + lc]
        tokv = (c0 + 1 + lax.broadcasted_iota(jnp.int32, (1, lc), 1)).astype(F32)
        tok_hi = jnp.floor(tokv * (1.0 / 64))
        tok_lo = tokv - 64.0 * tok_hi
        for k in range(TOP_K):
            hit = rho_c == k
            dk = jnp.sum(jnp.where(hit, dest_dense, 0.0), axis=0, keepdims=True)
            wk = jnp.sum(jnp.where(hit, w_c, 0.0), axis=0, keepdims=True)
            qk = jnp.floor(dk * (1.0 / m))
            rk = dk - m * qk
            w_hi = wk.astype(BF16).astype(F32)
            w_mid = (wk - w_hi).astype(BF16).astype(F32)
            w_lo = wk - w_hi - w_mid
            onehot_q = jnp.where(iota_q == qk + LIST_LEAD, 1.0, 0.0).astype(BF16)
            rmask = iota_r == rk
            vals = jnp.concatenate(
                [jnp.where(rmask, piece, 0.0) for piece in (tok_hi, tok_lo, w_hi, w_mid, w_lo)],
                axis=0).astype(BF16)
            lists = lists + lax.dot_general(onehot_q, vals, (((1,), (1,)), ((), ())),
                                            preferred_element_type=F32)
    tok = lists[:, 0:m] * 64.0 + lists[:, m:2 * m]
    tile = jnp.where(tok == 0.0, float(tb), tok - 1.0)
    rowl_ref[0] = (tile * n_chunk).astype(jnp.int32)
    wl_ref[0] = (lists[:, 2 * m:3 * m] + lists[:, 3 * m:4 * m]) + lists[:, 4 * m:5 * m]


def _route(x1, mod, g_pre, w_router, router_bias, w_sh_gate, w_sh_up, w_sh_down, seq, tb):
    t, d = x1.shape
    nb = t // tb
    n_chunk = d // LANES
    d_sh = w_sh_gate.shape[1]
    perm = jnp.arange(N_EXPERTS).reshape(N_GROUPS, GROUP_SIZE).T.reshape(-1)
    wrt = w_router.T[perm]
    bias = router_bias[perm].reshape(N_EXPERTS, 1)
    nq = _num_list_rows(tb)
    lc = min(ROUTE_LANES, tb)
    n_sub = tb // lc
    const2 = lambda bi, i: (0, 0)
    per_block = lambda bi, i: (bi, 0, 0)
    sub_tile = lambda bi, i: (bi * n_sub + jnp.minimum(i, n_sub - 1), 0)
    outs = pl.pallas_call(
        _route_kernel,
        grid=(nb, n_sub + 1),
        in_specs=[
            pl.BlockSpec((lc, d), sub_tile),
            pl.BlockSpec((1, 6, d), lambda bi, i: (bi * tb // seq, 0, 0)),
            pl.BlockSpec((1, d), const2),
            pl.BlockSpec((N_EXPERTS, d), const2),
            pl.BlockSpec((N_EXPERTS, 1), const2),
            pl.BlockSpec((d, d_sh), const2),
            pl.BlockSpec((d, d_sh), const2),
            pl.BlockSpec((d_sh, d), const2),
        ],
        out_specs=[
            pl.BlockSpec((1, (tb + 1) * n_chunk, LANES), per_block),
            pl.BlockSpec((lc, d), sub_tile),
            pl.BlockSpec((1, nq, MOE_CHUNK), per_block),
            pl.BlockSpec((1, nq, MOE_CHUNK), per_block),
            pl.BlockSpec((1, N_EXPERTS, LANES), per_block),
            pl.BlockSpec((1, N_EXPERTS, LANES), per_block),
        ],
        out_shape=[
            jax.ShapeDtypeStruct((nb, (tb + 1) * n_chunk, LANES), F32),
            jax.ShapeDtypeStruct((t, d), F32),
            jax.ShapeDtypeStruct((nb, nq, MOE_CHUNK), jnp.int32),
            jax.ShapeDtypeStruct((nb, nq, MOE_CHUNK), F32),
            jax.ShapeDtypeStruct((nb, N_EXPERTS, LANES), jnp.int32),
            jax.ShapeDtypeStruct((nb, N_EXPERTS, LANES), jnp.int32),
        ],
        scratch_shapes=[
            pltpu.VMEM((n_sub, N_EXPERTS, lc), F32),
            pltpu.VMEM((N_EXPERTS, tb), F32),
            pltpu.VMEM((N_EXPERTS, tb), jnp.int32),
            pltpu.VMEM((N_EXPERTS, tb), F32),
        ],
        compiler_params=pltpu.CompilerParams(
            dimension_semantics=("arbitrary", "arbitrary"), vmem_limit_bytes=VMEM_LIMIT),
    )(x1, mod, g_pre.reshape(1, d), wrt, bias, w_sh_gate.astype(BF16), w_sh_up.astype(BF16),
      w_sh_down.astype(BF16))
    return outs


def _num_list_rows(tb):
    return TOP_K * tb // MOE_CHUNK + N_EXPERTS + SUBLANES


def _moe_kernel(nch_sm, q0_sm, prev_sm, h2g_hbm, rowl_hbm, wl_ref, wg_ref, wu_ref, wd_ref,
                wdp_ref, o_hbm, h2g_ref, acc_ref, xt0_ref, xt1_ref, yt0_ref, yt1_ref, act0_ref,
                act1_ref, rowl_sm, sems):
    bi = pl.program_id(0)
    r = pl.program_id(1)
    n_chunk = xt0_ref.shape[0] // (MOE_CHUNK + SUBLANES)
    m = MOE_CHUNK
    stride = m + SUBLANES
    eye = (lax.broadcasted_iota(jnp.int32, (m, m), 0)
           == lax.broadcasted_iota(jnp.int32, (m, m), 1))
    xts, yts, acts = (xt0_ref, xt1_ref), (yt0_ref, yt1_ref), (act0_ref, act1_ref)

    def gather(lrow, xt_ref):
        for mi in range(m):
            row = pl.multiple_of(rowl_sm[lrow, mi], n_chunk)
            xt_ref[pl.ds(mi, n_chunk, stride=stride), :] = h2g_ref[pl.ds(row, n_chunk), :]

    def scatter(lrow, yt_ref):
        for g0 in range(0, m, SUBLANES):
            rows, vals = [], []
            for mi in range(g0, g0 + SUBLANES):
                row = pl.multiple_of(rowl_sm[lrow, mi], n_chunk)
                rows.append(row)
                vals.append(acc_ref[pl.ds(row, n_chunk), :]
                            + yt_ref[pl.ds(mi, n_chunk, stride=stride), :])
            for row, val in zip(rows, vals):
                acc_ref[pl.ds(row, n_chunk), :] = val

    def gate_up(xt_ref, act_ref):
        xs = jnp.concatenate([xt_ref[c * stride:c * stride + m, :] for c in range(n_chunk)],
                             axis=1).astype(BF16)
        act = (_silu(jnp.dot(xs, wg_ref[0, 0], preferred_element_type=F32))
               * jnp.dot(xs, wu_ref[0, 0], preferred_element_type=F32))
        act_ref[...] = act.astype(BF16)

    def down(lrow, act_ref, w_ref, yt_ref):
        y = jnp.dot(act_ref[...], w_ref[0, 0], preferred_element_type=F32)
        w_row = wl_ref[0, pl.ds(lrow, 1), :]
        w_col = jnp.sum(jnp.where(eye, w_row, 0.0), axis=1, keepdims=True)
        y = y * w_col
        for c in range(n_chunk):
            yt_ref[c * stride:c * stride + m, :] = y[:, c * LANES:(c + 1) * LANES]

    def step(q, par, w_down_ref):
        cur, oth = par, 1 - par
        gather(q + 1 + LIST_LEAD, xts[oth])
        gate_up(xts[cur], acts[cur])
        down(q - 1 + LIST_LEAD, acts[oth], w_down_ref, yts[oth])
        scatter(q - 2 + LIST_LEAD, yts[cur])

    def by_parity(q, fn):
        for par in (0, 1):
            @pl.when((q & 1) == par)
            def _():
                fn(par)

    def block_loads():
        return (pltpu.make_async_copy(rowl_hbm.at[bi], rowl_sm, sems.at[0]),
                pltpu.make_async_copy(h2g_hbm.at[bi], h2g_ref, sems.at[1]))

    def block_store():
        return pltpu.make_async_copy(acc_ref, o_hbm.at[bi], sems.at[2])

    @pl.when(r == 0)
    def _():
        for cp in block_loads():
            cp.start()
        acc_ref[...] = jnp.zeros(acc_ref.shape, F32)
        act1_ref[...] = jnp.zeros(act1_ref.shape, BF16)
        yt0_ref[...] = jnp.zeros(yt0_ref.shape, F32)
        for cp in block_loads():
            cp.wait()
        gather(LIST_LEAD, xt0_ref)

    n_chunks = nch_sm[bi * N_EXPERTS + r]
    q_first = q0_sm[bi * N_EXPERTS + r]

    @pl.when(n_chunks > 0)
    def _():
        by_parity(q_first, lambda par: step(q_first, par, wdp_ref))

    def chunk(ci, carry):
        q = q_first + ci
        by_parity(q, lambda par: step(q, par, wd_ref))
        return carry

    lax.fori_loop(1, n_chunks, chunk, 0)

    @pl.when(r == N_EXPERTS - 1)
    def _():
        q_last = q_first + n_chunks - 1

        def drain(par, w_down_ref):
            scatter(q_last - 1 + LIST_LEAD, yts[1 - par])
            down(q_last + LIST_LEAD, acts[par], w_down_ref, yts[par])
            scatter(q_last + LIST_LEAD, yts[par])

        @pl.when(n_chunks > 0)
        def _():
            by_parity(q_last, lambda par: drain(par, wd_ref))

        @pl.when(n_chunks == 0)
        def _():
            by_parity(q_last, lambda par: drain(par, wdp_ref))

        block_store().start()
        block_store().wait()


def _moe(h2g, rowl, wl, nch, q0, w_gate, w_up, w_down, layer, t, d, tb):
    nb = t // tb
    n_chunk = d // LANES
    d_e = w_gate.shape[3]
    m = MOE_CHUNK
    stride = m + SUBLANES
    nq = _num_list_rows(tb)

    def expert_of(r):
        return (r % N_GROUPS) * GROUP_SIZE + r // N_GROUPS

    rows = jnp.where(nch > 0, jnp.arange(N_EXPERTS, dtype=jnp.int32)[None, :], -1)
    last = lax.cummax(rows, axis=1)
    prev = jnp.maximum(jnp.concatenate([jnp.full((nb, 1), -1, jnp.int32), last[:, :-1]], axis=1), 0)

    cur_w = lambda bi, r, c, o, p: (layer, expert_of(r), 0, 0)
    prev_w = lambda bi, r, c, o, p: (layer, expert_of(p[bi * N_EXPERTS + r]), 0, 0)
    per_block = lambda bi, r, c, o, p: (bi, 0, 0)
    staging = pltpu.VMEM((n_chunk * stride, LANES), F32)
    grid_spec = pltpu.PrefetchScalarGridSpec(
        num_scalar_prefetch=3,
        grid=(nb, N_EXPERTS),
        in_specs=[
            pl.BlockSpec(memory_space=pl.ANY),
            pl.BlockSpec(memory_space=pl.ANY),
            pl.BlockSpec((1, nq, m), per_block),
            pl.BlockSpec((1, 1, d, d_e), cur_w),
            pl.BlockSpec((1, 1, d, d_e), cur_w),
            pl.BlockSpec((1, 1, d_e, d), cur_w),
            pl.BlockSpec((1, 1, d_e, d), prev_w),
        ],
        out_specs=pl.BlockSpec(memory_space=pl.ANY),
        scratch_shapes=[
            pltpu.VMEM(((tb + 1) * n_chunk, LANES), F32),
            pltpu.VMEM(((tb + 1) * n_chunk, LANES), F32),
            staging, staging, staging, staging,
            pltpu.VMEM((m, d_e), BF16),
            pltpu.VMEM((m, d_e), BF16),
            pltpu.SMEM((nq, m), jnp.int32),
            pltpu.SemaphoreType.DMA((3,)),
        ],
    )
    return pl.pallas_call(
        _moe_kernel,
        grid_spec=grid_spec,
        out_shape=jax.ShapeDtypeStruct((nb, (tb + 1) * n_chunk, LANES), F32),
        compiler_params=pltpu.CompilerParams(
            dimension_semantics=("arbitrary", "arbitrary"), vmem_limit_bytes=VMEM_LIMIT),
    )(nch.reshape(-1), q0.reshape(-1), prev.reshape(-1), h2g, rowl, wl, w_gate, w_up, w_down,
      w_down)


def _epilogue_kernel(x_ref, routed_ref, shared_ref, mod_ref, g_ref, o_ref):
    o_ref[...] = _ffn_residual(x_ref[...], routed_ref, shared_ref[...], mod_ref[0][5:6], g_ref[...])


def _epilogue(x1, routed, shared, mod, g_post, seq, tb):
    t, d = x1.shape
    n_chunk = d // LANES
    te = min(EPI_TILE, tb)
    n_sub = tb // te
    row = pl.BlockSpec((te, d), lambda bi, i: (bi * n_sub + i, 0))
    return pl.pallas_call(
        _epilogue_kernel,
        grid=(t // tb, n_sub),
        in_specs=[row,
                  pl.BlockSpec((1, te * n_chunk, LANES), lambda bi, i: (bi, i, 0)),
                  row,
                  pl.BlockSpec((1, 6, d), lambda bi, i: (bi * tb // seq, 0, 0)),
                  pl.BlockSpec((1, d), lambda bi, i: (0, 0))],
        out_specs=row,
        out_shape=jax.ShapeDtypeStruct((t, d), F32),
    )(x1, routed, shared, mod, g_post.reshape(1, d))


def kernel(x, c, w_ada, b_ada, g_pre_mix, g_post_mix, g_pre_ffn, g_post_ffn, w_in, conv_w,
           w_conv_out, w_pool_group, pool_scale, w_pool_proj, w_o, w_router, router_bias,
           w_exp_gate, w_exp_up, w_exp_down, w_sh_gate, w_sh_up, w_sh_down):
    b, s, d = x.shape
    depth = w_ada.shape[0]
    t = b * s
    tb = min(MOE_BLOCK, s)
    mods = _ada_mod(c, w_ada, b_ada).reshape(depth, b, 6, d)
    w_exp_gate, w_exp_up, w_exp_down = (w.astype(BF16) for w in (w_exp_gate, w_exp_up, w_exp_down))
    pending_ffn = None
    for l in range(depth):
        mod = mods[l]
        x = _token_mixer(x, pending_ffn, mod, g_pre_mix[l], g_post_mix[l], w_in[l], conv_w[l],
                         w_conv_out[l], w_pool_group[l], pool_scale[l], w_pool_proj[l], w_o[l])
        x1 = x.reshape(t, d)
        h2g, shared, rowl, wl, nch, q0 = _route(
            x1, mod, g_pre_ffn[l], w_router[l], router_bias[l], w_sh_gate[l], w_sh_up[l],
            w_sh_down[l], s, tb)
        routed = _moe(h2g, rowl, wl, nch[:, :, 0], q0[:, :, 0],
                      w_exp_gate, w_exp_up, w_exp_down, l, t, d, tb)
        pending_ffn = (routed, shared, mod, g_post_ffn[l], tb)
    routed, shared, mod, g_post, tb = pending_ffn
    return _epilogue(x.reshape(t, d), routed, shared, mod, g_post, s, tb).reshape(b, s, d)
```

```python
import functools

import jax
import jax.numpy as jnp
from jax import lax
from jax.experimental import pallas as pl
from jax.experimental.pallas import tpu as pltpu

F32 = jnp.float32
BF16 = jnp.bfloat16

EPS = 1e-6
POOL_WINDOWS = (2, 4, 8, 16)
POOL_GROUP_DIM = 128
N_EXPERTS = 64
N_GROUPS = 8
GROUP_SIZE = 8
TOPK_GROUPS = 4
TOP_K = 8
ROUTED_SCALE = 2.5

LANES = 128
SUBLANES = 8
CONV_HALO = 8
POOL_HALO = 16
VMEM_LIMIT = 56 * 1024 * 1024

MIX_TILE = 512
MIX_CHAINS = 2
ROUTE_LANES = 512
MOE_BLOCK = 4096
MOE_CHUNK = 128
LIST_LEAD = 2
EPI_TILE = 512


def _silu(v):
    return v * jax.nn.sigmoid(v)


def _rms_scale(v):
    return lax.rsqrt(jnp.mean(v * v, axis=-1, keepdims=True) + EPS)


def _ada_kernel(c_ref, w_ref, b_ref, o_ref):
    cond = _silu(c_ref[...])
    o_ref[0] = jnp.dot(cond, w_ref[0], preferred_element_type=F32,
                       precision=lax.Precision.HIGHEST) + b_ref[0]


def _ada_mod(c, w_ada, b_ada):
    depth, d, d6 = w_ada.shape
    b = c.shape[0]
    n_col = d6 // d
    return pl.pallas_call(
        _ada_kernel,
        grid=(depth, n_col),
        in_specs=[
            pl.BlockSpec((b, d), lambda l, n: (0, 0)),
            pl.BlockSpec((1, d, d), lambda l, n: (l, 0, n)),
            pl.BlockSpec((1, 1, d), lambda l, n: (l, 0, n)),
        ],
        out_specs=pl.BlockSpec((1, b, d), lambda l, n: (l, 0, n)),
        out_shape=jax.ShapeDtypeStruct((depth, b, d6), F32),
    )(c, w_ada, b_ada.reshape(depth, 1, d6))


def _ffn_residual(x, routed_ref, shared, gt2, g_post, row0=0):
    rows, d = x.shape
    n_chunk = d // LANES
    routed = jnp.concatenate(
        [routed_ref[0, pl.ds(row0 * n_chunk + c, rows, stride=n_chunk), :]
         for c in range(n_chunk)], axis=1)
    y = routed + shared
    return x + gt2 * (y * _rms_scale(y) * g_post)


def _mixer_kernel(*refs, after_ffn):
    if after_ffn:
        (x_ref, routed_ref, shared_ref, modp_ref, gpp_ref), refs = refs[:5], refs[5:]
    else:
        x_ref, refs = refs[0], refs[1:]
    (mod_ref, gpre_ref, gpost_ref, win_ref, convw_ref, wco_ref, wpg_ref, pscale_ref, wpp_ref,
     wo_ref, o_ref, uext_ref, pext_ref) = refs
    j = pl.program_id(1)
    tm, d = x_ref.shape[1], x_ref.shape[2]
    d_pool = pext_ref.shape[1]

    @pl.when(j == 0)
    def _():
        uext_ref[0:CONV_HALO, :] = jnp.zeros((CONV_HALO, d), F32)
        pext_ref[0:POOL_HALO, :] = jnp.zeros((POOL_HALO, d_pool), F32)

    mod = mod_ref[0]
    sh1, sc1, gt1 = mod[0:1], mod[1:2], mod[2:3]
    cw = convw_ref[...]
    ts = tm // MIX_CHAINS
    for ch in range(MIX_CHAINS):
        r0 = ch * ts
        x = x_ref[0, r0:r0 + ts, :]
        if after_ffn:
            x = _ffn_residual(x, routed_ref, shared_ref[0, r0:r0 + ts, :], modp_ref[0][5:6],
                              gpp_ref[...], row0=r0)
        h = x * _rms_scale(x) * gpre_ref[...] * (1.0 + sc1) + sh1
        hb = h.astype(BF16)

        def proj(lo, hi):
            return jnp.dot(hb, win_ref[:, lo:hi], preferred_element_type=F32)

        u = proj(d, 2 * d) * proj(2 * d, 3 * d)
        u0 = CONV_HALO + r0
        uext_ref[u0:u0 + ts, :] = u
        conv = (cw[2:3] * u
                + cw[1:2] * uext_ref[u0 - 1:u0 - 1 + ts, :]
                + cw[0:1] * uext_ref[u0 - 2:u0 - 2 + ts, :])
        y_conv = jnp.dot((proj(0, d) * conv).astype(BF16), wco_ref[...],
                         preferred_element_type=F32)

        up = proj(3 * d, 3 * d + d_pool)
        p0 = POOL_HALO + r0
        pext_ref[p0:p0 + ts, :] = up
        pos = j * tm + r0 + lax.broadcasted_iota(jnp.int32, (ts, 1), 0)
        zs = []
        for g, w in enumerate(POOL_WINDOWS):
            c0 = g * POOL_GROUP_DIM
            ug = up[:, c0:c0 + POOL_GROUP_DIM]
            acc = ug
            for k in range(1, w):
                acc = acc + pext_ref[p0 - k:p0 - k + ts, c0:c0 + POOL_GROUP_DIM]
            inv_cnt = 1.0 / jnp.minimum(pos + 1, w).astype(F32)
            diff = acc * inv_cnt - ug
            zs.append(jnp.dot(diff.astype(BF16), wpg_ref[g], preferred_element_type=F32))
        z = jnp.concatenate(zs, axis=1) * pscale_ref[...]
        y_pool = jnp.dot(z.astype(BF16), wpp_ref[...], preferred_element_type=F32)

        a_conv = proj(3 * d + d_pool, 4 * d + d_pool)
        a_pool = proj(4 * d + d_pool, 5 * d + d_pool)
        merged = jax.nn.sigmoid(a_conv) * y_conv + jax.nn.sigmoid(a_pool) * y_pool
        y = jnp.dot(merged.astype(BF16), wo_ref[...], preferred_element_type=F32)
        o_ref[0, r0:r0 + ts, :] = x + gt1 * (y * _rms_scale(y) * gpost_ref[...])

    uext_ref[0:CONV_HALO, :] = uext_ref[tm:tm + CONV_HALO, :]
    pext_ref[0:POOL_HALO, :] = pext_ref[tm:tm + POOL_HALO, :]


def _token_mixer(x, pending_ffn, mod, g_pre, g_post, w_in, conv_w, w_conv_out, w_pool_group,
                 pool_scale, w_pool_proj, w_o):
    b, s, d = x.shape
    d_in = w_in.shape[1]
    d_pool = w_pool_proj.shape[0]
    n_chunk = d // LANES
    tm = min(MIX_TILE, s)
    const2 = lambda bi, j: (0, 0)
    const3 = lambda bi, j: (0, 0, 0)
    rows = pl.BlockSpec((1, tm, d), lambda bi, j: (bi, j, 0))
    per_batch = pl.BlockSpec((1, 6, d), lambda bi, j: (bi, 0, 0))
    prev_specs, prev_args = [], []
    if pending_ffn is not None:
        routed, shared, mod_p, g_post_p, tb = pending_ffn
        tiles = tb // tm
        prev_specs = [
            pl.BlockSpec((1, tm * n_chunk, LANES),
                         lambda bi, j: ((bi * (s // tm) + j) // tiles, (bi * (s // tm) + j) % tiles, 0)),
            rows, per_batch, pl.BlockSpec((1, d), const2)]
        prev_args = [routed, shared.reshape(b, s, d), mod_p, g_post_p.reshape(1, d)]
    return pl.pallas_call(
        functools.partial(_mixer_kernel, after_ffn=pending_ffn is not None),
        grid=(b, s // tm),
        in_specs=[
            rows,
            *prev_specs,
            per_batch,
            pl.BlockSpec((1, d), const2),
            pl.BlockSpec((1, d), const2),
            pl.BlockSpec((d, d_in), const2),
            pl.BlockSpec((3, d), const2),
            pl.BlockSpec((d, d), const2),
            pl.BlockSpec(w_pool_group.shape, const3),
            pl.BlockSpec((1, d_pool), const2),
            pl.BlockSpec((d_pool, d), const2),
            pl.BlockSpec((d, d), const2),
        ],
        out_specs=pl.BlockSpec((1, tm, d), lambda bi, j: (bi, j, 0)),
        out_shape=jax.ShapeDtypeStruct(x.shape, F32),
        scratch_shapes=[
            pltpu.VMEM((CONV_HALO + tm, d), F32),
            pltpu.VMEM((POOL_HALO + tm, d_pool), F32),
        ],
        compiler_params=pltpu.CompilerParams(
            dimension_semantics=("arbitrary", "arbitrary"), vmem_limit_bytes=VMEM_LIMIT),
    )(x, *prev_args, mod, g_pre.reshape(1, d), g_post.reshape(1, d), w_in.astype(BF16), conv_w,
      w_conv_out.astype(BF16), w_pool_group.astype(BF16), pool_scale.reshape(1, d_pool),
      w_pool_proj.astype(BF16), w_o.astype(BF16))


def _wins(other, v, tie_i):
    return jnp.where(other > v, 1, 0) + jnp.where(other == v, tie_i, 0)


def _route_kernel(x_ref, mod_ref, g_ref, wrt_ref, bias_ref, wsg_ref, wsu_ref, wsd_ref,
                  h2g_ref, shared_ref, rowl_ref, wl_ref, nch_ref, q0_ref,
                  lg_ref, pre_ref, rho_ref, wd_ref):
    i = pl.program_id(1)
    lc, d = x_ref.shape
    n_sub = lg_ref.shape[0]
    n_chunk = d // LANES

    @pl.when(i < n_sub)
    def _():
        mod = mod_ref[0]
        sh2, sc2 = mod[3:4], mod[4:5]

        @pl.when(i == 0)
        def _():
            spare = n_sub * lc * n_chunk
            h2g_ref[0, spare:spare + n_chunk, :] = jnp.zeros((n_chunk, LANES), F32)

        x = x_ref[...]
        h = x * _rms_scale(x) * g_ref[...] * (1.0 + sc2) + sh2
        for c in range(n_chunk):
            h2g_ref[0, pl.ds(i * lc * n_chunk + c, lc, stride=n_chunk), :] = (
                h[:, c * LANES:(c + 1) * LANES])
        hb = h.astype(BF16)
        act = (_silu(jnp.dot(hb, wsg_ref[...], preferred_element_type=F32))
               * jnp.dot(hb, wsu_ref[...], preferred_element_type=F32))
        shared_ref[...] = jnp.dot(act.astype(BF16), wsd_ref[...], preferred_element_type=F32)
        lg_ref[i] = lax.dot_general(
            wrt_ref[...], hb, (((1,), (1,)), ((), ())), preferred_element_type=F32)

    @pl.when(i == n_sub)
    def _():
        _route_plan(bias_ref, rowl_ref, wl_ref, nch_ref, q0_ref, lg_ref, pre_ref, rho_ref, wd_ref,
                    lc, n_chunk)


def _route_plan(bias_ref, rowl_ref, wl_ref, nch_ref, q0_ref, lg_ref, pre_ref, rho_ref, wd_ref,
                lc, n_chunk):
    n_sub = lg_ref.shape[0]
    tb = n_sub * lc
    gidx = lax.broadcasted_iota(jnp.int32, (N_GROUPS, lc), 0)
    tie = [None] + [jnp.where(gidx >= r, 1, 0) for r in range(1, N_GROUPS)]
    tri = (lax.broadcasted_iota(jnp.int32, (lc, lc), 0)
           < lax.broadcasted_iota(jnp.int32, (lc, lc), 1)).astype(BF16)
    carry = jnp.zeros((N_EXPERTS, 1), F32)
    neg_inf = jnp.float32(-jnp.inf)
    for ci in range(tb // lc):
        c0 = ci * lc
        s_all = jax.nn.sigmoid(lg_ref[ci])
        aff = [s_all[GROUP_SIZE * jj:GROUP_SIZE * (jj + 1), :] for jj in range(GROUP_SIZE)]
        sel = [aff[jj] + bias_ref[GROUP_SIZE * jj:GROUP_SIZE * (jj + 1), :]
               for jj in range(GROUP_SIZE)]
        m1, m2 = sel[0], jnp.full_like(sel[0], neg_inf)
        for jj in range(1, GROUP_SIZE):
            m2 = jnp.maximum(m2, jnp.minimum(m1, sel[jj]))
            m1 = jnp.maximum(m1, sel[jj])
        gs = m1 + m2
        beaten = jnp.zeros((N_GROUPS, lc), jnp.int32)
        for r in range(1, N_GROUPS):
            other = pltpu.roll(gs, r, axis=0)
            beaten = beaten + _wins(other, gs, tie[r])
        gmask = beaten < TOPK_GROUPS
        masked = [jnp.where(gmask, sel[jj], neg_inf) for jj in range(GROUP_SIZE)]
        rolled = [[masked[jj]] + [pltpu.roll(masked[jj], r, axis=0) for r in range(1, N_GROUPS)]
                  for jj in range(GROUP_SIZE)]
        rho = []
        for jj in range(GROUP_SIZE):
            v = masked[jj]
            cnt = jnp.zeros((N_GROUPS, lc), jnp.int32)
            for j2 in range(GROUP_SIZE):
                for r in range(N_GROUPS):
                    if r == 0 and j2 == jj:
                        continue
                    other = rolled[j2][r]
                    if r == 0:
                        wins = (other >= v) if j2 < jj else (other > v)
                        cnt = cnt + jnp.where(wins, 1, 0)
                    else:
                        cnt = cnt + _wins(other, v, tie[r])
            rho.append(cnt)
        chosen = [rho[jj] < TOP_K for jj in range(GROUP_SIZE)]
        ssum = jnp.zeros((N_GROUPS, lc), F32)
        for jj in range(GROUP_SIZE):
            ssum = ssum + jnp.where(chosen[jj], aff[jj], 0.0)
        ssum = jnp.sum(ssum, axis=0, keepdims=True)
        wdense = [jnp.where(chosen[jj], aff[jj] / ssum * ROUTED_SCALE, 0.0)
                  for jj in range(GROUP_SIZE)]
        chosen_f = jnp.concatenate([c.astype(F32) for c in chosen], axis=0)
        prefix = jnp.dot(chosen_f.astype(BF16), tri, preferred_element_type=F32) + carry
        carry = carry + jnp.sum(chosen_f, axis=1, keepdims=True)
        pre_ref[:, c0:c0 + lc] = prefix
        rho_ref[:, c0:c0 + lc] = jnp.concatenate(rho, axis=0)
        wd_ref[:, c0:c0 + lc] = jnp.concatenate(wdense, axis=0)

    m = MOE_CHUNK
    nch_b = jnp.broadcast_to(jnp.floor((carry + (m - 1)) * (1.0 / m)), (N_EXPERTS, LANES))
    lower = (lax.broadcasted_iota(jnp.int32, (N_EXPERTS, N_EXPERTS), 1)
             < lax.broadcasted_iota(jnp.int32, (N_EXPERTS, N_EXPERTS), 0)).astype(F32)
    q0_b = jnp.dot(lower, nch_b, preferred_element_type=F32, precision=lax.Precision.HIGHEST)
    nch_ref[0] = nch_b.astype(jnp.int32)
    q0_ref[0] = q0_b.astype(jnp.int32)
    offs_col = q0_b[:, 0:1] * m

    nq = rowl_ref.shape[1]
    iota_q = lax.broadcasted_iota(jnp.int32, (nq, lc), 0).astype(F32)
    iota_r = lax.broadcasted_iota(jnp.int32, (m, lc), 0).astype(F32)
    lists = jnp.zeros((nq, 5 * m), F32)
    for ci in range(tb // lc):
        c0 = ci * lc
        dest_dense = pre_ref[:, c0:c0 + lc] + offs_col
        rho_c = rho_ref[:, c0:c0 + lc]
        w_c = wd_ref[:, c0:c0 + lc]
        tokv = (c0 + 1 + lax.broadcasted_iota(jnp.int32, (1, lc), 1)).astype(F32)
        tok_hi = jnp.floor(tokv * (1.0 / 64))
        tok_lo = tokv - 64.0 * tok_hi
        for k in range(TOP_K):
            hit = rho_c == k
            dk = jnp.sum(jnp.where(hit, dest_dense, 0.0), axis=0, keepdims=True)
            wk = jnp.sum(jnp.where(hit, w_c, 0.0), axis=0, keepdims=True)
            qk = jnp.floor(dk * (1.0 / m))
            rk = dk - m * qk
            w_hi = wk.astype(BF16).astype(F32)
            w_mid = (wk - w_hi).astype(BF16).astype(F32)
            w_lo = wk - w_hi - w_mid
            onehot_q = jnp.where(iota_q == qk + LIST_LEAD, 1.0, 0.0).astype(BF16)
            rmask = iota_r == rk
            vals = jnp.concatenate(
                [jnp.where(rmask, piece, 0.0) for piece in (tok_hi, tok_lo, w_hi, w_mid, w_lo)],
                axis=0).astype(BF16)
            lists = lists + lax.dot_general(onehot_q, vals, (((1,), (1,)), ((), ())),
                                            preferred_element_type=F32)
    tok = lists[:, 0:m] * 64.0 + lists[:, m:2 * m]
    tile = jnp.where(tok == 0.0, float(tb), tok - 1.0)
    rowl_ref[0] = (tile * n_chunk).astype(jnp.int32)
    wl_ref[0] = (lists[:, 2 * m:3 * m] + lists[:, 3 * m:4 * m]) + lists[:, 4 * m:5 * m]


def _route(x1, mod, g_pre, w_router, router_bias, w_sh_gate, w_sh_up, w_sh_down, seq, tb):
    t, d = x1.shape
    nb = t // tb
    n_chunk = d // LANES
    d_sh = w_sh_gate.shape[1]
    perm = jnp.arange(N_EXPERTS).reshape(N_GROUPS, GROUP_SIZE).T.reshape(-1)
    wrt = w_router.T[perm].astype(BF16)
    bias = router_bias[perm].reshape(N_EXPERTS, 1)
    nq = _num_list_rows(tb)
    lc = min(ROUTE_LANES, tb)
    n_sub = tb // lc
    const2 = lambda bi, i: (0, 0)
    per_block = lambda bi, i: (bi, 0, 0)
    sub_tile = lambda bi, i: (bi * n_sub + jnp.minimum(i, n_sub - 1), 0)
    outs = pl.pallas_call(
        _route_kernel,
        grid=(nb, n_sub + 1),
        in_specs=[
            pl.BlockSpec((lc, d), sub_tile),
            pl.BlockSpec((1, 6, d), lambda bi, i: (bi * tb // seq, 0, 0)),
            pl.BlockSpec((1, d), const2),
            pl.BlockSpec((N_EXPERTS, d), const2),
            pl.BlockSpec((N_EXPERTS, 1), const2),
            pl.BlockSpec((d, d_sh), const2),
            pl.BlockSpec((d, d_sh), const2),
            pl.BlockSpec((d_sh, d), const2),
        ],
        out_specs=[
            pl.BlockSpec((1, (tb + 1) * n_chunk, LANES), per_block),
            pl.BlockSpec((lc, d), sub_tile),
            pl.BlockSpec((1, nq, MOE_CHUNK), per_block),
            pl.BlockSpec((1, nq, MOE_CHUNK), per_block),
            pl.BlockSpec((1, N_EXPERTS, LANES), per_block),
            pl.BlockSpec((1, N_EXPERTS, LANES), per_block),
        ],
        out_shape=[
            jax.ShapeDtypeStruct((nb, (tb + 1) * n_chunk, LANES), F32),
            jax.ShapeDtypeStruct((t, d), F32),
            jax.ShapeDtypeStruct((nb, nq, MOE_CHUNK), jnp.int32),
            jax.ShapeDtypeStruct((nb, nq, MOE_CHUNK), F32),
            jax.ShapeDtypeStruct((nb, N_EXPERTS, LANES), jnp.int32),
            jax.ShapeDtypeStruct((nb, N_EXPERTS, LANES), jnp.int32),
        ],
        scratch_shapes=[
            pltpu.VMEM((n_sub, N_EXPERTS, lc), F32),
            pltpu.VMEM((N_EXPERTS, tb), F32),
            pltpu.VMEM((N_EXPERTS, tb), jnp.int32),
            pltpu.VMEM((N_EXPERTS, tb), F32),
        ],
        compiler_params=pltpu.CompilerParams(
            dimension_semantics=("arbitrary", "arbitrary"), vmem_limit_bytes=VMEM_LIMIT),
    )(x1, mod, g_pre.reshape(1, d), wrt, bias, w_sh_gate.astype(BF16), w_sh_up.astype(BF16),
      w_sh_down.astype(BF16))
    return outs


def _num_list_rows(tb):
    return TOP_K * tb // MOE_CHUNK + N_EXPERTS + SUBLANES


def _moe_kernel(nch_sm, q0_sm, prev_sm, h2g_hbm, rowl_hbm, wl_ref, wg_ref, wu_ref, wd_ref,
                wdp_ref, o_hbm, h2g_ref, acc_ref, xt0_ref, xt1_ref, yt0_ref, yt1_ref, act0_ref,
                act1_ref, rowl_sm, sems):
    bi = pl.program_id(0)
    r = pl.program_id(1)
    n_chunk = xt0_ref.shape[0] // (MOE_CHUNK + SUBLANES)
    m = MOE_CHUNK
    stride = m + SUBLANES
    eye = (lax.broadcasted_iota(jnp.int32, (m, m), 0)
           == lax.broadcasted_iota(jnp.int32, (m, m), 1))
    xts, yts, acts = (xt0_ref, xt1_ref), (yt0_ref, yt1_ref), (act0_ref, act1_ref)

    def gather(lrow, xt_ref):
        for mi in range(m):
            row = pl.multiple_of(rowl_sm[lrow, mi], n_chunk)
            xt_ref[pl.ds(mi, n_chunk, stride=stride), :] = h2g_ref[pl.ds(row, n_chunk), :]

    def scatter(lrow, yt_ref):
        for g0 in range(0, m, SUBLANES):
            rows, vals = [], []
            for mi in range(g0, g0 + SUBLANES):
                row = pl.multiple_of(rowl_sm[lrow, mi], n_chunk)
                rows.append(row)
                vals.append(acc_ref[pl.ds(row, n_chunk), :]
                            + yt_ref[pl.ds(mi, n_chunk, stride=stride), :])
            for row, val in zip(rows, vals):
                acc_ref[pl.ds(row, n_chunk), :] = val

    def gate_up(xt_ref, act_ref):
        xs = jnp.concatenate([xt_ref[c * stride:c * stride + m, :] for c in range(n_chunk)],
                             axis=1).astype(BF16)
        act = (_silu(jnp.dot(xs, wg_ref[0, 0], preferred_element_type=F32))
               * jnp.dot(xs, wu_ref[0, 0], preferred_element_type=F32))
        act_ref[...] = act.astype(BF16)

    def down(lrow, act_ref, w_ref, yt_ref):
        y = jnp.dot(act_ref[...], w_ref[0, 0], preferred_element_type=F32)
        w_row = wl_ref[0, pl.ds(lrow, 1), :]
        w_col = jnp.sum(jnp.where(eye, w_row, 0.0), axis=1, keepdims=True)
        y = y * w_col
        for c in range(n_chunk):
            yt_ref[c * stride:c * stride + m, :] = y[:, c * LANES:(c + 1) * LANES]

    def step(q, par, w_down_ref):
        cur, oth = par, 1 - par
        gather(q + 1 + LIST_LEAD, xts[oth])
        gate_up(xts[cur], acts[cur])
        down(q - 1 + LIST_LEAD, acts[oth], w_down_ref, yts[oth])
        scatter(q - 2 + LIST_LEAD, yts[cur])

    def by_parity(q, fn):
        for par in (0, 1):
            @pl.when((q & 1) == par)
            def _():
                fn(par)

    def block_loads():
        return (pltpu.make_async_copy(rowl_hbm.at[bi], rowl_sm, sems.at[0]),
                pltpu.make_async_copy(h2g_hbm.at[bi], h2g_ref, sems.at[1]))

    def block_store():
        return pltpu.make_async_copy(acc_ref, o_hbm.at[bi], sems.at[2])

    @pl.when(r == 0)
    def _():
        for cp in block_loads():
            cp.start()
        acc_ref[...] = jnp.zeros(acc_ref.shape, F32)
        act1_ref[...] = jnp.zeros(act1_ref.shape, BF16)
        yt0_ref[...] = jnp.zeros(yt0_ref.shape, F32)
        for cp in block_loads():
            cp.wait()
        gather(LIST_LEAD, xt0_ref)

    n_chunks = nch_sm[bi * N_EXPERTS + r]
    q_first = q0_sm[bi * N_EXPERTS + r]

    @pl.when(n_chunks > 0)
    def _():
        by_parity(q_first, lambda par: step(q_first, par, wdp_ref))

    def chunk(ci, carry):
        q = q_first + ci
        by_parity(q, lambda par: step(q, par, wd_ref))
        return carry

    lax.fori_loop(1, n_chunks, chunk, 0)

    @pl.when(r == N_EXPERTS - 1)
    def _():
        q_last = q_first + n_chunks - 1

        def drain(par, w_down_ref):
            scatter(q_last - 1 + LIST_LEAD, yts[1 - par])
            down(q_last + LIST_LEAD, acts[par], w_down_ref, yts[par])
            scatter(q_last + LIST_LEAD, yts[par])

        @pl.when(n_chunks > 0)
        def _():
            by_parity(q_last, lambda par: drain(par, wd_ref))

        @pl.when(n_chunks == 0)
        def _():
            by_parity(q_last, lambda par: drain(par, wdp_ref))

        block_store().start()
        block_store().wait()


def _moe(h2g, rowl, wl, nch, q0, w_gate, w_up, w_down, layer, t, d, tb):
    nb = t // tb
    n_chunk = d // LANES
    d_e = w_gate.shape[3]
    m = MOE_CHUNK
    stride = m + SUBLANES
    nq = _num_list_rows(tb)

    def expert_of(r):
        return (r % N_GROUPS) * GROUP_SIZE + r // N_GROUPS

    rows = jnp.where(nch > 0, jnp.arange(N_EXPERTS, dtype=jnp.int32)[None, :], -1)
    last = lax.cummax(rows, axis=1)
    prev = jnp.maximum(jnp.concatenate([jnp.full((nb, 1), -1, jnp.int32), last[:, :-1]], axis=1), 0)

    cur_w = lambda bi, r, c, o, p: (layer, expert_of(r), 0, 0)
    prev_w = lambda bi, r, c, o, p: (layer, expert_of(p[bi * N_EXPERTS + r]), 0, 0)
    per_block = lambda bi, r, c, o, p: (bi, 0, 0)
    staging = pltpu.VMEM((n_chunk * stride, LANES), F32)
    grid_spec = pltpu.PrefetchScalarGridSpec(
        num_scalar_prefetch=3,
        grid=(nb, N_EXPERTS),
        in_specs=[
            pl.BlockSpec(memory_space=pl.ANY),
            pl.BlockSpec(memory_space=pl.ANY),
            pl.BlockSpec((1, nq, m), per_block),
            pl.BlockSpec((1, 1, d, d_e), cur_w),
            pl.BlockSpec((1, 1, d, d_e), cur_w),
            pl.BlockSpec((1, 1, d_e, d), cur_w),
            pl.BlockSpec((1, 1, d_e, d), prev_w),
        ],
        out_specs=pl.BlockSpec(memory_space=pl.ANY),
        scratch_shapes=[
            pltpu.VMEM(((tb + 1) * n_chunk, LANES), F32),
            pltpu.VMEM(((tb + 1) * n_chunk, LANES), F32),
            staging, staging, staging, staging,
            pltpu.VMEM((m, d_e), BF16),
            pltpu.VMEM((m, d_e), BF16),
            pltpu.SMEM((nq, m), jnp.int32),
            pltpu.SemaphoreType.DMA((3,)),
        ],
    )
    return pl.pallas_call(
        _moe_kernel,
        grid_spec=grid_spec,
        out_shape=jax.ShapeDtypeStruct((nb, (tb + 1) * n_chunk, LANES), F32),
        compiler_params=pltpu.CompilerParams(
            dimension_semantics=("arbitrary", "arbitrary"), vmem_limit_bytes=VMEM_LIMIT),
    )(nch.reshape(-1), q0.reshape(-1), prev.reshape(-1), h2g, rowl, wl, w_gate, w_up, w_down,
      w_down)


def _epilogue_kernel(x_ref, routed_ref, shared_ref, mod_ref, g_ref, o_ref):
    o_ref[...] = _ffn_residual(x_ref[...], routed_ref, shared_ref[...], mod_ref[0][5:6], g_ref[...])


def _epilogue(x1, routed, shared, mod, g_post, seq, tb):
    t, d = x1.shape
    n_chunk = d // LANES
    te = min(EPI_TILE, tb)
    n_sub = tb // te
    row = pl.BlockSpec((te, d), lambda bi, i: (bi * n_sub + i, 0))
    return pl.pallas_call(
        _epilogue_kernel,
        grid=(t // tb, n_sub),
        in_specs=[row,
                  pl.BlockSpec((1, te * n_chunk, LANES), lambda bi, i: (bi, i, 0)),
                  row,
                  pl.BlockSpec((1, 6, d), lambda bi, i: (bi * tb // seq, 0, 0)),
                  pl.BlockSpec((1, d), lambda bi, i: (0, 0))],
        out_specs=row,
        out_shape=jax.ShapeDtypeStruct((t, d), F32),
    )(x1, routed, shared, mod, g_post.reshape(1, d))


def kernel(x, c, w_ada, b_ada, g_pre_mix, g_post_mix, g_pre_ffn, g_post_ffn, w_in, conv_w,
           w_conv_out, w_pool_group, pool_scale, w_pool_proj, w_o, w_router, router_bias,
           w_exp_gate, w_exp_up, w_exp_down, w_sh_gate, w_sh_up, w_sh_down):
    b, s, d = x.shape
    depth = w_ada.shape[0]
    t = b * s
    tb = min(MOE_BLOCK, s)
    mods = _ada_mod(c, w_ada, b_ada).reshape(depth, b, 6, d)
    w_exp_gate, w_exp_up, w_exp_down = (w.astype(BF16) for w in (w_exp_gate, w_exp_up, w_exp_down))
    pending_ffn = None
    for l in range(depth):
        mod = mods[l]
        x = _token_mixer(x, pending_ffn, mod, g_pre_mix[l], g_post_mix[l], w_in[l], conv_w[l],
                         w_conv_out[l], w_pool_group[l], pool_scale[l], w_pool_proj[l], w_o[l])
        x1 = x.reshape(t, d)
        h2g, shared, rowl, wl, nch, q0 = _route(
            x1, mod, g_pre_ffn[l], w_router[l], router_bias[l], w_sh_gate[l], w_sh_up[l],
            w_sh_down[l], s, tb)
        routed = _moe(h2g, rowl, wl, nch[:, :, 0], q0[:, :, 0],
                      w_exp_gate, w_exp_up, w_exp_down, l, t, d, tb)
        pending_ffn = (routed, shared, mod, g_post_ffn[l], tb)
    routed, shared, mod, g_post, tb = pending_ffn
    return _epilogue(x.reshape(t, d), routed, shared, mod, g_post, s, tb).reshape(b, s, d)
```

```python
import functools

import jax
import jax.numpy as jnp
from jax import lax
from jax.experimental import pallas as pl
from jax.experimental.pallas import tpu as pltpu

F32 = jnp.float32
BF16 = jnp.bfloat16

EPS = 1e-6
POOL_WINDOWS = (2, 4, 8, 16)
POOL_GROUP_DIM = 128
N_EXPERTS = 64
N_GROUPS = 8
GROUP_SIZE = 8
TOPK_GROUPS = 4
TOP_K = 8
ROUTED_SCALE = 2.5

LANES = 128
SUBLANES = 8
CONV_HALO = 8
POOL_HALO = 16
VMEM_LIMIT = 56 * 1024 * 1024

MIX_TILE = 512
MIX_CHAINS = 2
ROUTE_LANES = 512
MOE_BLOCK = 4096
MOE_CHUNK = 128
LIST_LEAD = 2
EPI_TILE = 512


def _silu(v):
    return v * jax.nn.sigmoid(v)


def _rms_scale(v):
    return lax.rsqrt(jnp.mean(v * v, axis=-1, keepdims=True) + EPS)


def _ada_kernel(c_ref, w_ref, b_ref, o_ref):
    cond = _silu(c_ref[...])
    o_ref[0] = jnp.dot(cond, w_ref[0], preferred_element_type=F32,
                       precision=lax.Precision.HIGHEST) + b_ref[0]


def _ada_mod(c, w_ada, b_ada):
    depth, d, d6 = w_ada.shape
    b = c.shape[0]
    n_col = d6 // d
    return pl.pallas_call(
        _ada_kernel,
        grid=(depth, n_col),
        in_specs=[
            pl.BlockSpec((b, d), lambda l, n: (0, 0)),
            pl.BlockSpec((1, d, d), lambda l, n: (l, 0, n)),
            pl.BlockSpec((1, 1, d), lambda l, n: (l, 0, n)),
        ],
        out_specs=pl.BlockSpec((1, b, d), lambda l, n: (l, 0, n)),
        out_shape=jax.ShapeDtypeStruct((depth, b, d6), F32),
    )(c, w_ada, b_ada.reshape(depth, 1, d6))


def _ffn_residual(x, routed_ref, shared, gt2, g_post, row0=0):
    rows, d = x.shape
    n_chunk = d // LANES
    routed = jnp.concatenate(
        [routed_ref[0, pl.ds(row0 * n_chunk + c, rows, stride=n_chunk), :]
         for c in range(n_chunk)], axis=1)
    y = routed + shared
    return x + gt2 * (y * _rms_scale(y) * g_post)


def _mixer_kernel(*refs, after_ffn):
    if after_ffn:
        (x_ref, routed_ref, shared_ref, modp_ref, gpp_ref), refs = refs[:5], refs[5:]
    else:
        x_ref, refs = refs[0], refs[1:]
    (mod_ref, gpre_ref, gpost_ref, win_ref, convw_ref, wco_ref, wpg_ref, pscale_ref, wpp_ref,
     wo_ref, o_ref, uext_ref, pext_ref) = refs
    j = pl.program_id(1)
    tm, d = x_ref.shape[1], x_ref.shape[2]
    d_pool = pext_ref.shape[1]

    @pl.when(j == 0)
    def _():
        uext_ref[0:CONV_HALO, :] = jnp.zeros((CONV_HALO, d), F32)
        pext_ref[0:POOL_HALO, :] = jnp.zeros((POOL_HALO, d_pool), F32)

    mod = mod_ref[0]
    sh1, sc1, gt1 = mod[0:1], mod[1:2], mod[2:3]
    cw = convw_ref[...]
    ts = tm // MIX_CHAINS
    for ch in range(MIX_CHAINS):
        r0 = ch * ts
        x = x_ref[0, r0:r0 + ts, :]
        if after_ffn:
            x = _ffn_residual(x, routed_ref, shared_ref[0, r0:r0 + ts, :], modp_ref[0][5:6],
                              gpp_ref[...], row0=r0)
        h = x * _rms_scale(x) * gpre_ref[...] * (1.0 + sc1) + sh1
        hb = h.astype(BF16)

        def proj(lo, hi):
            return jnp.dot(hb, win_ref[:, lo:hi], preferred_element_type=F32)

        u = proj(d, 2 * d) * proj(2 * d, 3 * d)
        u0 = CONV_HALO + r0
        uext_ref[u0:u0 + ts, :] = u
        conv = (cw[2:3] * u
                + cw[1:2] * uext_ref[u0 - 1:u0 - 1 + ts, :]
                + cw[0:1] * uext_ref[u0 - 2:u0 - 2 + ts, :])
        y_conv = jnp.dot((proj(0, d) * conv).astype(BF16), wco_ref[...],
                         preferred_element_type=F32)

        up = proj(3 * d, 3 * d + d_pool)
        p0 = POOL_HALO + r0
        pext_ref[p0:p0 + ts, :] = up
        pos = j * tm + r0 + lax.broadcasted_iota(jnp.int32, (ts, 1), 0)
        zs = []
        for g, w in enumerate(POOL_WINDOWS):
            c0 = g * POOL_GROUP_DIM
            ug = up[:, c0:c0 + POOL_GROUP_DIM]
            acc = ug
            for k in range(1, w):
                acc = acc + pext_ref[p0 - k:p0 - k + ts, c0:c0 + POOL_GROUP_DIM]
            inv_cnt = 1.0 / jnp.minimum(pos + 1, w).astype(F32)
            diff = acc * inv_cnt - ug
            zs.append(jnp.dot(diff.astype(BF16), wpg_ref[g], preferred_element_type=F32))
        z = jnp.concatenate(zs, axis=1) * pscale_ref[...]
        y_pool = jnp.dot(z.astype(BF16), wpp_ref[...], preferred_element_type=F32)

        a_conv = proj(3 * d + d_pool, 4 * d + d_pool)
        a_pool = proj(4 * d + d_pool, 5 * d + d_pool)
        merged = jax.nn.sigmoid(a_conv) * y_conv + jax.nn.sigmoid(a_pool) * y_pool
        y = jnp.dot(merged.astype(BF16), wo_ref[...], preferred_element_type=F32)
        o_ref[0, r0:r0 + ts, :] = x + gt1 * (y * _rms_scale(y) * gpost_ref[...])

    uext_ref[0:CONV_HALO, :] = uext_ref[tm:tm + CONV_HALO, :]
    pext_ref[0:POOL_HALO, :] = pext_ref[tm:tm + POOL_HALO, :]


def _token_mixer(x, pending_ffn, mod, g_pre, g_post, w_in, conv_w, w_conv_out, w_pool_group,
                 pool_scale, w_pool_proj, w_o):
    b, s, d = x.shape
    d_in = w_in.shape[1]
    d_pool = w_pool_proj.shape[0]
    n_chunk = d // LANES
    tm = min(MIX_TILE, s)
    const2 = lambda bi, j: (0, 0)
    const3 = lambda bi, j: (0, 0, 0)
    rows = pl.BlockSpec((1, tm, d), lambda bi, j: (bi, j, 0))
    per_batch = pl.BlockSpec((1, 6, d), lambda bi, j: (bi, 0, 0))
    prev_specs, prev_args = [], []
    if pending_ffn is not None:
        routed, shared, mod_p, g_post_p, tb = pending_ffn
        tiles = tb // tm
        prev_specs = [
            pl.BlockSpec((1, tm * n_chunk, LANES),
                         lambda bi, j: ((bi * (s // tm) + j) // tiles, (bi * (s // tm) + j) % tiles, 0)),
            rows, per_batch, pl.BlockSpec((1, d), const2)]
        prev_args = [routed, shared.reshape(b, s, d), mod_p, g_post_p.reshape(1, d)]
    return pl.pallas_call(
        functools.partial(_mixer_kernel, after_ffn=pending_ffn is not None),
        grid=(b, s // tm),
        in_specs=[
            rows,
            *prev_specs,
            per_batch,
            pl.BlockSpec((1, d), const2),
            pl.BlockSpec((1, d), const2),
            pl.BlockSpec((d, d_in), const2),
            pl.BlockSpec((3, d), const2),
            pl.BlockSpec((d, d), const2),
            pl.BlockSpec(w_pool_group.shape, const3),
            pl.BlockSpec((1, d_pool), const2),
            pl.BlockSpec((d_pool, d), const2),
            pl.BlockSpec((d, d), const2),
        ],
        out_specs=pl.BlockSpec((1, tm, d), lambda bi, j: (bi, j, 0)),
        out_shape=jax.ShapeDtypeStruct(x.shape, F32),
        scratch_shapes=[
            pltpu.VMEM((CONV_HALO + tm, d), F32),
            pltpu.VMEM((POOL_HALO + tm, d_pool), F32),
        ],
        compiler_params=pltpu.CompilerParams(
            dimension_semantics=("arbitrary", "arbitrary"), vmem_limit_bytes=VMEM_LIMIT),
    )(x, *prev_args, mod, g_pre.reshape(1, d), g_post.reshape(1, d), w_in.astype(BF16), conv_w,
      w_conv_out.astype(BF16), w_pool_group.astype(BF16), pool_scale.reshape(1, d_pool),
      w_pool_proj.astype(BF16), w_o.astype(BF16))


def _wins(other, v, tie_i):
    return jnp.where(other > v, 1, 0) + jnp.where(other == v, tie_i, 0)


def _route_kernel(x_ref, mod_ref, g_ref, wrt_ref, bias_ref, wsg_ref, wsu_ref, wsd_ref,
                  h2g_ref, shared_ref, rowl_ref, wl_ref, nch_ref, q0_ref,
                  lg_ref, pre_ref, rho_ref, wd_ref):
    i = pl.program_id(1)
    lc, d = x_ref.shape
    n_sub = lg_ref.shape[0]
    n_chunk = d // LANES

    @pl.when(i < n_sub)
    def _():
        mod = mod_ref[0]
        sh2, sc2 = mod[3:4], mod[4:5]

        @pl.when(i == 0)
        def _():
            spare = n_sub * lc * n_chunk
            h2g_ref[0, spare:spare + n_chunk, :] = jnp.zeros((n_chunk, LANES), F32)

        x = x_ref[...]
        h = x * _rms_scale(x) * g_ref[...] * (1.0 + sc2) + sh2
        for c in range(n_chunk):
            h2g_ref[0, pl.ds(i * lc * n_chunk + c, lc, stride=n_chunk), :] = (
                h[:, c * LANES:(c + 1) * LANES])
        hb = h.astype(BF16)
        act = (_silu(jnp.dot(hb, wsg_ref[...], preferred_element_type=F32))
               * jnp.dot(hb, wsu_ref[...], preferred_element_type=F32))
        shared_ref[...] = jnp.dot(act.astype(BF16), wsd_ref[...], preferred_element_type=F32)
        lg_ref[i] = lax.dot_general(
            wrt_ref[...], hb, (((1,), (1,)), ((), ())), preferred_element_type=F32)

    @pl.when(i == n_sub)
    def _():
        _route_plan(bias_ref, rowl_ref, wl_ref, nch_ref, q0_ref, lg_ref, pre_ref, rho_ref, wd_ref,
                    lc, n_chunk)


def _route_plan(bias_ref, rowl_ref, wl_ref, nch_ref, q0_ref, lg_ref, pre_ref, rho_ref, wd_ref,
                lc, n_chunk):
    n_sub = lg_ref.shape[0]
    tb = n_sub * lc
    gidx = lax.broadcasted_iota(jnp.int32, (N_GROUPS, lc), 0)
    tie = [None] + [jnp.where(gidx >= r, 1, 0) for r in range(1, N_GROUPS)]
    tri = (lax.broadcasted_iota(jnp.int32, (lc, lc), 0)
           < lax.broadcasted_iota(jnp.int32, (lc, lc), 1)).astype(BF16)
    carry = jnp.zeros((N_EXPERTS, 1), F32)
    neg_inf = jnp.float32(-jnp.inf)
    for ci in range(tb // lc):
        c0 = ci * lc
        s_all = jax.nn.sigmoid(lg_ref[ci])
        aff = [s_all[GROUP_SIZE * jj:GROUP_SIZE * (jj + 1), :] for jj in range(GROUP_SIZE)]
        sel = [aff[jj] + bias_ref[GROUP_SIZE * jj:GROUP_SIZE * (jj + 1), :]
               for jj in range(GROUP_SIZE)]
        m1, m2 = sel[0], jnp.full_like(sel[0], neg_inf)
        for jj in range(1, GROUP_SIZE):
            m2 = jnp.maximum(m2, jnp.minimum(m1, sel[jj]))
            m1 = jnp.maximum(m1, sel[jj])
        gs = m1 + m2
        beaten = jnp.zeros((N_GROUPS, lc), jnp.int32)
        for r in range(1, N_GROUPS):
            other = pltpu.roll(gs, r, axis=0)
            beaten = beaten + _wins(other, gs, tie[r])
        gmask = beaten < TOPK_GROUPS
        masked = [jnp.where(gmask, sel[jj], neg_inf) for jj in range(GROUP_SIZE)]
        rolled = [[masked[jj]] + [pltpu.roll(masked[jj], r, axis=0) for r in range(1, N_GROUPS)]
                  for jj in range(GROUP_SIZE)]
        rho = []
        for jj in range(GROUP_SIZE):
            v = masked[jj]
            cnt = jnp.zeros((N_GROUPS, lc), jnp.int32)
            for j2 in range(GROUP_SIZE):
                for r in range(N_GROUPS):
                    if r == 0 and j2 == jj:
                        continue
                    other = rolled[j2][r]
                    if r == 0:
                        wins = (other >= v) if j2 < jj else (other > v)
                        cnt = cnt + jnp.where(wins, 1, 0)
                    else:
                        cnt = cnt + _wins(other, v, tie[r])
            rho.append(cnt)
        chosen = [rho[jj] < TOP_K for jj in range(GROUP_SIZE)]
        ssum = jnp.zeros((N_GROUPS, lc), F32)
        for jj in range(GROUP_SIZE):
            ssum = ssum + jnp.where(chosen[jj], aff[jj], 0.0)
        ssum = jnp.sum(ssum, axis=0, keepdims=True)
        wdense = [jnp.where(chosen[jj], aff[jj] / ssum * ROUTED_SCALE, 0.0)
                  for jj in range(GROUP_SIZE)]
        chosen_f = jnp.concatenate([c.astype(F32) for c in chosen], axis=0)
        prefix = jnp.dot(chosen_f.astype(BF16), tri, preferred_element_type=F32) + carry
        carry = carry + jnp.sum(chosen_f, axis=1, keepdims=True)
        pre_ref[:, c0:c0 + lc] = prefix
        rho_ref[:, c0:c0 + lc] = jnp.concatenate(rho, axis=0)
        wd_ref[:, c0:c0 + lc] = jnp.concatenate(wdense, axis=0)

    m = MOE_CHUNK
    nch_b = jnp.broadcast_to(jnp.floor((carry + (m - 1)) * (1.0 / m)), (N_EXPERTS, LANES))
    lower = (lax.broadcasted_iota(jnp.int32, (N_EXPERTS, N_EXPERTS), 1)
             < lax.broadcasted_iota(jnp.int32, (N_EXPERTS, N_EXPERTS), 0)).astype(F32)
    q0_b = jnp.dot(lower, nch_b, preferred_element_type=F32, precision=lax.Precision.HIGHEST)
    nch_ref[0] = nch_b.astype(jnp.int32)
    q0_ref[0] = q0_b.astype(jnp.int32)
    offs_col = q0_b[:, 0:1] * m

    nq = rowl_ref.shape[1]
    iota_q = lax.broadcasted_iota(jnp.int32, (nq, lc), 0).astype(F32)
    iota_r = lax.broadcasted_iota(jnp.int32, (m, lc), 0).astype(F32)
    lists = jnp.zeros((nq, 5 * m), F32)
    for ci in range(tb // lc):
        c0 = ci * lc
        dest_dense = pre_ref[:, c0:c0 + lc] + offs_col
        rho_c = rho_ref[:, c0:c0 + lc]
        w_c = wd_ref[:, c0:c0 + lc]
        tokv = (c0 + 1 + lax.broadcasted_iota(jnp.int32, (1, lc), 1)).astype(F32)
        tok_hi = jnp.floor(tokv * (1.0 / 64))
        tok_lo = tokv - 64.0 * tok_hi
        for k in range(TOP_K):
            hit = rho_c == k
            dk = jnp.sum(jnp.where(hit, dest_dense, 0.0), axis=0, keepdims=True)
            wk = jnp.sum(jnp.where(hit, w_c, 0.0), axis=0, keepdims=True)
            qk = jnp.floor(dk * (1.0 / m))
            rk = dk - m * qk
            w_hi = wk.astype(BF16).astype(F32)
            w_mid = (wk - w_hi).astype(BF16).astype(F32)
            w_lo = wk - w_hi - w_mid
            onehot_q = jnp.where(iota_q == qk + LIST_LEAD, 1.0, 0.0).astype(BF16)
            rmask = iota_r == rk
            vals = jnp.concatenate(
                [jnp.where(rmask, piece, 0.0) for piece in (tok_hi, tok_lo, w_hi, w_mid, w_lo)],
                axis=0).astype(BF16)
            lists = lists + lax.dot_general(onehot_q, vals, (((1,), (1,)), ((), ())),
                                            preferred_element_type=F32)
    tok = lists[:, 0:m] * 64.0 + lists[:, m:2 * m]
    tile = jnp.where(tok == 0.0, float(tb), tok - 1.0)
    rowl_ref[0] = (tile * n_chunk).astype(jnp.int32)
    wl_ref[0] = (lists[:, 2 * m:3 * m] + lists[:, 3 * m:4 * m]) + lists[:, 4 * m:5 * m]


def _route(x1, mod, g_pre, w_router, router_bias, w_sh_gate, w_sh_up, w_sh_down, seq, tb):
    t, d = x1.shape
    nb = t // tb
    n_chunk = d // LANES
    d_sh = w_sh_gate.shape[1]
    perm = jnp.arange(N_EXPERTS).reshape(N_GROUPS, GROUP_SIZE).T.reshape(-1)
    wrt = w_router.T[perm].astype(BF16)
    bias = router_bias[perm].reshape(N_EXPERTS, 1)
    nq = _num_list_rows(tb)
    lc = min(ROUTE_LANES, tb)
    n_sub = tb // lc
    const2 = lambda bi, i: (0, 0)
    per_block = lambda bi, i: (bi, 0, 0)
    sub_tile = lambda bi, i: (bi * n_sub + jnp.minimum(i, n_sub - 1), 0)
    outs = pl.pallas_call(
        _route_kernel,
        grid=(nb, n_sub + 1),
        in_specs=[
            pl.BlockSpec((lc, d), sub_tile),
            pl.BlockSpec((1, 6, d), lambda bi, i: (bi * tb // seq, 0, 0)),
            pl.BlockSpec((1, d), const2),
            pl.BlockSpec((N_EXPERTS, d), const2),
            pl.BlockSpec((N_EXPERTS, 1), const2),
            pl.BlockSpec((d, d_sh), const2),
            pl.BlockSpec((d, d_sh), const2),
            pl.BlockSpec((d_sh, d), const2),
        ],
        out_specs=[
            pl.BlockSpec((1, (tb + 1) * n_chunk, LANES), per_block),
            pl.BlockSpec((lc, d), sub_tile),
            pl.BlockSpec((1, nq, MOE_CHUNK), per_block),
            pl.BlockSpec((1, nq, MOE_CHUNK), per_block),
            pl.BlockSpec((1, N_EXPERTS, LANES), per_block),
            pl.BlockSpec((1, N_EXPERTS, LANES), per_block),
        ],
        out_shape=[
            jax.ShapeDtypeStruct((nb, (tb + 1) * n_chunk, LANES), F32),
            jax.ShapeDtypeStruct((t, d), F32),
            jax.ShapeDtypeStruct((nb, nq, MOE_CHUNK), jnp.int32),
            jax.ShapeDtypeStruct((nb, nq, MOE_CHUNK), F32),
            jax.ShapeDtypeStruct((nb, N_EXPERTS, LANES), jnp.int32),
            jax.ShapeDtypeStruct((nb, N_EXPERTS, LANES), jnp.int32),
        ],
        scratch_shapes=[
            pltpu.VMEM((n_sub, N_EXPERTS, lc), F32),
            pltpu.VMEM((N_EXPERTS, tb), F32),
            pltpu.VMEM((N_EXPERTS, tb), jnp.int32),
            pltpu.VMEM((N_EXPERTS, tb), F32),
        ],
        compiler_params=pltpu.CompilerParams(
            dimension_semantics=("arbitrary", "arbitrary"), vmem_limit_bytes=VMEM_LIMIT),
    )(x1, mod, g_pre.reshape(1, d), wrt, bias, w_sh_gate.astype(BF16), w_sh_up.astype(BF16),
      w_sh_down.astype(BF16))
    return outs


def _num_list_rows(tb):
    return TOP_K * tb // MOE_CHUNK + N_EXPERTS + SUBLANES


def _moe_kernel(nch_sm, q0_sm, h2g_hbm, rowl_hbm, wl_ref, wg_ref, wu_ref, wd_ref, o_hbm,
                h2g_ref, acc_ref, xt0_ref, xt1_ref, yt0_ref, yt1_ref, act0_ref, act1_ref,
                wgb_ref, wub_ref, wdb_ref, rowl_sm, experts_sm, sems):
    bi = pl.program_id(0)
    r = pl.program_id(1)
    n_chunk = xt0_ref.shape[0] // (MOE_CHUNK + SUBLANES)
    m = MOE_CHUNK
    stride = m + SUBLANES
    eye = (lax.broadcasted_iota(jnp.int32, (m, m), 0)
           == lax.broadcasted_iota(jnp.int32, (m, m), 1))
    xts, yts, acts = (xt0_ref, xt1_ref), (yt0_ref, yt1_ref), (act0_ref, act1_ref)

    def gather(lrow, xt_ref):
        for mi in range(m):
            row = pl.multiple_of(rowl_sm[lrow, mi], n_chunk)
            xt_ref[pl.ds(mi, n_chunk, stride=stride), :] = h2g_ref[pl.ds(row, n_chunk), :]

    def scatter(lrow, yt_ref):
        for g0 in range(0, m, SUBLANES):
            rows, vals = [], []
            for mi in range(g0, g0 + SUBLANES):
                row = pl.multiple_of(rowl_sm[lrow, mi], n_chunk)
                rows.append(row)
                vals.append(acc_ref[pl.ds(row, n_chunk), :]
                            + yt_ref[pl.ds(mi, n_chunk, stride=stride), :])
            for row, val in zip(rows, vals):
                acc_ref[pl.ds(row, n_chunk), :] = val

    def gate_up(xt_ref, act_ref):
        xs = jnp.concatenate([xt_ref[c * stride:c * stride + m, :] for c in range(n_chunk)],
                             axis=1).astype(BF16)
        act = (_silu(jnp.dot(xs, wgb_ref[...], preferred_element_type=F32))
               * jnp.dot(xs, wub_ref[...], preferred_element_type=F32))
        act_ref[...] = act.astype(BF16)

    def down(lrow, act_ref, slot, yt_ref):
        y = jnp.dot(act_ref[...], wdb_ref[slot], preferred_element_type=F32)
        w_row = wl_ref[0, pl.ds(lrow, 1), :]
        w_col = jnp.sum(jnp.where(eye, w_row, 0.0), axis=1, keepdims=True)
        y = y * w_col
        for c in range(n_chunk):
            yt_ref[c * stride:c * stride + m, :] = y[:, c * LANES:(c + 1) * LANES]

    def step(q, par, down_slot):
        cur, oth = par, 1 - par
        gather(q + 1 + LIST_LEAD, xts[oth])
        gate_up(xts[cur], acts[cur])
        down(q - 1 + LIST_LEAD, acts[oth], down_slot, yts[oth])
        scatter(q - 2 + LIST_LEAD, yts[cur])

    def by_parity(q, fn):
        for par in (0, 1):
            @pl.when((q & 1) == par)
            def _():
                fn(par)

    def block_loads():
        return (pltpu.make_async_copy(rowl_hbm.at[bi], rowl_sm, sems.at[0]),
                pltpu.make_async_copy(h2g_hbm.at[bi], h2g_ref, sems.at[1]))

    def block_store():
        return pltpu.make_async_copy(acc_ref, o_hbm.at[bi], sems.at[2])

    @pl.when(r == 0)
    def _():
        for cp in block_loads():
            cp.start()
        acc_ref[...] = jnp.zeros(acc_ref.shape, F32)
        act1_ref[...] = jnp.zeros(act1_ref.shape, BF16)
        yt0_ref[...] = jnp.zeros(yt0_ref.shape, F32)
        wdb_ref[...] = jnp.zeros(wdb_ref.shape, BF16)
        experts_sm[0] = 0
        for cp in block_loads():
            cp.wait()
        gather(LIST_LEAD, xt0_ref)

    n_chunks = nch_sm[bi * N_EXPERTS + r]
    q_first = q0_sm[bi * N_EXPERTS + r]
    slot = experts_sm[0] & 1

    @pl.when(n_chunks > 0)
    def _():
        wgb_ref[...] = wg_ref[0, 0].astype(BF16)
        wub_ref[...] = wu_ref[0, 0].astype(BF16)
        wdb_ref[slot] = wd_ref[0, 0].astype(BF16)
        by_parity(q_first, lambda par: step(q_first, par, 1 - slot))
        experts_sm[0] = experts_sm[0] + 1

    def chunk(ci, carry):
        q = q_first + ci
        by_parity(q, lambda par: step(q, par, slot))
        return carry

    lax.fori_loop(1, n_chunks, chunk, 0)

    @pl.when(r == N_EXPERTS - 1)
    def _():
        q_last = q_first + n_chunks - 1
        last_slot = (experts_sm[0] - 1) & 1

        def drain(par):
            scatter(q_last - 1 + LIST_LEAD, yts[1 - par])
            down(q_last + LIST_LEAD, acts[par], last_slot, yts[par])
            scatter(q_last + LIST_LEAD, yts[par])

        by_parity(q_last, drain)

        block_store().start()
        block_store().wait()


def _moe(h2g, rowl, wl, nch, q0, w_gate, w_up, w_down, layer, t, d, tb):
    nb = t // tb
    n_chunk = d // LANES
    d_e = w_gate.shape[3]
    m = MOE_CHUNK
    stride = m + SUBLANES
    nq = _num_list_rows(tb)

    def expert_of(r):
        return (r % N_GROUPS) * GROUP_SIZE + r // N_GROUPS

    cur_w = lambda bi, r, c, o: (layer, expert_of(r), 0, 0)
    per_block = lambda bi, r, c, o: (bi, 0, 0)
    staging = pltpu.VMEM((n_chunk * stride, LANES), F32)
    grid_spec = pltpu.PrefetchScalarGridSpec(
        num_scalar_prefetch=2,
        grid=(nb, N_EXPERTS),
        in_specs=[
            pl.BlockSpec(memory_space=pl.ANY),
            pl.BlockSpec(memory_space=pl.ANY),
            pl.BlockSpec((1, nq, m), per_block),
            pl.BlockSpec((1, 1, d, d_e), cur_w),
            pl.BlockSpec((1, 1, d, d_e), cur_w),
            pl.BlockSpec((1, 1, d_e, d), cur_w),
        ],
        out_specs=pl.BlockSpec(memory_space=pl.ANY),
        scratch_shapes=[
            pltpu.VMEM(((tb + 1) * n_chunk, LANES), F32),
            pltpu.VMEM(((tb + 1) * n_chunk, LANES), F32),
            staging, staging, staging, staging,
            pltpu.VMEM((m, d_e), BF16),
            pltpu.VMEM((m, d_e), BF16),
            pltpu.VMEM((d, d_e), BF16),
            pltpu.VMEM((d, d_e), BF16),
            pltpu.VMEM((2, d_e, d), BF16),
            pltpu.SMEM((nq, m), jnp.int32),
            pltpu.SMEM((1,), jnp.int32),
            pltpu.SemaphoreType.DMA((3,)),
        ],
    )
    return pl.pallas_call(
        _moe_kernel,
        grid_spec=grid_spec,
        out_shape=jax.ShapeDtypeStruct((nb, (tb + 1) * n_chunk, LANES), F32),
        compiler_params=pltpu.CompilerParams(
            dimension_semantics=("arbitrary", "arbitrary"), vmem_limit_bytes=VMEM_LIMIT),
    )(nch.reshape(-1), q0.reshape(-1), h2g, rowl, wl, w_gate, w_up, w_down)


def _epilogue_kernel(x_ref, routed_ref, shared_ref, mod_ref, g_ref, o_ref):
    o_ref[...] = _ffn_residual(x_ref[...], routed_ref, shared_ref[...], mod_ref[0][5:6], g_ref[...])


def _epilogue(x1, routed, shared, mod, g_post, seq, tb):
    t, d = x1.shape
    n_chunk = d // LANES
    te = min(EPI_TILE, tb)
    n_sub = tb // te
    row = pl.BlockSpec((te, d), lambda bi, i: (bi * n_sub + i, 0))
    return pl.pallas_call(
        _epilogue_kernel,
        grid=(t // tb, n_sub),
        in_specs=[row,
                  pl.BlockSpec((1, te * n_chunk, LANES), lambda bi, i: (bi, i, 0)),
                  row,
                  pl.BlockSpec((1, 6, d), lambda bi, i: (bi * tb // seq, 0, 0)),
                  pl.BlockSpec((1, d), lambda bi, i: (0, 0))],
        out_specs=row,
        out_shape=jax.ShapeDtypeStruct((t, d), F32),
    )(x1, routed, shared, mod, g_post.reshape(1, d))


def kernel(x, c, w_ada, b_ada, g_pre_mix, g_post_mix, g_pre_ffn, g_post_ffn, w_in, conv_w,
           w_conv_out, w_pool_group, pool_scale, w_pool_proj, w_o, w_router, router_bias,
           w_exp_gate, w_exp_up, w_exp_down, w_sh_gate, w_sh_up, w_sh_down):
    b, s, d = x.shape
    depth = w_ada.shape[0]
    t = b * s
    tb = min(MOE_BLOCK, s)
    mods = _ada_mod(c, w_ada, b_ada).reshape(depth, b, 6, d)
    pending_ffn = None
    for l in range(depth):
        mod = mods[l]
        x = _token_mixer(x, pending_ffn, mod, g_pre_mix[l], g_post_mix[l], w_in[l], conv_w[l],
                         w_conv_out[l], w_pool_group[l], pool_scale[l], w_pool_proj[l], w_o[l])
        x1 = x.reshape(t, d)
        h2g, shared, rowl, wl, nch, q0 = _route(
            x1, mod, g_pre_ffn[l], w_router[l], router_bias[l], w_sh_gate[l], w_sh_up[l],
            w_sh_down[l], s, tb)
        routed = _moe(h2g, rowl, wl, nch[:, :, 0], q0[:, :, 0],
                      w_exp_gate, w_exp_up, w_exp_down, l, t, d, tb)
        pending_ffn = (routed, shared, mod, g_post_ffn[l], tb)
    routed, shared, mod, g_post, tb = pending_ffn
    return _epilogue(x.reshape(t, d), routed, shared, mod, g_post, s, tb).reshape(b, s, d)
```

```python
import functools

import jax
import jax.numpy as jnp
from jax import lax
from jax.experimental import pallas as pl
from jax.experimental.pallas import tpu as pltpu

F32 = jnp.float32
BF16 = jnp.bfloat16

EPS = 1e-6
POOL_WINDOWS = (2, 4, 8, 16)
POOL_GROUP_DIM = 128
N_EXPERTS = 64
N_GROUPS = 8
GROUP_SIZE = 8
TOPK_GROUPS = 4
TOP_K = 8
ROUTED_SCALE = 2.5

LANES = 128
SUBLANES = 8
CONV_HALO = 8
POOL_HALO = 16
VMEM_LIMIT = 56 * 1024 * 1024

MIX_TILE = 512
MIX_CHAINS = 2
ROUTE_LANES = 512
MOE_BLOCK = 4096
MOE_CHUNK = 192
LIST_LEAD = 2
EPI_TILE = 512


def _silu(v):
    return v * jax.nn.sigmoid(v)


def _rms_scale(v):
    return lax.rsqrt(jnp.mean(v * v, axis=-1, keepdims=True) + EPS)


def _ada_kernel(c_ref, w_ref, b_ref, o_ref):
    cond = _silu(c_ref[...])
    o_ref[0] = jnp.dot(cond, w_ref[0], preferred_element_type=F32,
                       precision=lax.Precision.HIGHEST) + b_ref[0]


def _ada_mod(c, w_ada, b_ada):
    depth, d, d6 = w_ada.shape
    b = c.shape[0]
    n_col = d6 // d
    return pl.pallas_call(
        _ada_kernel,
        grid=(depth, n_col),
        in_specs=[
            pl.BlockSpec((b, d), lambda l, n: (0, 0)),
            pl.BlockSpec((1, d, d), lambda l, n: (l, 0, n)),
            pl.BlockSpec((1, 1, d), lambda l, n: (l, 0, n)),
        ],
        out_specs=pl.BlockSpec((1, b, d), lambda l, n: (l, 0, n)),
        out_shape=jax.ShapeDtypeStruct((depth, b, d6), F32),
    )(c, w_ada, b_ada.reshape(depth, 1, d6))


def _ffn_residual(x, routed_ref, shared, gt2, g_post, row0=0):
    rows, d = x.shape
    n_chunk = d // LANES
    routed = jnp.concatenate(
        [routed_ref[0, pl.ds(row0 * n_chunk + c, rows, stride=n_chunk), :]
         for c in range(n_chunk)], axis=1)
    y = routed + shared
    return x + gt2 * (y * _rms_scale(y) * g_post)


def _mixer_kernel(*refs, after_ffn):
    if after_ffn:
        (x_ref, routed_ref, shared_ref, modp_ref, gpp_ref), refs = refs[:5], refs[5:]
    else:
        x_ref, refs = refs[0], refs[1:]
    (mod_ref, gpre_ref, gpost_ref, win_ref, convw_ref, wco_ref, wpg_ref, pscale_ref, wpp_ref,
     wo_ref, o_ref, uext_ref, pext_ref) = refs
    j = pl.program_id(1)
    tm, d = x_ref.shape[1], x_ref.shape[2]
    d_pool = pext_ref.shape[1]

    @pl.when(j == 0)
    def _():
        uext_ref[0:CONV_HALO, :] = jnp.zeros((CONV_HALO, d), F32)
        pext_ref[0:POOL_HALO, :] = jnp.zeros((POOL_HALO, d_pool), F32)

    mod = mod_ref[0]
    sh1, sc1, gt1 = mod[0:1], mod[1:2], mod[2:3]
    cw = convw_ref[...]
    ts = tm // MIX_CHAINS
    for ch in range(MIX_CHAINS):
        r0 = ch * ts
        x = x_ref[0, r0:r0 + ts, :]
        if after_ffn:
            x = _ffn_residual(x, routed_ref, shared_ref[0, r0:r0 + ts, :], modp_ref[0][5:6],
                              gpp_ref[...], row0=r0)
        h = x * _rms_scale(x) * gpre_ref[...] * (1.0 + sc1) + sh1
        hb = h.astype(BF16)

        def proj(lo, hi):
            return jnp.dot(hb, win_ref[:, lo:hi], preferred_element_type=F32)

        u = proj(d, 2 * d) * proj(2 * d, 3 * d)
        u0 = CONV_HALO + r0
        uext_ref[u0:u0 + ts, :] = u
        conv = (cw[2:3] * u
                + cw[1:2] * uext_ref[u0 - 1:u0 - 1 + ts, :]
                + cw[0:1] * uext_ref[u0 - 2:u0 - 2 + ts, :])
        y_conv = jnp.dot((proj(0, d) * conv).astype(BF16), wco_ref[...],
                         preferred_element_type=F32)

        up = proj(3 * d, 3 * d + d_pool)
        p0 = POOL_HALO + r0
        pext_ref[p0:p0 + ts, :] = up
        pos = j * tm + r0 + lax.broadcasted_iota(jnp.int32, (ts, 1), 0)
        zs = []
        for g, w in enumerate(POOL_WINDOWS):
            c0 = g * POOL_GROUP_DIM
            ug = up[:, c0:c0 + POOL_GROUP_DIM]
            acc = ug
            for k in range(1, w):
                acc = acc + pext_ref[p0 - k:p0 - k + ts, c0:c0 + POOL_GROUP_DIM]
            inv_cnt = 1.0 / jnp.minimum(pos + 1, w).astype(F32)
            diff = acc * inv_cnt - ug
            zs.append(jnp.dot(diff.astype(BF16), wpg_ref[g], preferred_element_type=F32))
        z = jnp.concatenate(zs, axis=1) * pscale_ref[...]
        y_pool = jnp.dot(z.astype(BF16), wpp_ref[...], preferred_element_type=F32)

        a_conv = proj(3 * d + d_pool, 4 * d + d_pool)
        a_pool = proj(4 * d + d_pool, 5 * d + d_pool)
        merged = jax.nn.sigmoid(a_conv) * y_conv + jax.nn.sigmoid(a_pool) * y_pool
        y = jnp.dot(merged.astype(BF16), wo_ref[...], preferred_element_type=F32)
        o_ref[0, r0:r0 + ts, :] = x + gt1 * (y * _rms_scale(y) * gpost_ref[...])

    uext_ref[0:CONV_HALO, :] = uext_ref[tm:tm + CONV_HALO, :]
    pext_ref[0:POOL_HALO, :] = pext_ref[tm:tm + POOL_HALO, :]


def _token_mixer(x, pending_ffn, mod, g_pre, g_post, w_in, conv_w, w_conv_out, w_pool_group,
                 pool_scale, w_pool_proj, w_o):
    b, s, d = x.shape
    d_in = w_in.shape[1]
    d_pool = w_pool_proj.shape[0]
    n_chunk = d // LANES
    tm = min(MIX_TILE, s)
    const2 = lambda bi, j: (0, 0)
    const3 = lambda bi, j: (0, 0, 0)
    rows = pl.BlockSpec((1, tm, d), lambda bi, j: (bi, j, 0))
    per_batch = pl.BlockSpec((1, 6, d), lambda bi, j: (bi, 0, 0))
    prev_specs, prev_args = [], []
    if pending_ffn is not None:
        routed, shared, mod_p, g_post_p, tb = pending_ffn
        tiles = tb // tm
        prev_specs = [
            pl.BlockSpec((1, tm * n_chunk, LANES),
                         lambda bi, j: ((bi * (s // tm) + j) // tiles, (bi * (s // tm) + j) % tiles, 0)),
            rows, per_batch, pl.BlockSpec((1, d), const2)]
        prev_args = [routed, shared.reshape(b, s, d), mod_p, g_post_p.reshape(1, d)]
    return pl.pallas_call(
        functools.partial(_mixer_kernel, after_ffn=pending_ffn is not None),
        grid=(b, s // tm),
        in_specs=[
            rows,
            *prev_specs,
            per_batch,
            pl.BlockSpec((1, d), const2),
            pl.BlockSpec((1, d), const2),
            pl.BlockSpec((d, d_in), const2),
            pl.BlockSpec((3, d), const2),
            pl.BlockSpec((d, d), const2),
            pl.BlockSpec(w_pool_group.shape, const3),
            pl.BlockSpec((1, d_pool), const2),
            pl.BlockSpec((d_pool, d), const2),
            pl.BlockSpec((d, d), const2),
        ],
        out_specs=pl.BlockSpec((1, tm, d), lambda bi, j: (bi, j, 0)),
        out_shape=jax.ShapeDtypeStruct(x.shape, F32),
        scratch_shapes=[
            pltpu.VMEM((CONV_HALO + tm, d), F32),
            pltpu.VMEM((POOL_HALO + tm, d_pool), F32),
        ],
        compiler_params=pltpu.CompilerParams(
            dimension_semantics=("arbitrary", "arbitrary"), vmem_limit_bytes=VMEM_LIMIT),
    )(x, *prev_args, mod, g_pre.reshape(1, d), g_post.reshape(1, d), w_in.astype(BF16), conv_w,
      w_conv_out.astype(BF16), w_pool_group.astype(BF16), pool_scale.reshape(1, d_pool),
      w_pool_proj.astype(BF16), w_o.astype(BF16))


def _wins(other, v, tie_i):
    return jnp.where(other > v, 1, 0) + jnp.where(other == v, tie_i, 0)


def _route_kernel(x_ref, mod_ref, g_ref, wrt_ref, bias_ref, wsg_ref, wsu_ref, wsd_ref,
                  h2g_ref, shared_ref, rowl_ref, wl_ref, nch_ref, q0_ref,
                  lg_ref, pre_ref, rho_ref, wd_ref):
    i = pl.program_id(1)
    lc, d = x_ref.shape
    n_sub = lg_ref.shape[0]
    n_chunk = d // LANES

    @pl.when(i < n_sub)
    def _():
        mod = mod_ref[0]
        sh2, sc2 = mod[3:4], mod[4:5]

        @pl.when(i == 0)
        def _():
            spare = n_sub * lc * n_chunk
            h2g_ref[0, spare:spare + n_chunk, :] = jnp.zeros((n_chunk, LANES), F32)

        x = x_ref[...]
        h = x * _rms_scale(x) * g_ref[...] * (1.0 + sc2) + sh2
        for c in range(n_chunk):
            h2g_ref[0, pl.ds(i * lc * n_chunk + c, lc, stride=n_chunk), :] = (
                h[:, c * LANES:(c + 1) * LANES])
        hb = h.astype(BF16)
        act = (_silu(jnp.dot(hb, wsg_ref[...], preferred_element_type=F32))
               * jnp.dot(hb, wsu_ref[...], preferred_element_type=F32))
        shared_ref[...] = jnp.dot(act.astype(BF16), wsd_ref[...], preferred_element_type=F32)
        lg_ref[i] = lax.dot_general(
            wrt_ref[...], hb, (((1,), (1,)), ((), ())), preferred_element_type=F32)

    @pl.when(i == n_sub)
    def _():
        _route_plan(bias_ref, rowl_ref, wl_ref, nch_ref, q0_ref, lg_ref, pre_ref, rho_ref, wd_ref,
                    lc, n_chunk)


def _route_plan(bias_ref, rowl_ref, wl_ref, nch_ref, q0_ref, lg_ref, pre_ref, rho_ref, wd_ref,
                lc, n_chunk):
    n_sub = lg_ref.shape[0]
    tb = n_sub * lc
    gidx = lax.broadcasted_iota(jnp.int32, (N_GROUPS, lc), 0)
    tie = [None] + [jnp.where(gidx >= r, 1, 0) for r in range(1, N_GROUPS)]
    tri = (lax.broadcasted_iota(jnp.int32, (lc, lc), 0)
           < lax.broadcasted_iota(jnp.int32, (lc, lc), 1)).astype(BF16)
    carry = jnp.zeros((N_EXPERTS, 1), F32)
    neg_inf = jnp.float32(-jnp.inf)
    for ci in range(tb // lc):
        c0 = ci * lc
        s_all = jax.nn.sigmoid(lg_ref[ci])
        aff = [s_all[GROUP_SIZE * jj:GROUP_SIZE * (jj + 1), :] for jj in range(GROUP_SIZE)]
        sel = [aff[jj] + bias_ref[GROUP_SIZE * jj:GROUP_SIZE * (jj + 1), :]
               for jj in range(GROUP_SIZE)]
        m1, m2 = sel[0], jnp.full_like(sel[0], neg_inf)
        for jj in range(1, GROUP_SIZE):
            m2 = jnp.maximum(m2, jnp.minimum(m1, sel[jj]))
            m1 = jnp.maximum(m1, sel[jj])
        gs = m1 + m2
        beaten = jnp.zeros((N_GROUPS, lc), jnp.int32)
        for r in range(1, N_GROUPS):
            other = pltpu.roll(gs, r, axis=0)
            beaten = beaten + _wins(other, gs, tie[r])
        gmask = beaten < TOPK_GROUPS
        masked = [jnp.where(gmask, sel[jj], neg_inf) for jj in range(GROUP_SIZE)]
        rolled = [[masked[jj]] + [pltpu.roll(masked[jj], r, axis=0) for r in range(1, N_GROUPS)]
                  for jj in range(GROUP_SIZE)]
        rho = []
        for jj in range(GROUP_SIZE):
            v = masked[jj]
            cnt = jnp.zeros((N_GROUPS, lc), jnp.int32)
            for j2 in range(GROUP_SIZE):
                for r in range(N_GROUPS):
                    if r == 0 and j2 == jj:
                        continue
                    other = rolled[j2][r]
                    if r == 0:
                        wins = (other >= v) if j2 < jj else (other > v)
                        cnt = cnt + jnp.where(wins, 1, 0)
                    else:
                        cnt = cnt + _wins(other, v, tie[r])
            rho.append(cnt)
        chosen = [rho[jj] < TOP_K for jj in range(GROUP_SIZE)]
        ssum = jnp.zeros((N_GROUPS, lc), F32)
        for jj in range(GROUP_SIZE):
            ssum = ssum + jnp.where(chosen[jj], aff[jj], 0.0)
        ssum = jnp.sum(ssum, axis=0, keepdims=True)
        wdense = [jnp.where(chosen[jj], aff[jj] / ssum * ROUTED_SCALE, 0.0)
                  for jj in range(GROUP_SIZE)]
        chosen_f = jnp.concatenate([c.astype(F32) for c in chosen], axis=0)
        prefix = jnp.dot(chosen_f.astype(BF16), tri, preferred_element_type=F32) + carry
        carry = carry + jnp.sum(chosen_f, axis=1, keepdims=True)
        pre_ref[:, c0:c0 + lc] = prefix
        rho_ref[:, c0:c0 + lc] = jnp.concatenate(rho, axis=0)
        wd_ref[:, c0:c0 + lc] = jnp.concatenate(wdense, axis=0)

    m = MOE_CHUNK
    nch_b = jnp.broadcast_to(jnp.floor((carry + (m - 0.5)) * (1.0 / m)), (N_EXPERTS, LANES))
    lower = (lax.broadcasted_iota(jnp.int32, (N_EXPERTS, N_EXPERTS), 1)
             < lax.broadcasted_iota(jnp.int32, (N_EXPERTS, N_EXPERTS), 0)).astype(F32)
    q0_b = jnp.dot(lower, nch_b, preferred_element_type=F32, precision=lax.Precision.HIGHEST)
    nch_ref[0] = nch_b.astype(jnp.int32)
    q0_ref[0] = q0_b.astype(jnp.int32)
    offs_col = q0_b[:, 0:1] * m

    nq = rowl_ref.shape[1]
    iota_q = lax.broadcasted_iota(jnp.int32, (nq, lc), 0).astype(F32)
    iota_r = lax.broadcasted_iota(jnp.int32, (m, lc), 0).astype(F32)
    lists = jnp.zeros((nq, 5 * m), F32)
    for ci in range(tb // lc):
        c0 = ci * lc
        dest_dense = pre_ref[:, c0:c0 + lc] + offs_col
        rho_c = rho_ref[:, c0:c0 + lc]
        w_c = wd_ref[:, c0:c0 + lc]
        tokv = (c0 + 1 + lax.broadcasted_iota(jnp.int32, (1, lc), 1)).astype(F32)
        tok_hi = jnp.floor(tokv * (1.0 / 64))
        tok_lo = tokv - 64.0 * tok_hi
        for k in range(TOP_K):
            hit = rho_c == k
            dk = jnp.sum(jnp.where(hit, dest_dense, 0.0), axis=0, keepdims=True)
            wk = jnp.sum(jnp.where(hit, w_c, 0.0), axis=0, keepdims=True)
            qk = jnp.floor((dk + 0.5) * (1.0 / m))
            rk = dk - m * qk
            w_hi = wk.astype(BF16).astype(F32)
            w_mid = (wk - w_hi).astype(BF16).astype(F32)
            w_lo = wk - w_hi - w_mid
            onehot_q = jnp.where(iota_q == qk + LIST_LEAD, 1.0, 0.0).astype(BF16)
            rmask = iota_r == rk
            vals = jnp.concatenate(
                [jnp.where(rmask, piece, 0.0) for piece in (tok_hi, tok_lo, w_hi, w_mid, w_lo)],
                axis=0).astype(BF16)
            lists = lists + lax.dot_general(onehot_q, vals, (((1,), (1,)), ((), ())),
                                            preferred_element_type=F32)
    tok = lists[:, 0:m] * 64.0 + lists[:, m:2 * m]
    tile = jnp.where(tok == 0.0, float(tb), tok - 1.0)
    rowl_ref[0] = (tile * n_chunk).astype(jnp.int32)
    wl_ref[0] = (lists[:, 2 * m:3 * m] + lists[:, 3 * m:4 * m]) + lists[:, 4 * m:5 * m]


def _route(x1, mod, g_pre, w_router, router_bias, w_sh_gate, w_sh_up, w_sh_down, seq, tb):
    t, d = x1.shape
    nb = t // tb
    n_chunk = d // LANES
    d_sh = w_sh_gate.shape[1]
    perm = jnp.arange(N_EXPERTS).reshape(N_GROUPS, GROUP_SIZE).T.reshape(-1)
    wrt = w_router.T[perm].astype(BF16)
    bias = router_bias[perm].reshape(N_EXPERTS, 1)
    nq = _num_list_rows(tb)
    lc = min(ROUTE_LANES, tb)
    n_sub = tb // lc
    const2 = lambda bi, i: (0, 0)
    per_block = lambda bi, i: (bi, 0, 0)
    sub_tile = lambda bi, i: (bi * n_sub + jnp.minimum(i, n_sub - 1), 0)
    outs = pl.pallas_call(
        _route_kernel,
        grid=(nb, n_sub + 1),
        in_specs=[
            pl.BlockSpec((lc, d), sub_tile),
            pl.BlockSpec((1, 6, d), lambda bi, i: (bi * tb // seq, 0, 0)),
            pl.BlockSpec((1, d), const2),
            pl.BlockSpec((N_EXPERTS, d), const2),
            pl.BlockSpec((N_EXPERTS, 1), const2),
            pl.BlockSpec((d, d_sh), const2),
            pl.BlockSpec((d, d_sh), const2),
            pl.BlockSpec((d_sh, d), const2),
        ],
        out_specs=[
            pl.BlockSpec((1, (tb + 1) * n_chunk, LANES), per_block),
            pl.BlockSpec((lc, d), sub_tile),
            pl.BlockSpec((1, nq, MOE_CHUNK), per_block),
            pl.BlockSpec((1, nq, MOE_CHUNK), per_block),
            pl.BlockSpec((1, N_EXPERTS, LANES), per_block),
            pl.BlockSpec((1, N_EXPERTS, LANES), per_block),
        ],
        out_shape=[
            jax.ShapeDtypeStruct((nb, (tb + 1) * n_chunk, LANES), F32),
            jax.ShapeDtypeStruct((t, d), F32),
            jax.ShapeDtypeStruct((nb, nq, MOE_CHUNK), jnp.int32),
            jax.ShapeDtypeStruct((nb, nq, MOE_CHUNK), F32),
            jax.ShapeDtypeStruct((nb, N_EXPERTS, LANES), jnp.int32),
            jax.ShapeDtypeStruct((nb, N_EXPERTS, LANES), jnp.int32),
        ],
        scratch_shapes=[
            pltpu.VMEM((n_sub, N_EXPERTS, lc), F32),
            pltpu.VMEM((N_EXPERTS, tb), F32),
            pltpu.VMEM((N_EXPERTS, tb), jnp.int32),
            pltpu.VMEM((N_EXPERTS, tb), F32),
        ],
        compiler_params=pltpu.CompilerParams(
            dimension_semantics=("arbitrary", "arbitrary"), vmem_limit_bytes=VMEM_LIMIT),
    )(x1, mod, g_pre.reshape(1, d), wrt, bias, w_sh_gate.astype(BF16), w_sh_up.astype(BF16),
      w_sh_down.astype(BF16))
    return outs


def _num_list_rows(tb):
    rows = -(-TOP_K * tb // MOE_CHUNK) + N_EXPERTS + 2 * LIST_LEAD
    return -(-rows // SUBLANES) * SUBLANES


def _moe_kernel(nch_sm, q0_sm, h2g_hbm, rowl_hbm, wl_ref, wg_ref, wu_ref, wd_ref, o_hbm,
                h2g_ref, acc_ref, xt0_ref, xt1_ref, yt0_ref, yt1_ref, act0_ref, act1_ref,
                wgb_ref, wub_ref, wdb_ref, rowl_sm, experts_sm, sems):
    bi = pl.program_id(0)
    r = pl.program_id(1)
    n_chunk = xt0_ref.shape[0] // (MOE_CHUNK + SUBLANES)
    m = MOE_CHUNK
    stride = m + SUBLANES
    eye = (lax.broadcasted_iota(jnp.int32, (m, m), 0)
           == lax.broadcasted_iota(jnp.int32, (m, m), 1))
    xts, yts, acts = (xt0_ref, xt1_ref), (yt0_ref, yt1_ref), (act0_ref, act1_ref)

    def gather(lrow, xt_ref):
        for mi in range(m):
            row = pl.multiple_of(rowl_sm[lrow, mi], n_chunk)
            xt_ref[pl.ds(mi, n_chunk, stride=stride), :] = h2g_ref[pl.ds(row, n_chunk), :]

    def scatter(lrow, yt_ref):
        for g0 in range(0, m, SUBLANES):
            rows, vals = [], []
            for mi in range(g0, g0 + SUBLANES):
                row = pl.multiple_of(rowl_sm[lrow, mi], n_chunk)
                rows.append(row)
                vals.append(acc_ref[pl.ds(row, n_chunk), :]
                            + yt_ref[pl.ds(mi, n_chunk, stride=stride), :])
            for row, val in zip(rows, vals):
                acc_ref[pl.ds(row, n_chunk), :] = val

    def gate_up(xt_ref, act_ref):
        xs = jnp.concatenate([xt_ref[c * stride:c * stride + m, :] for c in range(n_chunk)],
                             axis=1).astype(BF16)
        act = (_silu(jnp.dot(xs, wgb_ref[...], preferred_element_type=F32))
               * jnp.dot(xs, wub_ref[...], preferred_element_type=F32))
        act_ref[...] = act.astype(BF16)

    def down(lrow, act_ref, slot, yt_ref):
        y = jnp.dot(act_ref[...], wdb_ref[slot], preferred_element_type=F32)
        w_row = wl_ref[0, pl.ds(lrow, 1), :]
        w_col = jnp.sum(jnp.where(eye, w_row, 0.0), axis=1, keepdims=True)
        y = y * w_col
        for c in range(n_chunk):
            yt_ref[c * stride:c * stride + m, :] = y[:, c * LANES:(c + 1) * LANES]

    def step(q, par, down_slot):
        cur, oth = par, 1 - par
        gather(q + 1 + LIST_LEAD, xts[oth])
        gate_up(xts[cur], acts[cur])
        down(q - 1 + LIST_LEAD, acts[oth], down_slot, yts[oth])
        scatter(q - 2 + LIST_LEAD, yts[cur])

    def by_parity(q, fn):
        for par in (0, 1):
            @pl.when((q & 1) == par)
            def _():
                fn(par)

    def block_loads():
        return (pltpu.make_async_copy(rowl_hbm.at[bi], rowl_sm, sems.at[0]),
                pltpu.make_async_copy(h2g_hbm.at[bi], h2g_ref, sems.at[1]))

    def block_store():
        return pltpu.make_async_copy(acc_ref, o_hbm.at[bi], sems.at[2])

    @pl.when(r == 0)
    def _():
        for cp in block_loads():
            cp.start()
        acc_ref[...] = jnp.zeros(acc_ref.shape, F32)
        act1_ref[...] = jnp.zeros(act1_ref.shape, BF16)
        yt0_ref[...] = jnp.zeros(yt0_ref.shape, F32)
        wdb_ref[...] = jnp.zeros(wdb_ref.shape, BF16)
        experts_sm[0] = 0
        for cp in block_loads():
            cp.wait()
        gather(LIST_LEAD, xt0_ref)

    n_chunks = nch_sm[bi * N_EXPERTS + r]
    q_first = q0_sm[bi * N_EXPERTS + r]
    slot = experts_sm[0] & 1

    @pl.when(n_chunks > 0)
    def _():
        wgb_ref[...] = wg_ref[0, 0].astype(BF16)
        wub_ref[...] = wu_ref[0, 0].astype(BF16)
        wdb_ref[slot] = wd_ref[0, 0].astype(BF16)
        by_parity(q_first, lambda par: step(q_first, par, 1 - slot))
        experts_sm[0] = experts_sm[0] + 1

    def chunk(ci, carry):
        q = q_first + ci
        by_parity(q, lambda par: step(q, par, slot))
        return carry

    lax.fori_loop(1, n_chunks, chunk, 0)

    @pl.when(r == N_EXPERTS - 1)
    def _():
        q_last = q_first + n_chunks - 1
        last_slot = (experts_sm[0] - 1) & 1

        def drain(par):
            scatter(q_last - 1 + LIST_LEAD, yts[1 - par])
            down(q_last + LIST_LEAD, acts[par], last_slot, yts[par])
            scatter(q_last + LIST_LEAD, yts[par])

        by_parity(q_last, drain)

        block_store().start()
        block_store().wait()


def _moe(h2g, rowl, wl, nch, q0, w_gate, w_up, w_down, layer, t, d, tb):
    nb = t // tb
    n_chunk = d // LANES
    d_e = w_gate.shape[3]
    m = MOE_CHUNK
    stride = m + SUBLANES
    nq = _num_list_rows(tb)

    def expert_of(r):
        return (r % N_GROUPS) * GROUP_SIZE + r // N_GROUPS

    cur_w = lambda bi, r, c, o: (layer, expert_of(r), 0, 0)
    per_block = lambda bi, r, c, o: (bi, 0, 0)
    staging = pltpu.VMEM((n_chunk * stride, LANES), F32)
    grid_spec = pltpu.PrefetchScalarGridSpec(
        num_scalar_prefetch=2,
        grid=(nb, N_EXPERTS),
        in_specs=[
            pl.BlockSpec(memory_space=pl.ANY),
            pl.BlockSpec(memory_space=pl.ANY),
            pl.BlockSpec((1, nq, m), per_block),
            pl.BlockSpec((1, 1, d, d_e), cur_w),
            pl.BlockSpec((1, 1, d, d_e), cur_w),
            pl.BlockSpec((1, 1, d_e, d), cur_w),
        ],
        out_specs=pl.BlockSpec(memory_space=pl.ANY),
        scratch_shapes=[
            pltpu.VMEM(((tb + 1) * n_chunk, LANES), F32),
            pltpu.VMEM(((tb + 1) * n_chunk, LANES), F32),
            staging, staging, staging, staging,
            pltpu.VMEM((m, d_e), BF16),
            pltpu.VMEM((m, d_e), BF16),
            pltpu.VMEM((d, d_e), BF16),
            pltpu.VMEM((d, d_e), BF16),
            pltpu.VMEM((2, d_e, d), BF16),
            pltpu.SMEM((nq, m), jnp.int32),
            pltpu.SMEM((1,), jnp.int32),
            pltpu.SemaphoreType.DMA((3,)),
        ],
    )
    return pl.pallas_call(
        _moe_kernel,
        grid_spec=grid_spec,
        out_shape=jax.ShapeDtypeStruct((nb, (tb + 1) * n_chunk, LANES), F32),
        compiler_params=pltpu.CompilerParams(
            dimension_semantics=("arbitrary", "arbitrary"), vmem_limit_bytes=VMEM_LIMIT),
    )(nch.reshape(-1), q0.reshape(-1), h2g, rowl, wl, w_gate, w_up, w_down)


def _epilogue_kernel(x_ref, routed_ref, shared_ref, mod_ref, g_ref, o_ref):
    o_ref[...] = _ffn_residual(x_ref[...], routed_ref, shared_ref[...], mod_ref[0][5:6], g_ref[...])


def _epilogue(x1, routed, shared, mod, g_post, seq, tb):
    t, d = x1.shape
    n_chunk = d // LANES
    te = min(EPI_TILE, tb)
    n_sub = tb // te
    row = pl.BlockSpec((te, d), lambda bi, i: (bi * n_sub + i, 0))
    return pl.pallas_call(
        _epilogue_kernel,
        grid=(t // tb, n_sub),
        in_specs=[row,
                  pl.BlockSpec((1, te * n_chunk, LANES), lambda bi, i: (bi, i, 0)),
                  row,
                  pl.BlockSpec((1, 6, d), lambda bi, i: (bi * tb // seq, 0, 0)),
                  pl.BlockSpec((1, d), lambda bi, i: (0, 0))],
        out_specs=row,
        out_shape=jax.ShapeDtypeStruct((t, d), F32),
    )(x1, routed, shared, mod, g_post.reshape(1, d))


def kernel(x, c, w_ada, b_ada, g_pre_mix, g_post_mix, g_pre_ffn, g_post_ffn, w_in, conv_w,
           w_conv_out, w_pool_group, pool_scale, w_pool_proj, w_o, w_router, router_bias,
           w_exp_gate, w_exp_up, w_exp_down, w_sh_gate, w_sh_up, w_sh_down):
    b, s, d = x.shape
    depth = w_ada.shape[0]
    t = b * s
    tb = min(MOE_BLOCK, s)
    mods = _ada_mod(c, w_ada, b_ada).reshape(depth, b, 6, d)
    pending_ffn = None
    for l in range(depth):
        mod = mods[l]
        x = _token_mixer(x, pending_ffn, mod, g_pre_mix[l], g_post_mix[l], w_in[l], conv_w[l],
                         w_conv_out[l], w_pool_group[l], pool_scale[l], w_pool_proj[l], w_o[l])
        x1 = x.reshape(t, d)
        h2g, shared, rowl, wl, nch, q0 = _route(
            x1, mod, g_pre_ffn[l], w_router[l], router_bias[l], w_sh_gate[l], w_sh_up[l],
            w_sh_down[l], s, tb)
        routed = _moe(h2g, rowl, wl, nch[:, :, 0], q0[:, :, 0],
                      w_exp_gate, w_exp_up, w_exp_down, l, t, d, tb)
        pending_ffn = (routed, shared, mod, g_post_ffn[l], tb)
    routed, shared, mod, g_post, tb = pending_ffn
    return _epilogue(x.reshape(t, d), routed, shared, mod, g_post, s, tb).reshape(b, s, d)
```

```python
import functools

import jax
import jax.numpy as jnp
from jax import lax
from jax.experimental import pallas as pl
from jax.experimental.pallas import tpu as pltpu

F32 = jnp.float32
BF16 = jnp.bfloat16

EPS = 1e-6
POOL_WINDOWS = (2, 4, 8, 16)
POOL_GROUP_DIM = 128
N_EXPERTS = 64
N_GROUPS = 8
GROUP_SIZE = 8
TOPK_GROUPS = 4
TOP_K = 8
ROUTED_SCALE = 2.5

LANES = 128
SUBLANES = 8
CONV_HALO = 8
POOL_HALO = 16
VMEM_LIMIT = 56 * 1024 * 1024

MIX_TILE = 512
MIX_CHAINS = 2
ROUTE_LANES = 512
MOE_BLOCK = 4096
MOE_CHUNK = 192
LIST_LEAD = 2
EPI_TILE = 512


def _silu(v):
    return v * jax.nn.sigmoid(v)


def _rms_scale(v):
    return lax.rsqrt(jnp.mean(v * v, axis=-1, keepdims=True) + EPS)


def _ada_kernel(c_ref, w_ref, b_ref, o_ref):
    cond = _silu(c_ref[...])
    o_ref[0] = jnp.dot(cond, w_ref[0], preferred_element_type=F32,
                       precision=lax.Precision.HIGHEST) + b_ref[0]


def _ada_mod(c, w_ada, b_ada):
    depth, d, d6 = w_ada.shape
    b = c.shape[0]
    n_col = d6 // d
    return pl.pallas_call(
        _ada_kernel,
        grid=(depth, n_col),
        in_specs=[
            pl.BlockSpec((b, d), lambda l, n: (0, 0)),
            pl.BlockSpec((1, d, d), lambda l, n: (l, 0, n)),
            pl.BlockSpec((1, 1, d), lambda l, n: (l, 0, n)),
        ],
        out_specs=pl.BlockSpec((1, b, d), lambda l, n: (l, 0, n)),
        out_shape=jax.ShapeDtypeStruct((depth, b, d6), F32),
    )(c, w_ada, b_ada.reshape(depth, 1, d6))


def _ffn_residual(x, routed_ref, shared, gt2, g_post, row0=0):
    rows, d = x.shape
    n_chunk = d // LANES
    routed = jnp.concatenate(
        [routed_ref[0, pl.ds(row0 * n_chunk + c, rows, stride=n_chunk), :]
         for c in range(n_chunk)], axis=1)
    y = routed + shared
    return x + gt2 * (y * _rms_scale(y) * g_post)


def _mixer_kernel(*refs, after_ffn):
    if after_ffn:
        (x_ref, routed_ref, shared_ref, modp_ref, gpp_ref), refs = refs[:5], refs[5:]
    else:
        x_ref, refs = refs[0], refs[1:]
    (mod_ref, gpre_ref, gpost_ref, win_ref, convw_ref, wco_ref, wpg_ref, pscale_ref, wpp_ref,
     wo_ref, o_ref, uext_ref, pext_ref) = refs
    j = pl.program_id(1)
    tm, d = x_ref.shape[1], x_ref.shape[2]
    d_pool = pext_ref.shape[1]

    @pl.when(j == 0)
    def _():
        uext_ref[0:CONV_HALO, :] = jnp.zeros((CONV_HALO, d), F32)
        pext_ref[0:POOL_HALO, :] = jnp.zeros((POOL_HALO, d_pool), F32)

    mod = mod_ref[0]
    sh1, sc1, gt1 = mod[0:1], mod[1:2], mod[2:3]
    cw = convw_ref[...]
    ts = tm // MIX_CHAINS
    for ch in range(MIX_CHAINS):
        r0 = ch * ts
        x = x_ref[0, r0:r0 + ts, :]
        if after_ffn:
            x = _ffn_residual(x, routed_ref, shared_ref[0, r0:r0 + ts, :], modp_ref[0][5:6],
                              gpp_ref[...], row0=r0)
        h = x * _rms_scale(x) * gpre_ref[...] * (1.0 + sc1) + sh1
        hb = h.astype(BF16)

        def proj(lo, hi):
            return jnp.dot(hb, win_ref[:, lo:hi], preferred_element_type=F32)

        u = proj(d, 2 * d) * proj(2 * d, 3 * d)
        u0 = CONV_HALO + r0
        uext_ref[u0:u0 + ts, :] = u
        conv = (cw[2:3] * u
                + cw[1:2] * uext_ref[u0 - 1:u0 - 1 + ts, :]
                + cw[0:1] * uext_ref[u0 - 2:u0 - 2 + ts, :])
        y_conv = jnp.dot((proj(0, d) * conv).astype(BF16), wco_ref[...],
                         preferred_element_type=F32)

        up = proj(3 * d, 3 * d + d_pool)
        p0 = POOL_HALO + r0
        pext_ref[p0:p0 + ts, :] = up
        pos = j * tm + r0 + lax.broadcasted_iota(jnp.int32, (ts, 1), 0)
        zs = []
        for g, w in enumerate(POOL_WINDOWS):
            c0 = g * POOL_GROUP_DIM
            ug = up[:, c0:c0 + POOL_GROUP_DIM]
            acc = ug
            for k in range(1, w):
                acc = acc + pext_ref[p0 - k:p0 - k + ts, c0:c0 + POOL_GROUP_DIM]
            inv_cnt = 1.0 / jnp.minimum(pos + 1, w).astype(F32)
            diff = acc * inv_cnt - ug
            zs.append(jnp.dot(diff.astype(BF16), wpg_ref[g], preferred_element_type=F32))
        z = jnp.concatenate(zs, axis=1) * pscale_ref[...]
        y_pool = jnp.dot(z.astype(BF16), wpp_ref[...], preferred_element_type=F32)

        a_conv = proj(3 * d + d_pool, 4 * d + d_pool)
        a_pool = proj(4 * d + d_pool, 5 * d + d_pool)
        merged = jax.nn.sigmoid(a_conv) * y_conv + jax.nn.sigmoid(a_pool) * y_pool
        y = jnp.dot(merged.astype(BF16), wo_ref[...], preferred_element_type=F32)
        o_ref[0, r0:r0 + ts, :] = x + gt1 * (y * _rms_scale(y) * gpost_ref[...])

    uext_ref[0:CONV_HALO, :] = uext_ref[tm:tm + CONV_HALO, :]
    pext_ref[0:POOL_HALO, :] = pext_ref[tm:tm + POOL_HALO, :]


def _token_mixer(x, pending_ffn, mod, g_pre, g_post, w_in, conv_w, w_conv_out, w_pool_group,
                 pool_scale, w_pool_proj, w_o):
    b, s, d = x.shape
    d_in = w_in.shape[1]
    d_pool = w_pool_proj.shape[0]
    n_chunk = d // LANES
    tm = min(MIX_TILE, s)
    const2 = lambda bi, j: (0, 0)
    const3 = lambda bi, j: (0, 0, 0)
    rows = pl.BlockSpec((1, tm, d), lambda bi, j: (bi, j, 0))
    per_batch = pl.BlockSpec((1, 6, d), lambda bi, j: (bi, 0, 0))
    prev_specs, prev_args = [], []
    if pending_ffn is not None:
        routed, shared, mod_p, g_post_p, tb = pending_ffn
        tiles = tb // tm
        prev_specs = [
            pl.BlockSpec((1, tm * n_chunk, LANES),
                         lambda bi, j: ((bi * (s // tm) + j) // tiles, (bi * (s // tm) + j) % tiles, 0)),
            rows, per_batch, pl.BlockSpec((1, d), const2)]
        prev_args = [routed, shared.reshape(b, s, d), mod_p, g_post_p.reshape(1, d)]
    return pl.pallas_call(
        functools.partial(_mixer_kernel, after_ffn=pending_ffn is not None),
        grid=(b, s // tm),
        in_specs=[
            rows,
            *prev_specs,
            per_batch,
            pl.BlockSpec((1, d), const2),
            pl.BlockSpec((1, d), const2),
            pl.BlockSpec((d, d_in), const2),
            pl.BlockSpec((3, d), const2),
            pl.BlockSpec((d, d), const2),
            pl.BlockSpec(w_pool_group.shape, const3),
            pl.BlockSpec((1, d_pool), const2),
            pl.BlockSpec((d_pool, d), const2),
            pl.BlockSpec((d, d), const2),
        ],
        out_specs=pl.BlockSpec((1, tm, d), lambda bi, j: (bi, j, 0)),
        out_shape=jax.ShapeDtypeStruct(x.shape, F32),
        scratch_shapes=[
            pltpu.VMEM((CONV_HALO + tm, d), F32),
            pltpu.VMEM((POOL_HALO + tm, d_pool), F32),
        ],
        compiler_params=pltpu.CompilerParams(
            dimension_semantics=("arbitrary", "arbitrary"), vmem_limit_bytes=VMEM_LIMIT),
    )(x, *prev_args, mod, g_pre.reshape(1, d), g_post.reshape(1, d), w_in.astype(BF16), conv_w,
      w_conv_out.astype(BF16), w_pool_group.astype(BF16), pool_scale.reshape(1, d_pool),
      w_pool_proj.astype(BF16), w_o.astype(BF16))


def _wins(other, v, tie_i):
    return jnp.where(other > v, 1, 0) + jnp.where(other == v, tie_i, 0)


def _route_kernel(x_ref, mod_ref, g_ref, wrt_ref, bias_ref, wsg_ref, wsu_ref, wsd_ref,
                  h2g_ref, shared_ref, rowl_ref, wl_ref, nch_ref, q0_ref,
                  lg_ref, pre_ref, rho_ref, wd_ref):
    i = pl.program_id(1)
    lc, d = x_ref.shape
    n_sub = lg_ref.shape[0]
    n_chunk = d // LANES

    @pl.when(i < n_sub)
    def _():
        mod = mod_ref[0]
        sh2, sc2 = mod[3:4], mod[4:5]

        @pl.when(i == 0)
        def _():
            spare = n_sub * lc * n_chunk
            h2g_ref[0, spare:spare + n_chunk, :] = jnp.zeros((n_chunk, LANES), F32)

        x = x_ref[...]
        h = x * _rms_scale(x) * g_ref[...] * (1.0 + sc2) + sh2
        for c in range(n_chunk):
            h2g_ref[0, pl.ds(i * lc * n_chunk + c, lc, stride=n_chunk), :] = (
                h[:, c * LANES:(c + 1) * LANES])
        hb = h.astype(BF16)
        act = (_silu(jnp.dot(hb, wsg_ref[...], preferred_element_type=F32))
               * jnp.dot(hb, wsu_ref[...], preferred_element_type=F32))
        shared_ref[...] = jnp.dot(act.astype(BF16), wsd_ref[...], preferred_element_type=F32)
        lg_ref[i] = lax.dot_general(
            wrt_ref[...], hb, (((1,), (1,)), ((), ())), preferred_element_type=F32)

    @pl.when(i == n_sub)
    def _():
        _route_plan(bias_ref, rowl_ref, wl_ref, nch_ref, q0_ref, lg_ref, pre_ref, rho_ref, wd_ref,
                    lc, n_chunk)


def _route_plan(bias_ref, rowl_ref, wl_ref, nch_ref, q0_ref, lg_ref, pre_ref, rho_ref, wd_ref,
                lc, n_chunk):
    n_sub = lg_ref.shape[0]
    tb = n_sub * lc
    gidx = lax.broadcasted_iota(jnp.int32, (N_GROUPS, lc), 0)
    tie = [None] + [jnp.where(gidx >= r, 1, 0) for r in range(1, N_GROUPS)]
    tri = (lax.broadcasted_iota(jnp.int32, (lc, lc), 0)
           < lax.broadcasted_iota(jnp.int32, (lc, lc), 1)).astype(BF16)
    carry = jnp.zeros((N_EXPERTS, 1), F32)
    neg_inf = jnp.float32(-jnp.inf)
    for ci in range(tb // lc):
        c0 = ci * lc
        s_all = jax.nn.sigmoid(lg_ref[ci])
        aff = [s_all[GROUP_SIZE * jj:GROUP_SIZE * (jj + 1), :] for jj in range(GROUP_SIZE)]
        sel = [aff[jj] + bias_ref[GROUP_SIZE * jj:GROUP_SIZE * (jj + 1), :]
               for jj in range(GROUP_SIZE)]
        m1, m2 = sel[0], jnp.full_like(sel[0], neg_inf)
        for jj in range(1, GROUP_SIZE):
            m2 = jnp.maximum(m2, jnp.minimum(m1, sel[jj]))
            m1 = jnp.maximum(m1, sel[jj])
        gs = m1 + m2
        beaten = jnp.zeros((N_GROUPS, lc), jnp.int32)
        for r in range(1, N_GROUPS):
            other = pltpu.roll(gs, r, axis=0)
            beaten = beaten + _wins(other, gs, tie[r])
        gmask = beaten < TOPK_GROUPS
        masked = [jnp.where(gmask, sel[jj], neg_inf) for jj in range(GROUP_SIZE)]
        rolled = [[masked[jj]] + [pltpu.roll(masked[jj], r, axis=0) for r in range(1, N_GROUPS)]
                  for jj in range(GROUP_SIZE)]
        rho = []
        for jj in range(GROUP_SIZE):
            v = masked[jj]
            cnt = jnp.zeros((N_GROUPS, lc), jnp.int32)
            for j2 in range(GROUP_SIZE):
                for r in range(N_GROUPS):
                    if r == 0 and j2 == jj:
                        continue
                    other = rolled[j2][r]
                    if r == 0:
                        wins = (other >= v) if j2 < jj else (other > v)
                        cnt = cnt + jnp.where(wins, 1, 0)
                    else:
                        cnt = cnt + _wins(other, v, tie[r])
            rho.append(cnt)
        chosen = [rho[jj] < TOP_K for jj in range(GROUP_SIZE)]
        ssum = jnp.zeros((N_GROUPS, lc), F32)
        for jj in range(GROUP_SIZE):
            ssum = ssum + jnp.where(chosen[jj], aff[jj], 0.0)
        ssum = jnp.sum(ssum, axis=0, keepdims=True)
        wdense = [jnp.where(chosen[jj], aff[jj] / ssum * ROUTED_SCALE, 0.0)
                  for jj in range(GROUP_SIZE)]
        chosen_f = jnp.concatenate([c.astype(F32) for c in chosen], axis=0)
        prefix = jnp.dot(chosen_f.astype(BF16), tri, preferred_element_type=F32) + carry
        carry = carry + jnp.sum(chosen_f, axis=1, keepdims=True)
        pre_ref[:, c0:c0 + lc] = prefix
        rho_ref[:, c0:c0 + lc] = jnp.concatenate(rho, axis=0)
        wd_ref[:, c0:c0 + lc] = jnp.concatenate(wdense, axis=0)

    m = MOE_CHUNK
    nch_b = jnp.broadcast_to(jnp.floor((carry + (m - 0.5)) * (1.0 / m)), (N_EXPERTS, LANES))
    lower = (lax.broadcasted_iota(jnp.int32, (N_EXPERTS, N_EXPERTS), 1)
             < lax.broadcasted_iota(jnp.int32, (N_EXPERTS, N_EXPERTS), 0)).astype(F32)
    q0_b = jnp.dot(lower, nch_b, preferred_element_type=F32, precision=lax.Precision.HIGHEST)
    nch_ref[0] = nch_b.astype(jnp.int32)
    q0_ref[0] = q0_b.astype(jnp.int32)
    offs_col = q0_b[:, 0:1] * m

    nq = rowl_ref.shape[1]
    iota_q = lax.broadcasted_iota(jnp.int32, (nq, lc), 0).astype(F32)
    iota_r = lax.broadcasted_iota(jnp.int32, (m, lc), 0).astype(F32)
    lists = jnp.zeros((nq, 5 * m), F32)
    for ci in range(tb // lc):
        c0 = ci * lc
        dest_dense = pre_ref[:, c0:c0 + lc] + offs_col
        rho_c = rho_ref[:, c0:c0 + lc]
        w_c = wd_ref[:, c0:c0 + lc]
        tokv = (c0 + 1 + lax.broadcasted_iota(jnp.int32, (1, lc), 1)).astype(F32)
        tok_hi = jnp.floor(tokv * (1.0 / 64))
        tok_lo = tokv - 64.0 * tok_hi
        for k in range(TOP_K):
            hit = rho_c == k
            dk = jnp.sum(jnp.where(hit, dest_dense, 0.0), axis=0, keepdims=True)
            wk = jnp.sum(jnp.where(hit, w_c, 0.0), axis=0, keepdims=True)
            qk = jnp.floor((dk + 0.5) * (1.0 / m))
            rk = dk - m * qk
            w_hi = wk.astype(BF16).astype(F32)
            w_mid = (wk - w_hi).astype(BF16).astype(F32)
            w_lo = wk - w_hi - w_mid
            onehot_q = jnp.where(iota_q == qk + LIST_LEAD, 1.0, 0.0).astype(BF16)
            rmask = iota_r == rk
            vals = jnp.concatenate(
                [jnp.where(rmask, piece, 0.0) for piece in (tok_hi, tok_lo, w_hi, w_mid, w_lo)],
                axis=0).astype(BF16)
            lists = lists + lax.dot_general(onehot_q, vals, (((1,), (1,)), ((), ())),
                                            preferred_element_type=F32)
    tok = lists[:, 0:m] * 64.0 + lists[:, m:2 * m]
    tile = jnp.where(tok == 0.0, float(tb), tok - 1.0)
    rowl_ref[0] = (tile * n_chunk).astype(jnp.int32)
    wl_ref[0] = (lists[:, 2 * m:3 * m] + lists[:, 3 * m:4 * m]) + lists[:, 4 * m:5 * m]


def _route(x1, mod, g_pre, w_router, router_bias, w_sh_gate, w_sh_up, w_sh_down, seq, tb):
    t, d = x1.shape
    nb = t // tb
    n_chunk = d // LANES
    d_sh = w_sh_gate.shape[1]
    perm = jnp.arange(N_EXPERTS).reshape(N_GROUPS, GROUP_SIZE).T.reshape(-1)
    wrt = w_router.T[perm].astype(BF16)
    bias = router_bias[perm].reshape(N_EXPERTS, 1)
    nq = _num_list_rows(tb)
    lc = min(ROUTE_LANES, tb)
    n_sub = tb // lc
    const2 = lambda bi, i: (0, 0)
    per_block = lambda bi, i: (bi, 0, 0)
    sub_tile = lambda bi, i: (bi * n_sub + jnp.minimum(i, n_sub - 1), 0)
    outs = pl.pallas_call(
        _route_kernel,
        grid=(nb, n_sub + 1),
        in_specs=[
            pl.BlockSpec((lc, d), sub_tile),
            pl.BlockSpec((1, 6, d), lambda bi, i: (bi * tb // seq, 0, 0)),
            pl.BlockSpec((1, d), const2),
            pl.BlockSpec((N_EXPERTS, d), const2),
            pl.BlockSpec((N_EXPERTS, 1), const2),
            pl.BlockSpec((d, d_sh), const2),
            pl.BlockSpec((d, d_sh), const2),
            pl.BlockSpec((d_sh, d), const2),
        ],
        out_specs=[
            pl.BlockSpec((1, (tb + 1) * n_chunk, LANES), per_block),
            pl.BlockSpec((lc, d), sub_tile),
            pl.BlockSpec((1, nq, MOE_CHUNK), per_block),
            pl.BlockSpec((1, nq, MOE_CHUNK), per_block),
            pl.BlockSpec((1, N_EXPERTS, LANES), per_block),
            pl.BlockSpec((1, N_EXPERTS, LANES), per_block),
        ],
        out_shape=[
            jax.ShapeDtypeStruct((nb, (tb + 1) * n_chunk, LANES), F32),
            jax.ShapeDtypeStruct((t, d), F32),
            jax.ShapeDtypeStruct((nb, nq, MOE_CHUNK), jnp.int32),
            jax.ShapeDtypeStruct((nb, nq, MOE_CHUNK), F32),
            jax.ShapeDtypeStruct((nb, N_EXPERTS, LANES), jnp.int32),
            jax.ShapeDtypeStruct((nb, N_EXPERTS, LANES), jnp.int32),
        ],
        scratch_shapes=[
            pltpu.VMEM((n_sub, N_EXPERTS, lc), F32),
            pltpu.VMEM((N_EXPERTS, tb), F32),
            pltpu.VMEM((N_EXPERTS, tb), jnp.int32),
            pltpu.VMEM((N_EXPERTS, tb), F32),
        ],
        compiler_params=pltpu.CompilerParams(
            dimension_semantics=("arbitrary", "arbitrary"), vmem_limit_bytes=VMEM_LIMIT),
    )(x1, mod, g_pre.reshape(1, d), wrt, bias, w_sh_gate.astype(BF16), w_sh_up.astype(BF16),
      w_sh_down.astype(BF16))
    return outs


SEM_LOAD, SEM_STORE, SEM_GW, SEM_SW, N_MOE_SEMS = 0, 1, 2, 4, 6


def _num_list_rows(tb):
    rows = -(-TOP_K * tb // MOE_CHUNK) + N_EXPERTS + 2 * LIST_LEAD
    return -(-rows // SUBLANES) * SUBLANES


def _moe_kernel(nch_sm, q0_sm, h2g_hbm, rowl_ref, wl_ref, wg_ref, wu_ref, wd_ref, o_hbm,
                h2g_ref, acc_ref, xt0_ref, xt1_ref, yt0_ref, yt1_ref, act0_ref, act1_ref,
                wgb_ref, wub_ref, wdb_ref, gw0_sm, gw1_sm, sw0_sm, sw1_sm, experts_sm, sems):
    bi = pl.program_id(0)
    r = pl.program_id(1)
    n_chunk = xt0_ref.shape[0] // (MOE_CHUNK + SUBLANES)
    m = MOE_CHUNK
    stride = m + SUBLANES
    eye = (lax.broadcasted_iota(jnp.int32, (m, m), 0)
           == lax.broadcasted_iota(jnp.int32, (m, m), 1))
    xts, yts, acts = (xt0_ref, xt1_ref), (yt0_ref, yt1_ref), (act0_ref, act1_ref)
    gws, sws = (gw0_sm, gw1_sm), (sw0_sm, sw1_sm)

    def gather_rows(q, par):
        return pltpu.make_async_copy(rowl_ref.at[0, q + LIST_LEAD], gws[par], sems.at[SEM_GW + par])

    def scatter_rows(q, par):
        return pltpu.make_async_copy(rowl_ref.at[0, q + LIST_LEAD], sws[par], sems.at[SEM_SW + par])

    def gather(rows_sm, xt_ref):
        for mi in range(m):
            row = pl.multiple_of(rows_sm[0, mi], n_chunk)
            xt_ref[pl.ds(mi, n_chunk, stride=stride), :] = h2g_ref[pl.ds(row, n_chunk), :]

    def scatter(rows_sm, yt_ref):
        for g0 in range(0, m, SUBLANES):
            rows, vals = [], []
            for mi in range(g0, g0 + SUBLANES):
                row = pl.multiple_of(rows_sm[0, mi], n_chunk)
                rows.append(row)
                vals.append(acc_ref[pl.ds(row, n_chunk), :]
                            + yt_ref[pl.ds(mi, n_chunk, stride=stride), :])
            for row, val in zip(rows, vals):
                acc_ref[pl.ds(row, n_chunk), :] = val

    def gate_up(xt_ref, act_ref):
        xs = jnp.concatenate([xt_ref[c * stride:c * stride + m, :] for c in range(n_chunk)],
                             axis=1).astype(BF16)
        act = (_silu(jnp.dot(xs, wgb_ref[...], preferred_element_type=F32))
               * jnp.dot(xs, wub_ref[...], preferred_element_type=F32))
        act_ref[...] = act.astype(BF16)

    def down(lrow, act_ref, slot, yt_ref):
        y = jnp.dot(act_ref[...], wdb_ref[slot], preferred_element_type=F32)
        w_row = wl_ref[0, pl.ds(lrow, 1), :]
        w_col = jnp.sum(jnp.where(eye, w_row, 0.0), axis=1, keepdims=True)
        y = y * w_col
        for c in range(n_chunk):
            yt_ref[c * stride:c * stride + m, :] = y[:, c * LANES:(c + 1) * LANES]

    def step(q, par, down_slot):
        cur, oth = par, 1 - par
        gather_rows(q + 1, oth).wait()
        scatter_rows(q - 2, cur).wait()
        gather_rows(q + 2, cur).start()
        scatter_rows(q - 1, oth).start()
        gather(gws[oth], xts[oth])
        gate_up(xts[cur], acts[cur])
        down(q - 1 + LIST_LEAD, acts[oth], down_slot, yts[oth])
        scatter(sws[cur], yts[cur])

    def by_parity(q, fn):
        for par in (0, 1):
            @pl.when((q & 1) == par)
            def _():
                fn(par)

    def block_load():
        return pltpu.make_async_copy(h2g_hbm.at[bi], h2g_ref, sems.at[SEM_LOAD])

    def block_store():
        return pltpu.make_async_copy(acc_ref, o_hbm.at[bi], sems.at[SEM_STORE])

    @pl.when(r == 0)
    def _():
        block_load().start()
        gather_rows(0, 0).start()
        acc_ref[...] = jnp.zeros(acc_ref.shape, F32)
        act1_ref[...] = jnp.zeros(act1_ref.shape, BF16)
        yt0_ref[...] = jnp.zeros(yt0_ref.shape, F32)
        wdb_ref[...] = jnp.zeros(wdb_ref.shape, BF16)
        experts_sm[0] = 0
        block_load().wait()
        gather_rows(0, 0).wait()
        gather(gws[0], xt0_ref)
        gather_rows(1, 1).start()
        scatter_rows(-2, 0).start()

    n_chunks = nch_sm[bi * N_EXPERTS + r]
    q_first = q0_sm[bi * N_EXPERTS + r]
    slot = experts_sm[0] & 1

    @pl.when(n_chunks > 0)
    def _():
        wgb_ref[...] = wg_ref[0, 0].astype(BF16)
        wub_ref[...] = wu_ref[0, 0].astype(BF16)
        wdb_ref[slot] = wd_ref[0, 0].astype(BF16)
        by_parity(q_first, lambda par: step(q_first, par, 1 - slot))
        experts_sm[0] = experts_sm[0] + 1

    def chunk(ci, carry):
        q = q_first + ci
        by_parity(q, lambda par: step(q, par, slot))
        return carry

    lax.fori_loop(1, n_chunks, chunk, 0)

    @pl.when(r == N_EXPERTS - 1)
    def _():
        q_last = q_first + n_chunks - 1
        last_slot = (experts_sm[0] - 1) & 1

        def drain(par):
            gather_rows(q_last + 2, par).wait()
            scatter_rows(q_last - 1, 1 - par).wait()
            scatter(sws[1 - par], yts[1 - par])
            down(q_last + LIST_LEAD, acts[par], last_slot, yts[par])
            scatter_rows(q_last, par).start()
            scatter_rows(q_last, par).wait()
            scatter(sws[par], yts[par])

        by_parity(q_last, drain)

        block_store().start()
        block_store().wait()


def _moe(h2g, rowl, wl, nch, q0, w_gate, w_up, w_down, layer, t, d, tb):
    nb = t // tb
    n_chunk = d // LANES
    d_e = w_gate.shape[3]
    m = MOE_CHUNK
    stride = m + SUBLANES
    nq = _num_list_rows(tb)

    def expert_of(r):
        return (r % N_GROUPS) * GROUP_SIZE + r // N_GROUPS

    cur_w = lambda bi, r, c, o: (layer, expert_of(r), 0, 0)
    per_block = lambda bi, r, c, o: (bi, 0, 0)
    staging = pltpu.VMEM((n_chunk * stride, LANES), F32)
    row_window = pltpu.SMEM((1, m), jnp.int32)
    grid_spec = pltpu.PrefetchScalarGridSpec(
        num_scalar_prefetch=2,
        grid=(nb, N_EXPERTS),
        in_specs=[
            pl.BlockSpec(memory_space=pl.ANY),
            pl.BlockSpec((1, nq, 1, m), lambda bi, r, c, o: (bi, 0, 0, 0)),
            pl.BlockSpec((1, nq, m), per_block),
            pl.BlockSpec((1, 1, d, d_e), cur_w),
            pl.BlockSpec((1, 1, d, d_e), cur_w),
            pl.BlockSpec((1, 1, d_e, d), cur_w),
        ],
        out_specs=pl.BlockSpec(memory_space=pl.ANY),
        scratch_shapes=[
            pltpu.VMEM(((tb + 1) * n_chunk, LANES), F32),
            pltpu.VMEM(((tb + 1) * n_chunk, LANES), F32),
            staging, staging, staging, staging,
            pltpu.VMEM((m, d_e), BF16),
            pltpu.VMEM((m, d_e), BF16),
            pltpu.VMEM((d, d_e), BF16),
            pltpu.VMEM((d, d_e), BF16),
            pltpu.VMEM((2, d_e, d), BF16),
            row_window, row_window, row_window, row_window,
            pltpu.SMEM((1,), jnp.int32),
            pltpu.SemaphoreType.DMA((N_MOE_SEMS,)),
        ],
    )
    return pl.pallas_call(
        _moe_kernel,
        grid_spec=grid_spec,
        out_shape=jax.ShapeDtypeStruct((nb, (tb + 1) * n_chunk, LANES), F32),
        compiler_params=pltpu.CompilerParams(
            dimension_semantics=("arbitrary", "arbitrary"), vmem_limit_bytes=VMEM_LIMIT),
    )(nch.reshape(-1), q0.reshape(-1), h2g, rowl.reshape(nb, nq, 1, m), wl, w_gate, w_up, w_down)


def _epilogue_kernel(x_ref, routed_ref, shared_ref, mod_ref, g_ref, o_ref):
    o_ref[...] = _ffn_residual(x_ref[...], routed_ref, shared_ref[...], mod_ref[0][5:6], g_ref[...])


def _epilogue(x1, routed, shared, mod, g_post, seq, tb):
    t, d = x1.shape
    n_chunk = d // LANES
    te = min(EPI_TILE, tb)
    n_sub = tb // te
    row = pl.BlockSpec((te, d), lambda bi, i: (bi * n_sub + i, 0))
    return pl.pallas_call(
        _epilogue_kernel,
        grid=(t // tb, n_sub),
        in_specs=[row,
                  pl.BlockSpec((1, te * n_chunk, LANES), lambda bi, i: (bi, i, 0)),
                  row,
                  pl.BlockSpec((1, 6, d), lambda bi, i: (bi * tb // seq, 0, 0)),
                  pl.BlockSpec((1, d), lambda bi, i: (0, 0))],
        out_specs=row,
        out_shape=jax.ShapeDtypeStruct((t, d), F32),
    )(x1, routed, shared, mod, g_post.reshape(1, d))


def kernel(x, c, w_ada, b_ada, g_pre_mix, g_post_mix, g_pre_ffn, g_post_ffn, w_in, conv_w,
           w_conv_out, w_pool_group, pool_scale, w_pool_proj, w_o, w_router, router_bias,
           w_exp_gate, w_exp_up, w_exp_down, w_sh_gate, w_sh_up, w_sh_down):
    b, s, d = x.shape
    depth = w_ada.shape[0]
    t = b * s
    tb = min(MOE_BLOCK, s)
    mods = _ada_mod(c, w_ada, b_ada).reshape(depth, b, 6, d)
    pending_ffn = None
    for l in range(depth):
        mod = mods[l]
        x = _token_mixer(x, pending_ffn, mod, g_pre_mix[l], g_post_mix[l], w_in[l], conv_w[l],
                         w_conv_out[l], w_pool_group[l], pool_scale[l], w_pool_proj[l], w_o[l])
        x1 = x.reshape(t, d)
        h2g, shared, rowl, wl, nch, q0 = _route(
            x1, mod, g_pre_ffn[l], w_router[l], router_bias[l], w_sh_gate[l], w_sh_up[l],
            w_sh_down[l], s, tb)
        routed = _moe(h2g, rowl, wl, nch[:, :, 0], q0[:, :, 0],
                      w_exp_gate, w_exp_up, w_exp_down, l, t, d, tb)
        pending_ffn = (routed, shared, mod, g_post_ffn[l], tb)
    routed, shared, mod, g_post, tb = pending_ffn
    return _epilogue(x.reshape(t, d), routed, shared, mod, g_post, s, tb).reshape(b, s, d)
```

```python
import functools

import jax
import jax.numpy as jnp
from jax import lax
from jax.experimental import pallas as pl
from jax.experimental.pallas import tpu as pltpu

F32 = jnp.float32
BF16 = jnp.bfloat16

EPS = 1e-6
POOL_WINDOWS = (2, 4, 8, 16)
POOL_GROUP_DIM = 128
N_EXPERTS = 64
N_GROUPS = 8
GROUP_SIZE = 8
TOPK_GROUPS = 4
TOP_K = 8
ROUTED_SCALE = 2.5

LANES = 128
SUBLANES = 8
CONV_HALO = 8
POOL_HALO = 16
VMEM_LIMIT = 56 * 1024 * 1024

ADA_COLS = 2048
MIX_TILE = 512
MIX_CHAINS = 2
ROUTE_LANES = 512
MOE_BLOCK = 4096
MOE_CHUNK = 192
SCATTER_GROUP = 8
TOKEN_RADIX = 64.0
LIST_LEAD = 2
EPI_TILE = 512


def _silu(v):
    return v * jax.nn.sigmoid(v)


def _rms_scale(v):
    return lax.rsqrt(jnp.mean(v * v, axis=-1, keepdims=True) + EPS)


def _ada_kernel(c_ref, w_ref, b_ref, o_ref):
    cond = _silu(c_ref[...])
    o_ref[0] = jnp.dot(cond, w_ref[0], preferred_element_type=F32,
                       precision=lax.Precision.HIGHEST) + b_ref[0]


def _ada_mod(c, w_ada, b_ada):
    depth, d, d6 = w_ada.shape
    b = c.shape[0]
    cols = ADA_COLS
    return pl.pallas_call(
        _ada_kernel,
        grid=(depth, d6 // cols),
        in_specs=[
            pl.BlockSpec((b, d), lambda l, n: (0, 0)),
            pl.BlockSpec((1, d, cols), lambda l, n: (l, 0, n)),
            pl.BlockSpec((1, 1, cols), lambda l, n: (l, 0, n)),
        ],
        out_specs=pl.BlockSpec((1, b, cols), lambda l, n: (l, 0, n)),
        out_shape=jax.ShapeDtypeStruct((depth, b, d6), F32),
    )(c, w_ada, b_ada.reshape(depth, 1, d6))


def _ffn_residual(x, routed_ref, shared, gt2, g_post, row0=0):
    rows, d = x.shape
    n_chunk = d // LANES
    routed = jnp.concatenate(
        [routed_ref[0, pl.ds(row0 * n_chunk + c, rows, stride=n_chunk), :]
         for c in range(n_chunk)], axis=1)
    y = routed + shared
    return x + gt2 * (y * _rms_scale(y) * g_post)


def _mixer_kernel(*refs, after_ffn):
    if after_ffn:
        (x_ref, routed_ref, shared_ref, modp_ref, gpp_ref), refs = refs[:5], refs[5:]
    else:
        x_ref, refs = refs[0], refs[1:]
    (mod_ref, gpre_ref, gpost_ref, win_ref, convw_ref, wco_ref, wpg_ref, pscale_ref, wpp_ref,
     wo_ref, o_ref, uext_ref, pext_ref) = refs
    j = pl.program_id(1)
    tm, d = x_ref.shape[1], x_ref.shape[2]
    d_pool = pext_ref.shape[1]

    @pl.when(j == 0)
    def _():
        uext_ref[0:CONV_HALO, :] = jnp.zeros((CONV_HALO, d), F32)
        pext_ref[0:POOL_HALO, :] = jnp.zeros((POOL_HALO, d_pool), F32)

    mod = mod_ref[0]
    sh1, sc1, gt1 = mod[0:1], mod[1:2], mod[2:3]
    cw = convw_ref[...]
    ts = tm // MIX_CHAINS
    for ch in range(MIX_CHAINS):
        r0 = ch * ts
        x = x_ref[0, r0:r0 + ts, :]
        if after_ffn:
            x = _ffn_residual(x, routed_ref, shared_ref[0, r0:r0 + ts, :], modp_ref[0][5:6],
                              gpp_ref[...], row0=r0)
        h = x * _rms_scale(x) * gpre_ref[...] * (1.0 + sc1) + sh1
        hb = h.astype(BF16)

        def proj(lo, hi):
            return jnp.dot(hb, win_ref[:, lo:hi], preferred_element_type=F32)

        u = proj(d, 2 * d) * proj(2 * d, 3 * d)
        u0 = CONV_HALO + r0
        uext_ref[u0:u0 + ts, :] = u
        conv = (cw[2:3] * u
                + cw[1:2] * uext_ref[u0 - 1:u0 - 1 + ts, :]
                + cw[0:1] * uext_ref[u0 - 2:u0 - 2 + ts, :])
        y_conv = jnp.dot((proj(0, d) * conv).astype(BF16), wco_ref[...],
                         preferred_element_type=F32)

        up = proj(3 * d, 3 * d + d_pool)
        p0 = POOL_HALO + r0
        pext_ref[p0:p0 + ts, :] = up
        pos = j * tm + r0 + lax.broadcasted_iota(jnp.int32, (ts, 1), 0)
        zs = []
        for g, w in enumerate(POOL_WINDOWS):
            c0 = g * POOL_GROUP_DIM
            ug = up[:, c0:c0 + POOL_GROUP_DIM]
            acc = ug
            for k in range(1, w):
                acc = acc + pext_ref[p0 - k:p0 - k + ts, c0:c0 + POOL_GROUP_DIM]
            inv_cnt = 1.0 / jnp.minimum(pos + 1, w).astype(F32)
            diff = acc * inv_cnt - ug
            zs.append(jnp.dot(diff.astype(BF16), wpg_ref[g], preferred_element_type=F32))
        z = jnp.concatenate(zs, axis=1) * pscale_ref[...]
        y_pool = jnp.dot(z.astype(BF16), wpp_ref[...], preferred_element_type=F32)

        a_conv = proj(3 * d + d_pool, 4 * d + d_pool)
        a_pool = proj(4 * d + d_pool, 5 * d + d_pool)
        merged = jax.nn.sigmoid(a_conv) * y_conv + jax.nn.sigmoid(a_pool) * y_pool
        y = jnp.dot(merged.astype(BF16), wo_ref[...], preferred_element_type=F32)
        o_ref[0, r0:r0 + ts, :] = x + gt1 * (y * _rms_scale(y) * gpost_ref[...])

    uext_ref[0:CONV_HALO, :] = uext_ref[tm:tm + CONV_HALO, :]
    pext_ref[0:POOL_HALO, :] = pext_ref[tm:tm + POOL_HALO, :]


def _token_mixer(x, pending_ffn, mod, g_pre, g_post, w_in, conv_w, w_conv_out, w_pool_group,
                 pool_scale, w_pool_proj, w_o):
    b, s, d = x.shape
    d_in = w_in.shape[1]
    d_pool = w_pool_proj.shape[0]
    n_chunk = d // LANES
    tm = min(MIX_TILE, s)
    const2 = lambda bi, j: (0, 0)
    const3 = lambda bi, j: (0, 0, 0)
    rows = pl.BlockSpec((1, tm, d), lambda bi, j: (bi, j, 0))
    per_batch = pl.BlockSpec((1, 6, d), lambda bi, j: (bi, 0, 0))
    prev_specs, prev_args = [], []
    if pending_ffn is not None:
        routed, shared, mod_p, g_post_p, tb = pending_ffn
        tiles = tb // tm
        prev_specs = [
            pl.BlockSpec((1, tm * n_chunk, LANES),
                         lambda bi, j: ((bi * (s // tm) + j) // tiles, (bi * (s // tm) + j) % tiles, 0)),
            rows, per_batch, pl.BlockSpec((1, d), const2)]
        prev_args = [routed, shared.reshape(b, s, d), mod_p, g_post_p.reshape(1, d)]
    return pl.pallas_call(
        functools.partial(_mixer_kernel, after_ffn=pending_ffn is not None),
        grid=(b, s // tm),
        in_specs=[
            rows,
            *prev_specs,
            per_batch,
            pl.BlockSpec((1, d), const2),
            pl.BlockSpec((1, d), const2),
            pl.BlockSpec((d, d_in), const2),
            pl.BlockSpec((3, d), const2),
            pl.BlockSpec((d, d), const2),
            pl.BlockSpec(w_pool_group.shape, const3),
            pl.BlockSpec((1, d_pool), const2),
            pl.BlockSpec((d_pool, d), const2),
            pl.BlockSpec((d, d), const2),
        ],
        out_specs=pl.BlockSpec((1, tm, d), lambda bi, j: (bi, j, 0)),
        out_shape=jax.ShapeDtypeStruct(x.shape, F32),
        scratch_shapes=[
            pltpu.VMEM((CONV_HALO + tm, d), F32),
            pltpu.VMEM((POOL_HALO + tm, d_pool), F32),
        ],
        compiler_params=pltpu.CompilerParams(
            dimension_semantics=("arbitrary", "arbitrary"), vmem_limit_bytes=VMEM_LIMIT),
    )(x, *prev_args, mod, g_pre.reshape(1, d), g_post.reshape(1, d), w_in.astype(BF16), conv_w,
      w_conv_out.astype(BF16), w_pool_group.astype(BF16), pool_scale.reshape(1, d_pool),
      w_pool_proj.astype(BF16), w_o.astype(BF16))


def _wins(other, v, tie_i):
    return jnp.where(other > v, 1, 0) + jnp.where(other == v, tie_i, 0)


def _route_kernel(x_ref, mod_ref, g_ref, wrt_ref, bias_ref, wsg_ref, wsu_ref, wsd_ref,
                  h2g_ref, shared_ref, rowl_ref, wl_ref, nch_ref, q0_ref,
                  lg_ref, pre_ref, rho_ref, wd_ref):
    i = pl.program_id(1)
    lc, d = x_ref.shape
    n_sub = lg_ref.shape[0]
    n_chunk = d // LANES

    @pl.when(i < n_sub)
    def _():
        mod = mod_ref[0]
        sh2, sc2 = mod[3:4], mod[4:5]

        @pl.when(i == 0)
        def _():
            spare = n_sub * lc * n_chunk
            h2g_ref[0, spare:spare + n_chunk, :] = jnp.zeros((n_chunk, LANES), F32)

        x = x_ref[...]
        h = x * _rms_scale(x) * g_ref[...] * (1.0 + sc2) + sh2
        for c in range(n_chunk):
            h2g_ref[0, pl.ds(i * lc * n_chunk + c, lc, stride=n_chunk), :] = (
                h[:, c * LANES:(c + 1) * LANES])
        hb = h.astype(BF16)
        act = (_silu(jnp.dot(hb, wsg_ref[...], preferred_element_type=F32))
               * jnp.dot(hb, wsu_ref[...], preferred_element_type=F32))
        shared_ref[...] = jnp.dot(act.astype(BF16), wsd_ref[...], preferred_element_type=F32)
        lg_ref[i] = lax.dot_general(
            wrt_ref[...], hb, (((1,), (1,)), ((), ())), preferred_element_type=F32)

    @pl.when(i == n_sub)
    def _():
        _route_plan(bias_ref, rowl_ref, wl_ref, nch_ref, q0_ref, lg_ref, pre_ref, rho_ref, wd_ref,
                    lc, n_chunk)


def _route_plan(bias_ref, rowl_ref, wl_ref, nch_ref, q0_ref, lg_ref, pre_ref, rho_ref, wd_ref,
                lc, n_chunk):
    n_sub = lg_ref.shape[0]
    tb = n_sub * lc
    gidx = lax.broadcasted_iota(jnp.int32, (N_GROUPS, lc), 0)
    tie = [None] + [jnp.where(gidx >= r, 1, 0) for r in range(1, N_GROUPS)]
    tri = (lax.broadcasted_iota(jnp.int32, (lc, lc), 0)
           < lax.broadcasted_iota(jnp.int32, (lc, lc), 1)).astype(BF16)
    carry = jnp.zeros((N_EXPERTS, 1), F32)
    neg_inf = jnp.float32(-jnp.inf)
    for ci in range(tb // lc):
        c0 = ci * lc
        s_all = jax.nn.sigmoid(lg_ref[ci])
        aff = [s_all[GROUP_SIZE * jj:GROUP_SIZE * (jj + 1), :] for jj in range(GROUP_SIZE)]
        sel = [aff[jj] + bias_ref[GROUP_SIZE * jj:GROUP_SIZE * (jj + 1), :]
               for jj in range(GROUP_SIZE)]
        m1, m2 = sel[0], jnp.full_like(sel[0], neg_inf)
        for jj in range(1, GROUP_SIZE):
            m2 = jnp.maximum(m2, jnp.minimum(m1, sel[jj]))
            m1 = jnp.maximum(m1, sel[jj])
        gs = m1 + m2
        beaten = jnp.zeros((N_GROUPS, lc), jnp.int32)
        for r in range(1, N_GROUPS):
            other = pltpu.roll(gs, r, axis=0)
            beaten = beaten + _wins(other, gs, tie[r])
        gmask = beaten < TOPK_GROUPS
        masked = [jnp.where(gmask, sel[jj], neg_inf) for jj in range(GROUP_SIZE)]
        rolled = [[masked[jj]] + [pltpu.roll(masked[jj], r, axis=0) for r in range(1, N_GROUPS)]
                  for jj in range(GROUP_SIZE)]
        rho = []
        for jj in range(GROUP_SIZE):
            v = masked[jj]
            cnt = jnp.zeros((N_GROUPS, lc), jnp.int32)
            for j2 in range(GROUP_SIZE):
                for r in range(N_GROUPS):
                    if r == 0 and j2 == jj:
                        continue
                    other = rolled[j2][r]
                    if r == 0:
                        wins = (other >= v) if j2 < jj else (other > v)
                        cnt = cnt + jnp.where(wins, 1, 0)
                    else:
                        cnt = cnt + _wins(other, v, tie[r])
            rho.append(cnt)
        chosen = [rho[jj] < TOP_K for jj in range(GROUP_SIZE)]
        ssum = jnp.zeros((N_GROUPS, lc), F32)
        for jj in range(GROUP_SIZE):
            ssum = ssum + jnp.where(chosen[jj], aff[jj], 0.0)
        ssum = jnp.sum(ssum, axis=0, keepdims=True)
        wdense = [jnp.where(chosen[jj], aff[jj] / ssum * ROUTED_SCALE, 0.0)
                  for jj in range(GROUP_SIZE)]
        chosen_f = jnp.concatenate([c.astype(F32) for c in chosen], axis=0)
        prefix = jnp.dot(chosen_f.astype(BF16), tri, preferred_element_type=F32) + carry
        carry = carry + jnp.sum(chosen_f, axis=1, keepdims=True)
        pre_ref[:, c0:c0 + lc] = prefix
        rho_ref[:, c0:c0 + lc] = jnp.concatenate(rho, axis=0)
        wd_ref[:, c0:c0 + lc] = jnp.concatenate(wdense, axis=0)

    m = MOE_CHUNK
    nch_b = jnp.broadcast_to(jnp.floor((carry + (m - 0.5)) * (1.0 / m)), (N_EXPERTS, LANES))
    lower = (lax.broadcasted_iota(jnp.int32, (N_EXPERTS, N_EXPERTS), 1)
             < lax.broadcasted_iota(jnp.int32, (N_EXPERTS, N_EXPERTS), 0)).astype(F32)
    q0_b = jnp.dot(lower, nch_b, preferred_element_type=F32, precision=lax.Precision.HIGHEST)
    nch_ref[0] = nch_b.astype(jnp.int32)
    q0_ref[0] = q0_b.astype(jnp.int32)
    offs_col = q0_b[:, 0:1] * m

    nq = rowl_ref.shape[1]
    iota_q = lax.broadcasted_iota(jnp.int32, (nq, lc), 0).astype(F32)
    iota_r = lax.broadcasted_iota(jnp.int32, (m, lc), 0).astype(F32)
    lists = jnp.zeros((nq, 5 * m), F32)
    for ci in range(tb // lc):
        c0 = ci * lc
        dest_dense = pre_ref[:, c0:c0 + lc] + offs_col
        rho_c = rho_ref[:, c0:c0 + lc]
        w_c = wd_ref[:, c0:c0 + lc]
        tokv = (c0 + 1 + lax.broadcasted_iota(jnp.int32, (1, lc), 1)).astype(F32)
        tok_hi = jnp.floor(tokv * (1.0 / TOKEN_RADIX))
        tok_lo = tokv - TOKEN_RADIX * tok_hi
        for k in range(TOP_K):
            hit = rho_c == k
            dk = jnp.sum(jnp.where(hit, dest_dense, 0.0), axis=0, keepdims=True)
            wk = jnp.sum(jnp.where(hit, w_c, 0.0), axis=0, keepdims=True)
            qk = jnp.floor((dk + 0.5) * (1.0 / m))
            rk = dk - m * qk
            w_hi = wk.astype(BF16).astype(F32)
            w_mid = (wk - w_hi).astype(BF16).astype(F32)
            w_lo = wk - w_hi - w_mid
            onehot_q = jnp.where(iota_q == qk + LIST_LEAD, 1.0, 0.0).astype(BF16)
            rmask = iota_r == rk
            vals = jnp.concatenate(
                [jnp.where(rmask, piece, 0.0) for piece in (tok_hi, tok_lo, w_hi, w_mid, w_lo)],
                axis=0).astype(BF16)
            lists = lists + lax.dot_general(onehot_q, vals, (((1,), (1,)), ((), ())),
                                            preferred_element_type=F32)
    tok = lists[:, 0:m] * TOKEN_RADIX + lists[:, m:2 * m]
    tile = jnp.where(tok == 0.0, float(tb), tok - 1.0)
    rowl_ref[0] = (tile * n_chunk).astype(jnp.int32)
    wl_ref[0] = (lists[:, 2 * m:3 * m] + lists[:, 3 * m:4 * m]) + lists[:, 4 * m:5 * m]


def _route(x1, mod, g_pre, w_router, router_bias, w_sh_gate, w_sh_up, w_sh_down, seq, tb):
    t, d = x1.shape
    nb = t // tb
    n_chunk = d // LANES
    d_sh = w_sh_gate.shape[1]
    perm = jnp.arange(N_EXPERTS).reshape(N_GROUPS, GROUP_SIZE).T.reshape(-1)
    wrt = w_router.T[perm].astype(BF16)
    bias = router_bias[perm].reshape(N_EXPERTS, 1)
    nq = _num_list_rows(tb)
    lc = min(ROUTE_LANES, tb)
    n_sub = tb // lc
    assert tb <= 256 * TOKEN_RADIX and TOKEN_RADIX <= 256, "token digits must be exact in bf16"
    const2 = lambda bi, i: (0, 0)
    per_block = lambda bi, i: (bi, 0, 0)
    sub_tile = lambda bi, i: (bi * n_sub + jnp.minimum(i, n_sub - 1), 0)
    outs = pl.pallas_call(
        _route_kernel,
        grid=(nb, n_sub + 1),
        in_specs=[
            pl.BlockSpec((lc, d), sub_tile),
            pl.BlockSpec((1, 6, d), lambda bi, i: (bi * tb // seq, 0, 0)),
            pl.BlockSpec((1, d), const2),
            pl.BlockSpec((N_EXPERTS, d), const2),
            pl.BlockSpec((N_EXPERTS, 1), const2),
            pl.BlockSpec((d, d_sh), const2),
            pl.BlockSpec((d, d_sh), const2),
            pl.BlockSpec((d_sh, d), const2),
        ],
        out_specs=[
            pl.BlockSpec((1, (tb + 1) * n_chunk, LANES), per_block),
            pl.BlockSpec((lc, d), sub_tile),
            pl.BlockSpec((1, nq, MOE_CHUNK), per_block),
            pl.BlockSpec((1, nq, MOE_CHUNK), per_block),
            pl.BlockSpec((1, N_EXPERTS, LANES), per_block),
            pl.BlockSpec((1, N_EXPERTS, LANES), per_block),
        ],
        out_shape=[
            jax.ShapeDtypeStruct((nb, (tb + 1) * n_chunk, LANES), F32),
            jax.ShapeDtypeStruct((t, d), F32),
            jax.ShapeDtypeStruct((nb, nq, MOE_CHUNK), jnp.int32),
            jax.ShapeDtypeStruct((nb, nq, MOE_CHUNK), F32),
            jax.ShapeDtypeStruct((nb, N_EXPERTS, LANES), jnp.int32),
            jax.ShapeDtypeStruct((nb, N_EXPERTS, LANES), jnp.int32),
        ],
        scratch_shapes=[
            pltpu.VMEM((n_sub, N_EXPERTS, lc), F32),
            pltpu.VMEM((N_EXPERTS, tb), F32),
            pltpu.VMEM((N_EXPERTS, tb), jnp.int32),
            pltpu.VMEM((N_EXPERTS, tb), F32),
        ],
        compiler_params=pltpu.CompilerParams(
            dimension_semantics=("arbitrary", "arbitrary"), vmem_limit_bytes=VMEM_LIMIT),
    )(x1, mod, g_pre.reshape(1, d), wrt, bias, w_sh_gate.astype(BF16), w_sh_up.astype(BF16),
      w_sh_down.astype(BF16))
    return outs


def _num_list_rows(tb):
    rows = -(-TOP_K * tb // MOE_CHUNK) + N_EXPERTS + 2 * LIST_LEAD
    return -(-rows // SUBLANES) * SUBLANES


def _moe_kernel(nch_sm, q0_sm, h2g_hbm, rowl_hbm, wl_ref, wg_ref, wu_ref, wd_ref, o_hbm,
                h2g_ref, acc_ref, xt0_ref, xt1_ref, yt0_ref, yt1_ref, act0_ref, act1_ref,
                wgb_ref, wub_ref, wdb_ref, rowl_sm, experts_sm, sems):
    bi = pl.program_id(0)
    r = pl.program_id(1)
    n_chunk = xt0_ref.shape[0] // (MOE_CHUNK + SUBLANES)
    m = MOE_CHUNK
    stride = m + SUBLANES
    eye = (lax.broadcasted_iota(jnp.int32, (m, m), 0)
           == lax.broadcasted_iota(jnp.int32, (m, m), 1))
    xts, yts, acts = (xt0_ref, xt1_ref), (yt0_ref, yt1_ref), (act0_ref, act1_ref)

    def gather(lrow, xt_ref):
        for mi in range(m):
            row = pl.multiple_of(rowl_sm[lrow, mi], n_chunk)
            xt_ref[pl.ds(mi, n_chunk, stride=stride), :] = h2g_ref[pl.ds(row, n_chunk), :]

    def scatter(lrow, yt_ref):
        for g0 in range(0, m, SCATTER_GROUP):
            rows, vals = [], []
            for mi in range(g0, g0 + SCATTER_GROUP):
                row = pl.multiple_of(rowl_sm[lrow, mi], n_chunk)
                rows.append(row)
                vals.append(acc_ref[pl.ds(row, n_chunk), :]
                            + yt_ref[pl.ds(mi, n_chunk, stride=stride), :])
            for row, val in zip(rows, vals):
                acc_ref[pl.ds(row, n_chunk), :] = val

    def gate_up(xt_ref, act_ref):
        xs = jnp.concatenate([xt_ref[c * stride:c * stride + m, :] for c in range(n_chunk)],
                             axis=1).astype(BF16)
        act = (_silu(jnp.dot(xs, wgb_ref[...], preferred_element_type=F32))
               * jnp.dot(xs, wub_ref[...], preferred_element_type=F32))
        act_ref[...] = act.astype(BF16)

    def down(lrow, act_ref, slot, yt_ref):
        y = jnp.dot(act_ref[...], wdb_ref[slot], preferred_element_type=F32)
        w_row = wl_ref[0, pl.ds(lrow, 1), :]
        w_col = jnp.sum(jnp.where(eye, w_row, 0.0), axis=1, keepdims=True)
        y = y * w_col
        for c in range(n_chunk):
            yt_ref[c * stride:c * stride + m, :] = y[:, c * LANES:(c + 1) * LANES]

    def step(q, par, down_slot):
        cur, oth = par, 1 - par
        gather(q + 1 + LIST_LEAD, xts[oth])
        gate_up(xts[cur], acts[cur])
        down(q - 1 + LIST_LEAD, acts[oth], down_slot, yts[oth])
        scatter(q - 2 + LIST_LEAD, yts[cur])

    def by_parity(q, fn):
        for par in (0, 1):
            @pl.when((q & 1) == par)
            def _():
                fn(par)

    def block_loads():
        return (pltpu.make_async_copy(rowl_hbm.at[bi], rowl_sm, sems.at[0]),
                pltpu.make_async_copy(h2g_hbm.at[bi], h2g_ref, sems.at[1]))

    def block_store():
        return pltpu.make_async_copy(acc_ref, o_hbm.at[bi], sems.at[2])

    @pl.when(r == 0)
    def _():
        for cp in block_loads():
            cp.start()
        acc_ref[...] = jnp.zeros(acc_ref.shape, F32)
        act1_ref[...] = jnp.zeros(act1_ref.shape, BF16)
        yt0_ref[...] = jnp.zeros(yt0_ref.shape, F32)
        wdb_ref[...] = jnp.zeros(wdb_ref.shape, BF16)
        experts_sm[0] = 0
        for cp in block_loads():
            cp.wait()
        gather(LIST_LEAD, xt0_ref)

    n_chunks = nch_sm[bi * N_EXPERTS + r]
    q_first = q0_sm[bi * N_EXPERTS + r]
    slot = experts_sm[0] & 1

    @pl.when(n_chunks > 0)
    def _():
        wgb_ref[...] = wg_ref[0, 0].astype(BF16)
        wub_ref[...] = wu_ref[0, 0].astype(BF16)
        wdb_ref[slot] = wd_ref[0, 0].astype(BF16)
        by_parity(q_first, lambda par: step(q_first, par, 1 - slot))
        experts_sm[0] = experts_sm[0] + 1

    def chunk(ci, carry):
        q = q_first + ci
        by_parity(q, lambda par: step(q, par, slot))
        return carry

    lax.fori_loop(1, n_chunks, chunk, 0)

    @pl.when(r == N_EXPERTS - 1)
    def _():
        q_last = q_first + n_chunks - 1
        last_slot = (experts_sm[0] - 1) & 1

        def drain(par):
            scatter(q_last - 1 + LIST_LEAD, yts[1 - par])
            down(q_last + LIST_LEAD, acts[par], last_slot, yts[par])
            scatter(q_last + LIST_LEAD, yts[par])

        by_parity(q_last, drain)

        block_store().start()
        block_store().wait()


def _moe(h2g, rowl, wl, nch, q0, w_gate, w_up, w_down, layer, t, d, tb):
    nb = t // tb
    n_chunk = d // LANES
    d_e = w_gate.shape[3]
    m = MOE_CHUNK
    stride = m + SUBLANES
    nq = _num_list_rows(tb)

    def expert_of(r):
        return (r % N_GROUPS) * GROUP_SIZE + r // N_GROUPS

    cur_w = lambda bi, r, c, o: (layer, expert_of(r), 0, 0)
    per_block = lambda bi, r, c, o: (bi, 0, 0)
    staging = pltpu.VMEM((n_chunk * stride, LANES), F32)
    grid_spec = pltpu.PrefetchScalarGridSpec(
        num_scalar_prefetch=2,
        grid=(nb, N_EXPERTS),
        in_specs=[
            pl.BlockSpec(memory_space=pl.ANY),
            pl.BlockSpec(memory_space=pl.ANY),
            pl.BlockSpec((1, nq, m), per_block),
            pl.BlockSpec((1, 1, d, d_e), cur_w),
            pl.BlockSpec((1, 1, d, d_e), cur_w),
            pl.BlockSpec((1, 1, d_e, d), cur_w),
        ],
        out_specs=pl.BlockSpec(memory_space=pl.ANY),
        scratch_shapes=[
            pltpu.VMEM(((tb + 1) * n_chunk, LANES), F32),
            pltpu.VMEM(((tb + 1) * n_chunk, LANES), F32),
            staging, staging, staging, staging,
            pltpu.VMEM((m, d_e), BF16),
            pltpu.VMEM((m, d_e), BF16),
            pltpu.VMEM((d, d_e), BF16),
            pltpu.VMEM((d, d_e), BF16),
            pltpu.VMEM((2, d_e, d), BF16),
            pltpu.SMEM((nq, m), jnp.int32),
            pltpu.SMEM((1,), jnp.int32),
            pltpu.SemaphoreType.DMA((3,)),
        ],
    )
    return pl.pallas_call(
        _moe_kernel,
        grid_spec=grid_spec,
        out_shape=jax.ShapeDtypeStruct((nb, (tb + 1) * n_chunk, LANES), F32),
        compiler_params=pltpu.CompilerParams(
            dimension_semantics=("arbitrary", "arbitrary"), vmem_limit_bytes=VMEM_LIMIT),
    )(nch.reshape(-1), q0.reshape(-1), h2g, rowl, wl, w_gate, w_up, w_down)


def _epilogue_kernel(x_ref, routed_ref, shared_ref, mod_ref, g_ref, o_ref):
    o_ref[...] = _ffn_residual(x_ref[...], routed_ref, shared_ref[...], mod_ref[0][5:6], g_ref[...])


def _epilogue(x1, routed, shared, mod, g_post, seq, tb):
    t, d = x1.shape
    n_chunk = d // LANES
    te = min(EPI_TILE, tb)
    n_sub = tb // te
    row = pl.BlockSpec((te, d), lambda bi, i: (bi * n_sub + i, 0))
    return pl.pallas_call(
        _epilogue_kernel,
        grid=(t // tb, n_sub),
        in_specs=[row,
                  pl.BlockSpec((1, te * n_chunk, LANES), lambda bi, i: (bi, i, 0)),
                  row,
                  pl.BlockSpec((1, 6, d), lambda bi, i: (bi * tb // seq, 0, 0)),
                  pl.BlockSpec((1, d), lambda bi, i: (0, 0))],
        out_specs=row,
        out_shape=jax.ShapeDtypeStruct((t, d), F32),
    )(x1, routed, shared, mod, g_post.reshape(1, d))


def kernel(x, c, w_ada, b_ada, g_pre_mix, g_post_mix, g_pre_ffn, g_post_ffn, w_in, conv_w,
           w_conv_out, w_pool_group, pool_scale, w_pool_proj, w_o, w_router, router_bias,
           w_exp_gate, w_exp_up, w_exp_down, w_sh_gate, w_sh_up, w_sh_down):
    b, s, d = x.shape
    depth = w_ada.shape[0]
    t = b * s
    tb = min(MOE_BLOCK, s)
    mods = _ada_mod(c, w_ada, b_ada).reshape(depth, b, 6, d)
    pending_ffn = None
    for l in range(depth):
        mod = mods[l]
        x = _token_mixer(x, pending_ffn, mod, g_pre_mix[l], g_post_mix[l], w_in[l], conv_w[l],
                         w_conv_out[l], w_pool_group[l], pool_scale[l], w_pool_proj[l], w_o[l])
        x1 = x.reshape(t, d)
        h2g, shared, rowl, wl, nch, q0 = _route(
            x1, mod, g_pre_ffn[l], w_router[l], router_bias[l], w_sh_gate[l], w_sh_up[l],
            w_sh_down[l], s, tb)
        routed = _moe(h2g, rowl, wl, nch[:, :, 0], q0[:, :, 0],
                      w_exp_gate, w_exp_up, w_exp_down, l, t, d, tb)
        pending_ffn = (routed, shared, mod, g_post_ffn[l], tb)
    routed, shared, mod, g_post, tb = pending_ffn
    return _epilogue(x.reshape(t, d), routed, shared, mod, g_post, s, tb).reshape(b, s, d)
```

```python
import functools

import jax
import jax.numpy as jnp
from jax import lax
from jax.experimental import pallas as pl
from jax.experimental.pallas import tpu as pltpu

F32 = jnp.float32
BF16 = jnp.bfloat16

EPS = 1e-6
POOL_WINDOWS = (2, 4, 8, 16)
POOL_GROUP_DIM = 128
N_EXPERTS = 64
N_GROUPS = 8
GROUP_SIZE = 8
TOPK_GROUPS = 4
TOP_K = 8
ROUTED_SCALE = 2.5

LANES = 128
SUBLANES = 8
CONV_HALO = 8
POOL_HALO = 16
VMEM_LIMIT = 56 * 1024 * 1024

ADA_COLS = 2048
MIX_TILE = 512
MIX_CHAINS = 2
ROUTE_LANES = 512
MOE_BLOCK = 4096
MOE_CHUNK = 192
SCATTER_GROUP = 8
TOKEN_RADIX = 64.0
LIST_LEAD = 2
EPI_TILE = 512


def _silu(v):
    return v * jax.nn.sigmoid(v)


def _rms_scale(v):
    return lax.rsqrt(jnp.mean(v * v, axis=-1, keepdims=True) + EPS)


def _ada_kernel(c_ref, w_ref, b_ref, o_ref):
    cond = _silu(c_ref[...]).astype(BF16)
    o_ref[0] = jnp.dot(cond, w_ref[0].astype(BF16), preferred_element_type=F32) + b_ref[0]


def _ada_mod(c, w_ada, b_ada):
    depth, d, d6 = w_ada.shape
    b = c.shape[0]
    cols = ADA_COLS
    return pl.pallas_call(
        _ada_kernel,
        grid=(depth, d6 // cols),
        in_specs=[
            pl.BlockSpec((b, d), lambda l, n: (0, 0)),
            pl.BlockSpec((1, d, cols), lambda l, n: (l, 0, n)),
            pl.BlockSpec((1, 1, cols), lambda l, n: (l, 0, n)),
        ],
        out_specs=pl.BlockSpec((1, b, cols), lambda l, n: (l, 0, n)),
        out_shape=jax.ShapeDtypeStruct((depth, b, d6), F32),
    )(c, w_ada, b_ada.reshape(depth, 1, d6))


def _ffn_residual(x, routed_ref, shared, gt2, g_post, row0=0):
    rows, d = x.shape
    n_chunk = d // LANES
    routed = jnp.concatenate(
        [routed_ref[0, pl.ds(row0 * n_chunk + c, rows, stride=n_chunk), :]
         for c in range(n_chunk)], axis=1)
    y = routed + shared
    return x + gt2 * (y * _rms_scale(y) * g_post)


def _mixer_kernel(*refs, after_ffn):
    if after_ffn:
        (x_ref, routed_ref, shared_ref, modp_ref, gpp_ref), refs = refs[:5], refs[5:]
    else:
        x_ref, refs = refs[0], refs[1:]
    (mod_ref, gpre_ref, gpost_ref, win_ref, convw_ref, wco_ref, wpg_ref, pscale_ref, wpp_ref,
     wo_ref, o_ref, uext_ref, pext_ref) = refs
    j = pl.program_id(1)
    tm, d = x_ref.shape[1], x_ref.shape[2]
    d_pool = pext_ref.shape[1]

    @pl.when(j == 0)
    def _():
        uext_ref[0:CONV_HALO, :] = jnp.zeros((CONV_HALO, d), F32)
        pext_ref[0:POOL_HALO, :] = jnp.zeros((POOL_HALO, d_pool), F32)

    mod = mod_ref[0]
    sh1, sc1, gt1 = mod[0:1], mod[1:2], mod[2:3]
    cw = convw_ref[...]
    ts = tm // MIX_CHAINS
    for ch in range(MIX_CHAINS):
        r0 = ch * ts
        x = x_ref[0, r0:r0 + ts, :]
        if after_ffn:
            x = _ffn_residual(x, routed_ref, shared_ref[0, r0:r0 + ts, :], modp_ref[0][5:6],
                              gpp_ref[...], row0=r0)
        h = x * _rms_scale(x) * gpre_ref[...] * (1.0 + sc1) + sh1
        hb = h.astype(BF16)

        def proj(lo, hi):
            return jnp.dot(hb, win_ref[:, lo:hi], preferred_element_type=F32)

        u = proj(d, 2 * d) * proj(2 * d, 3 * d)
        u0 = CONV_HALO + r0
        uext_ref[u0:u0 + ts, :] = u
        conv = (cw[2:3] * u
                + cw[1:2] * uext_ref[u0 - 1:u0 - 1 + ts, :]
                + cw[0:1] * uext_ref[u0 - 2:u0 - 2 + ts, :])
        y_conv = jnp.dot((proj(0, d) * conv).astype(BF16), wco_ref[...],
                         preferred_element_type=F32)

        up = proj(3 * d, 3 * d + d_pool)
        p0 = POOL_HALO + r0
        pext_ref[p0:p0 + ts, :] = up
        pos = j * tm + r0 + lax.broadcasted_iota(jnp.int32, (ts, 1), 0)
        zs = []
        for g, w in enumerate(POOL_WINDOWS):
            c0 = g * POOL_GROUP_DIM
            ug = up[:, c0:c0 + POOL_GROUP_DIM]
            acc = ug
            for k in range(1, w):
                acc = acc + pext_ref[p0 - k:p0 - k + ts, c0:c0 + POOL_GROUP_DIM]
            inv_cnt = 1.0 / jnp.minimum(pos + 1, w).astype(F32)
            diff = acc * inv_cnt - ug
            zs.append(jnp.dot(diff.astype(BF16), wpg_ref[g], preferred_element_type=F32))
        z = jnp.concatenate(zs, axis=1) * pscale_ref[...]
        y_pool = jnp.dot(z.astype(BF16), wpp_ref[...], preferred_element_type=F32)

        a_conv = proj(3 * d + d_pool, 4 * d + d_pool)
        a_pool = proj(4 * d + d_pool, 5 * d + d_pool)
        merged = jax.nn.sigmoid(a_conv) * y_conv + jax.nn.sigmoid(a_pool) * y_pool
        y = jnp.dot(merged.astype(BF16), wo_ref[...], preferred_element_type=F32)
        o_ref[0, r0:r0 + ts, :] = x + gt1 * (y * _rms_scale(y) * gpost_ref[...])

    uext_ref[0:CONV_HALO, :] = uext_ref[tm:tm + CONV_HALO, :]
    pext_ref[0:POOL_HALO, :] = pext_ref[tm:tm + POOL_HALO, :]


def _token_mixer(x, pending_ffn, mod, g_pre, g_post, w_in, conv_w, w_conv_out, w_pool_group,
                 pool_scale, w_pool_proj, w_o):
    b, s, d = x.shape
    d_in = w_in.shape[1]
    d_pool = w_pool_proj.shape[0]
    n_chunk = d // LANES
    tm = min(MIX_TILE, s)
    const2 = lambda bi, j: (0, 0)
    const3 = lambda bi, j: (0, 0, 0)
    rows = pl.BlockSpec((1, tm, d), lambda bi, j: (bi, j, 0))
    per_batch = pl.BlockSpec((1, 6, d), lambda bi, j: (bi, 0, 0))
    prev_specs, prev_args = [], []
    if pending_ffn is not None:
        routed, shared, mod_p, g_post_p, tb = pending_ffn
        tiles = tb // tm
        prev_specs = [
            pl.BlockSpec((1, tm * n_chunk, LANES),
                         lambda bi, j: ((bi * (s // tm) + j) // tiles, (bi * (s // tm) + j) % tiles, 0)),
            rows, per_batch, pl.BlockSpec((1, d), const2)]
        prev_args = [routed, shared.reshape(b, s, d), mod_p, g_post_p.reshape(1, d)]
    return pl.pallas_call(
        functools.partial(_mixer_kernel, after_ffn=pending_ffn is not None),
        grid=(b, s // tm),
        in_specs=[
            rows,
            *prev_specs,
            per_batch,
            pl.BlockSpec((1, d), const2),
            pl.BlockSpec((1, d), const2),
            pl.BlockSpec((d, d_in), const2),
            pl.BlockSpec((3, d), const2),
            pl.BlockSpec((d, d), const2),
            pl.BlockSpec(w_pool_group.shape, const3),
            pl.BlockSpec((1, d_pool), const2),
            pl.BlockSpec((d_pool, d), const2),
            pl.BlockSpec((d, d), const2),
        ],
        out_specs=pl.BlockSpec((1, tm, d), lambda bi, j: (bi, j, 0)),
        out_shape=jax.ShapeDtypeStruct(x.shape, F32),
        scratch_shapes=[
            pltpu.VMEM((CONV_HALO + tm, d), F32),
            pltpu.VMEM((POOL_HALO + tm, d_pool), F32),
        ],
        compiler_params=pltpu.CompilerParams(
            dimension_semantics=("arbitrary", "arbitrary"), vmem_limit_bytes=VMEM_LIMIT),
    )(x, *prev_args, mod, g_pre.reshape(1, d), g_post.reshape(1, d), w_in.astype(BF16), conv_w,
      w_conv_out.astype(BF16), w_pool_group.astype(BF16), pool_scale.reshape(1, d_pool),
      w_pool_proj.astype(BF16), w_o.astype(BF16))


def _wins(other, v, tie_i):
    return jnp.where(other > v, 1, 0) + jnp.where(other == v, tie_i, 0)


def _route_kernel(x_ref, mod_ref, g_ref, wrt_ref, bias_ref, wsg_ref, wsu_ref, wsd_ref,
                  h2g_ref, shared_ref, rowl_ref, wl_ref, nch_ref, q0_ref,
                  lg_ref, pre_ref, rho_ref, wd_ref):
    i = pl.program_id(1)
    lc, d = x_ref.shape
    n_sub = lg_ref.shape[0]
    n_chunk = d // LANES

    @pl.when(i < n_sub)
    def _():
        mod = mod_ref[0]
        sh2, sc2 = mod[3:4], mod[4:5]

        @pl.when(i == 0)
        def _():
            spare = n_sub * lc * n_chunk
            h2g_ref[0, spare:spare + n_chunk, :] = jnp.zeros((n_chunk, LANES), F32)

        x = x_ref[...]
        h = x * _rms_scale(x) * g_ref[...] * (1.0 + sc2) + sh2
        for c in range(n_chunk):
            h2g_ref[0, pl.ds(i * lc * n_chunk + c, lc, stride=n_chunk), :] = (
                h[:, c * LANES:(c + 1) * LANES])
        hb = h.astype(BF16)
        act = (_silu(jnp.dot(hb, wsg_ref[...], preferred_element_type=F32))
               * jnp.dot(hb, wsu_ref[...], preferred_element_type=F32))
        shared_ref[...] = jnp.dot(act.astype(BF16), wsd_ref[...], preferred_element_type=F32)
        lg_ref[i] = lax.dot_general(
            wrt_ref[...], hb, (((1,), (1,)), ((), ())), preferred_element_type=F32)

    @pl.when(i == n_sub)
    def _():
        _route_plan(bias_ref, rowl_ref, wl_ref, nch_ref, q0_ref, lg_ref, pre_ref, rho_ref, wd_ref,
                    lc, n_chunk)


def _route_plan(bias_ref, rowl_ref, wl_ref, nch_ref, q0_ref, lg_ref, pre_ref, rho_ref, wd_ref,
                lc, n_chunk):
    n_sub = lg_ref.shape[0]
    tb = n_sub * lc
    gidx = lax.broadcasted_iota(jnp.int32, (N_GROUPS, lc), 0)
    tie = [None] + [jnp.where(gidx >= r, 1, 0) for r in range(1, N_GROUPS)]
    tri = (lax.broadcasted_iota(jnp.int32, (lc, lc), 0)
           < lax.broadcasted_iota(jnp.int32, (lc, lc), 1)).astype(BF16)
    carry = jnp.zeros((N_EXPERTS, 1), F32)
    neg_inf = jnp.float32(-jnp.inf)
    for ci in range(tb // lc):
        c0 = ci * lc
        s_all = jax.nn.sigmoid(lg_ref[ci])
        aff = [s_all[GROUP_SIZE * jj:GROUP_SIZE * (jj + 1), :] for jj in range(GROUP_SIZE)]
        sel = [aff[jj] + bias_ref[GROUP_SIZE * jj:GROUP_SIZE * (jj + 1), :]
               for jj in range(GROUP_SIZE)]
        m1, m2 = sel[0], jnp.full_like(sel[0], neg_inf)
        for jj in range(1, GROUP_SIZE):
            m2 = jnp.maximum(m2, jnp.minimum(m1, sel[jj]))
            m1 = jnp.maximum(m1, sel[jj])
        gs = m1 + m2
        beaten = jnp.zeros((N_GROUPS, lc), jnp.int32)
        for r in range(1, N_GROUPS):
            other = pltpu.roll(gs, r, axis=0)
            beaten = beaten + _wins(other, gs, tie[r])
        gmask = beaten < TOPK_GROUPS
        masked = [jnp.where(gmask, sel[jj], neg_inf) for jj in range(GROUP_SIZE)]
        rolled = [[masked[jj]] + [pltpu.roll(masked[jj], r, axis=0) for r in range(1, N_GROUPS)]
                  for jj in range(GROUP_SIZE)]
        rho = []
        for jj in range(GROUP_SIZE):
            v = masked[jj]
            cnt = jnp.zeros((N_GROUPS, lc), jnp.int32)
            for j2 in range(GROUP_SIZE):
                for r in range(N_GROUPS):
                    if r == 0 and j2 == jj:
                        continue
                    other = rolled[j2][r]
                    if r == 0:
                        wins = (other >= v) if j2 < jj else (other > v)
                        cnt = cnt + jnp.where(wins, 1, 0)
                    else:
                        cnt = cnt + _wins(other, v, tie[r])
            rho.append(cnt)
        chosen = [rho[jj] < TOP_K for jj in range(GROUP_SIZE)]
        ssum = jnp.zeros((N_GROUPS, lc), F32)
        for jj in range(GROUP_SIZE):
            ssum = ssum + jnp.where(chosen[jj], aff[jj], 0.0)
        ssum = jnp.sum(ssum, axis=0, keepdims=True)
        wdense = [jnp.where(chosen[jj], aff[jj] / ssum * ROUTED_SCALE, 0.0)
                  for jj in range(GROUP_SIZE)]
        chosen_f = jnp.concatenate([c.astype(F32) for c in chosen], axis=0)
        prefix = jnp.dot(chosen_f.astype(BF16), tri, preferred_element_type=F32) + carry
        carry = carry + jnp.sum(chosen_f, axis=1, keepdims=True)
        pre_ref[:, c0:c0 + lc] = prefix
        rho_ref[:, c0:c0 + lc] = jnp.concatenate(rho, axis=0)
        wd_ref[:, c0:c0 + lc] = jnp.concatenate(wdense, axis=0)

    m = MOE_CHUNK
    nch_b = jnp.broadcast_to(jnp.floor((carry + (m - 0.5)) * (1.0 / m)), (N_EXPERTS, LANES))
    lower = (lax.broadcasted_iota(jnp.int32, (N_EXPERTS, N_EXPERTS), 1)
             < lax.broadcasted_iota(jnp.int32, (N_EXPERTS, N_EXPERTS), 0)).astype(F32)
    q0_b = jnp.dot(lower, nch_b, preferred_element_type=F32, precision=lax.Precision.HIGHEST)
    nch_ref[0] = nch_b.astype(jnp.int32)
    q0_ref[0] = q0_b.astype(jnp.int32)
    offs_col = q0_b[:, 0:1] * m

    nq = rowl_ref.shape[1]
    iota_q = lax.broadcasted_iota(jnp.int32, (nq, lc), 0).astype(F32)
    iota_r = lax.broadcasted_iota(jnp.int32, (m, lc), 0).astype(F32)
    lists = jnp.zeros((nq, 5 * m), F32)
    for ci in range(tb // lc):
        c0 = ci * lc
        dest_dense = pre_ref[:, c0:c0 + lc] + offs_col
        rho_c = rho_ref[:, c0:c0 + lc]
        w_c = wd_ref[:, c0:c0 + lc]
        tokv = (c0 + 1 + lax.broadcasted_iota(jnp.int32, (1, lc), 1)).astype(F32)
        tok_hi = jnp.floor(tokv * (1.0 / TOKEN_RADIX))
        tok_lo = tokv - TOKEN_RADIX * tok_hi
        for k in range(TOP_K):
            hit = rho_c == k
            dk = jnp.sum(jnp.where(hit, dest_dense, 0.0), axis=0, keepdims=True)
            wk = jnp.sum(jnp.where(hit, w_c, 0.0), axis=0, keepdims=True)
            qk = jnp.floor((dk + 0.5) * (1.0 / m))
            rk = dk - m * qk
            w_hi = wk.astype(BF16).astype(F32)
            w_mid = (wk - w_hi).astype(BF16).astype(F32)
            w_lo = wk - w_hi - w_mid
            onehot_q = jnp.where(iota_q == qk + LIST_LEAD, 1.0, 0.0).astype(BF16)
            rmask = iota_r == rk
            vals = jnp.concatenate(
                [jnp.where(rmask, piece, 0.0) for piece in (tok_hi, tok_lo, w_hi, w_mid, w_lo)],
                axis=0).astype(BF16)
            lists = lists + lax.dot_general(onehot_q, vals, (((1,), (1,)), ((), ())),
                                            preferred_element_type=F32)
    tok = lists[:, 0:m] * TOKEN_RADIX + lists[:, m:2 * m]
    tile = jnp.where(tok == 0.0, float(tb), tok - 1.0)
    rowl_ref[0] = (tile * n_chunk).astype(jnp.int32)
    wl_ref[0] = (lists[:, 2 * m:3 * m] + lists[:, 3 * m:4 * m]) + lists[:, 4 * m:5 * m]


def _route(x1, mod, g_pre, w_router, router_bias, w_sh_gate, w_sh_up, w_sh_down, seq, tb):
    t, d = x1.shape
    nb = t // tb
    n_chunk = d // LANES
    d_sh = w_sh_gate.shape[1]
    perm = jnp.arange(N_EXPERTS).reshape(N_GROUPS, GROUP_SIZE).T.reshape(-1)
    wrt = w_router.T[perm].astype(BF16)
    bias = router_bias[perm].reshape(N_EXPERTS, 1)
    nq = _num_list_rows(tb)
    lc = min(ROUTE_LANES, tb)
    n_sub = tb // lc
    assert tb <= 256 * TOKEN_RADIX and TOKEN_RADIX <= 256, "token digits must be exact in bf16"
    const2 = lambda bi, i: (0, 0)
    per_block = lambda bi, i: (bi, 0, 0)
    sub_tile = lambda bi, i: (bi * n_sub + jnp.minimum(i, n_sub - 1), 0)
    outs = pl.pallas_call(
        _route_kernel,
        grid=(nb, n_sub + 1),
        in_specs=[
            pl.BlockSpec((lc, d), sub_tile),
            pl.BlockSpec((1, 6, d), lambda bi, i: (bi * tb // seq, 0, 0)),
            pl.BlockSpec((1, d), const2),
            pl.BlockSpec((N_EXPERTS, d), const2),
            pl.BlockSpec((N_EXPERTS, 1), const2),
            pl.BlockSpec((d, d_sh), const2),
            pl.BlockSpec((d, d_sh), const2),
            pl.BlockSpec((d_sh, d), const2),
        ],
        out_specs=[
            pl.BlockSpec((1, (tb + 1) * n_chunk, LANES), per_block),
            pl.BlockSpec((lc, d), sub_tile),
            pl.BlockSpec((1, nq, MOE_CHUNK), per_block),
            pl.BlockSpec((1, nq, MOE_CHUNK), per_block),
            pl.BlockSpec((1, N_EXPERTS, LANES), per_block),
            pl.BlockSpec((1, N_EXPERTS, LANES), per_block),
        ],
        out_shape=[
            jax.ShapeDtypeStruct((nb, (tb + 1) * n_chunk, LANES), F32),
            jax.ShapeDtypeStruct((t, d), F32),
            jax.ShapeDtypeStruct((nb, nq, MOE_CHUNK), jnp.int32),
            jax.ShapeDtypeStruct((nb, nq, MOE_CHUNK), F32),
            jax.ShapeDtypeStruct((nb, N_EXPERTS, LANES), jnp.int32),
            jax.ShapeDtypeStruct((nb, N_EXPERTS, LANES), jnp.int32),
        ],
        scratch_shapes=[
            pltpu.VMEM((n_sub, N_EXPERTS, lc), F32),
            pltpu.VMEM((N_EXPERTS, tb), F32),
            pltpu.VMEM((N_EXPERTS, tb), jnp.int32),
            pltpu.VMEM((N_EXPERTS, tb), F32),
        ],
        compiler_params=pltpu.CompilerParams(
            dimension_semantics=("arbitrary", "arbitrary"), vmem_limit_bytes=VMEM_LIMIT),
    )(x1, mod, g_pre.reshape(1, d), wrt, bias, w_sh_gate.astype(BF16), w_sh_up.astype(BF16),
      w_sh_down.astype(BF16))
    return outs


def _num_list_rows(tb):
    rows = -(-TOP_K * tb // MOE_CHUNK) + N_EXPERTS + 2 * LIST_LEAD
    return -(-rows // SUBLANES) * SUBLANES


def _moe_kernel(nch_sm, q0_sm, h2g_hbm, rowl_hbm, wl_ref, wg_ref, wu_ref, wd_ref, o_hbm,
                h2g_ref, acc_ref, xt0_ref, xt1_ref, yt0_ref, yt1_ref, act0_ref, act1_ref,
                wgb_ref, wub_ref, wdb_ref, rowl_sm, experts_sm, sems):
    bi = pl.program_id(0)
    r = pl.program_id(1)
    n_chunk = xt0_ref.shape[0] // (MOE_CHUNK + SUBLANES)
    m = MOE_CHUNK
    stride = m + SUBLANES
    eye = (lax.broadcasted_iota(jnp.int32, (m, m), 0)
           == lax.broadcasted_iota(jnp.int32, (m, m), 1))
    xts, yts, acts = (xt0_ref, xt1_ref), (yt0_ref, yt1_ref), (act0_ref, act1_ref)

    def gather(lrow, xt_ref):
        for mi in range(m):
            row = pl.multiple_of(rowl_sm[lrow, mi], n_chunk)
            xt_ref[pl.ds(mi, n_chunk, stride=stride), :] = h2g_ref[pl.ds(row, n_chunk), :]

    def scatter(lrow, yt_ref):
        for g0 in range(0, m, SCATTER_GROUP):
            rows, vals = [], []
            for mi in range(g0, g0 + SCATTER_GROUP):
                row = pl.multiple_of(rowl_sm[lrow, mi], n_chunk)
                rows.append(row)
                vals.append(acc_ref[pl.ds(row, n_chunk), :]
                            + yt_ref[pl.ds(mi, n_chunk, stride=stride), :])
            for row, val in zip(rows, vals):
                acc_ref[pl.ds(row, n_chunk), :] = val

    def gate_up(xt_ref, act_ref):
        xs = jnp.concatenate([xt_ref[c * stride:c * stride + m, :] for c in range(n_chunk)],
                             axis=1).astype(BF16)
        act = (_silu(jnp.dot(xs, wgb_ref[...], preferred_element_type=F32))
               * jnp.dot(xs, wub_ref[...], preferred_element_type=F32))
        act_ref[...] = act.astype(BF16)

    def down(lrow, act_ref, slot, yt_ref):
        y = jnp.dot(act_ref[...], wdb_ref[slot], preferred_element_type=F32)
        w_row = wl_ref[0, pl.ds(lrow, 1), :]
        w_col = jnp.sum(jnp.where(eye, w_row, 0.0), axis=1, keepdims=True)
        y = y * w_col
        for c in range(n_chunk):
            yt_ref[c * stride:c * stride + m, :] = y[:, c * LANES:(c + 1) * LANES]

    def step(q, par, down_slot):
        cur, oth = par, 1 - par
        gather(q + 1 + LIST_LEAD, xts[oth])
        gate_up(xts[cur], acts[cur])
        down(q - 1 + LIST_LEAD, acts[oth], down_slot, yts[oth])
        scatter(q - 2 + LIST_LEAD, yts[cur])

    def by_parity(q, fn):
        for par in (0, 1):
            @pl.when((q & 1) == par)
            def _():
                fn(par)

    def block_loads():
        return (pltpu.make_async_copy(rowl_hbm.at[bi], rowl_sm, sems.at[0]),
                pltpu.make_async_copy(h2g_hbm.at[bi], h2g_ref, sems.at[1]))

    def block_store():
        return pltpu.make_async_copy(acc_ref, o_hbm.at[bi], sems.at[2])

    @pl.when(r == 0)
    def _():
        for cp in block_loads():
            cp.start()
        acc_ref[...] = jnp.zeros(acc_ref.shape, F32)
        act1_ref[...] = jnp.zeros(act1_ref.shape, BF16)
        yt0_ref[...] = jnp.zeros(yt0_ref.shape, F32)
        wdb_ref[...] = jnp.zeros(wdb_ref.shape, BF16)
        experts_sm[0] = 0
        for cp in block_loads():
            cp.wait()
        gather(LIST_LEAD, xt0_ref)

    n_chunks = nch_sm[bi * N_EXPERTS + r]
    q_first = q0_sm[bi * N_EXPERTS + r]
    slot = experts_sm[0] & 1

    @pl.when(n_chunks > 0)
    def _():
        wgb_ref[...] = wg_ref[0, 0].astype(BF16)
        wub_ref[...] = wu_ref[0, 0].astype(BF16)
        wdb_ref[slot] = wd_ref[0, 0].astype(BF16)
        by_parity(q_first, lambda par: step(q_first, par, 1 - slot))
        experts_sm[0] = experts_sm[0] + 1

    def chunk(ci, carry):
        q = q_first + ci
        by_parity(q, lambda par: step(q, par, slot))
        return carry

    lax.fori_loop(1, n_chunks, chunk, 0)

    @pl.when(r == N_EXPERTS - 1)
    def _():
        q_last = q_first + n_chunks - 1
        last_slot = (experts_sm[0] - 1) & 1

        def drain(par):
            scatter(q_last - 1 + LIST_LEAD, yts[1 - par])
            down(q_last + LIST_LEAD, acts[par], last_slot, yts[par])
            scatter(q_last + LIST_LEAD, yts[par])

        by_parity(q_last, drain)

        block_store().start()
        block_store().wait()


def _moe(h2g, rowl, wl, nch, q0, w_gate, w_up, w_down, layer, t, d, tb):
    nb = t // tb
    n_chunk = d // LANES
    d_e = w_gate.shape[3]
    m = MOE_CHUNK
    stride = m + SUBLANES
    nq = _num_list_rows(tb)

    def expert_of(r):
        return (r % N_GROUPS) * GROUP_SIZE + r // N_GROUPS

    cur_w = lambda bi, r, c, o: (layer, expert_of(r), 0, 0)
    per_block = lambda bi, r, c, o: (bi, 0, 0)
    staging = pltpu.VMEM((n_chunk * stride, LANES), F32)
    grid_spec = pltpu.PrefetchScalarGridSpec(
        num_scalar_prefetch=2,
        grid=(nb, N_EXPERTS),
        in_specs=[
            pl.BlockSpec(memory_space=pl.ANY),
            pl.BlockSpec(memory_space=pl.ANY),
            pl.BlockSpec((1, nq, m), per_block),
            pl.BlockSpec((1, 1, d, d_e), cur_w),
            pl.BlockSpec((1, 1, d, d_e), cur_w),
            pl.BlockSpec((1, 1, d_e, d), cur_w),
        ],
        out_specs=pl.BlockSpec(memory_space=pl.ANY),
        scratch_shapes=[
            pltpu.VMEM(((tb + 1) * n_chunk, LANES), F32),
            pltpu.VMEM(((tb + 1) * n_chunk, LANES), F32),
            staging, staging, staging, staging,
            pltpu.VMEM((m, d_e), BF16),
            pltpu.VMEM((m, d_e), BF16),
            pltpu.VMEM((d, d_e), BF16),
            pltpu.VMEM((d, d_e), BF16),
            pltpu.VMEM((2, d_e, d), BF16),
            pltpu.SMEM((nq, m), jnp.int32),
            pltpu.SMEM((1,), jnp.int32),
            pltpu.SemaphoreType.DMA((3,)),
        ],
    )
    return pl.pallas_call(
        _moe_kernel,
        grid_spec=grid_spec,
        out_shape=jax.ShapeDtypeStruct((nb, (tb + 1) * n_chunk, LANES), F32),
        compiler_params=pltpu.CompilerParams(
            dimension_semantics=("arbitrary", "arbitrary"), vmem_limit_bytes=VMEM_LIMIT),
    )(nch.reshape(-1), q0.reshape(-1), h2g, rowl, wl, w_gate, w_up, w_down)


def _epilogue_kernel(x_ref, routed_ref, shared_ref, mod_ref, g_ref, o_ref):
    o_ref[...] = _ffn_residual(x_ref[...], routed_ref, shared_ref[...], mod_ref[0][5:6], g_ref[...])


def _epilogue(x1, routed, shared, mod, g_post, seq, tb):
    t, d = x1.shape
    n_chunk = d // LANES
    te = min(EPI_TILE, tb)
    n_sub = tb // te
    row = pl.BlockSpec((te, d), lambda bi, i: (bi * n_sub + i, 0))
    return pl.pallas_call(
        _epilogue_kernel,
        grid=(t // tb, n_sub),
        in_specs=[row,
                  pl.BlockSpec((1, te * n_chunk, LANES), lambda bi, i: (bi, i, 0)),
                  row,
                  pl.BlockSpec((1, 6, d), lambda bi, i: (bi * tb // seq, 0, 0)),
                  pl.BlockSpec((1, d), lambda bi, i: (0, 0))],
        out_specs=row,
        out_shape=jax.ShapeDtypeStruct((t, d), F32),
    )(x1, routed, shared, mod, g_post.reshape(1, d))


def kernel(x, c, w_ada, b_ada, g_pre_mix, g_post_mix, g_pre_ffn, g_post_ffn, w_in, conv_w,
           w_conv_out, w_pool_group, pool_scale, w_pool_proj, w_o, w_router, router_bias,
           w_exp_gate, w_exp_up, w_exp_down, w_sh_gate, w_sh_up, w_sh_down):
    b, s, d = x.shape
    depth = w_ada.shape[0]
    t = b * s
    tb = min(MOE_BLOCK, s)
    mods = _ada_mod(c, w_ada, b_ada).reshape(depth, b, 6, d)
    pending_ffn = None
    for l in range(depth):
        mod = mods[l]
        x = _token_mixer(x, pending_ffn, mod, g_pre_mix[l], g_post_mix[l], w_in[l], conv_w[l],
                         w_conv_out[l], w_pool_group[l], pool_scale[l], w_pool_proj[l], w_o[l])
        x1 = x.reshape(t, d)
        h2g, shared, rowl, wl, nch, q0 = _route(
            x1, mod, g_pre_ffn[l], w_router[l], router_bias[l], w_sh_gate[l], w_sh_up[l],
            w_sh_down[l], s, tb)
        routed = _moe(h2g, rowl, wl, nch[:, :, 0], q0[:, :, 0],
                      w_exp_gate, w_exp_up, w_exp_down, l, t, d, tb)
        pending_ffn = (routed, shared, mod, g_post_ffn[l], tb)
    routed, shared, mod, g_post, tb = pending_ffn
    return _epilogue(x.reshape(t, d), routed, shared, mod, g_post, s, tb).reshape(b, s, d)
```

```python
import functools

import jax
import jax.numpy as jnp
from jax import lax
from jax.experimental import pallas as pl
from jax.experimental.pallas import tpu as pltpu

F32 = jnp.float32
BF16 = jnp.bfloat16

EPS = 1e-6
POOL_WINDOWS = (2, 4, 8, 16)
POOL_GROUP_DIM = 128
N_EXPERTS = 64
N_GROUPS = 8
GROUP_SIZE = 8
TOPK_GROUPS = 4
TOP_K = 8
ROUTED_SCALE = 2.5

LANES = 128
SUBLANES = 8
CONV_HALO = 8
POOL_HALO = 16
VMEM_LIMIT = 56 * 1024 * 1024

ADA_COLS = 2048
MIX_TILE = 512
MIX_CHAINS = 2
ROUTE_LANES = 512
MOE_BLOCK = 4096
MOE_CHUNK = 192
SCATTER_GROUP = 8
TOKEN_RADIX = 64.0
LIST_LEAD = 2
EPI_TILE = 512


def _silu(v):
    return v * jax.nn.sigmoid(v)


def _rms_scale(v):
    return lax.rsqrt(jnp.mean(v * v, axis=-1, keepdims=True) + EPS)


def _ada_kernel(c_ref, w_ref, b_ref, o_ref):
    cond = _silu(c_ref[...]).astype(BF16)
    o_ref[0] = jnp.dot(cond, w_ref[0].astype(BF16), preferred_element_type=F32) + b_ref[0]


def _ada_mod(c, w_ada, b_ada):
    depth, d, d6 = w_ada.shape
    b = c.shape[0]
    cols = ADA_COLS
    return pl.pallas_call(
        _ada_kernel,
        grid=(depth, d6 // cols),
        in_specs=[
            pl.BlockSpec((b, d), lambda l, n: (0, 0)),
            pl.BlockSpec((1, d, cols), lambda l, n: (l, 0, n)),
            pl.BlockSpec((1, 1, cols), lambda l, n: (l, 0, n)),
        ],
        out_specs=pl.BlockSpec((1, b, cols), lambda l, n: (l, 0, n)),
        out_shape=jax.ShapeDtypeStruct((depth, b, d6), F32),
    )(c, w_ada, b_ada.reshape(depth, 1, d6))


def _ffn_residual(x, routed_ref, shared, gt2, g_post, row0=0):
    rows, d = x.shape
    n_chunk = d // LANES
    routed = jnp.concatenate(
        [routed_ref[0, pl.ds(row0 * n_chunk + c, rows, stride=n_chunk), :]
         for c in range(n_chunk)], axis=1)
    y = routed + shared
    return x + gt2 * (y * _rms_scale(y) * g_post)


def _mixer_kernel(*refs, after_ffn):
    if after_ffn:
        (x_ref, routed_ref, shared_ref, modp_ref, gpp_ref), refs = refs[:5], refs[5:]
    else:
        x_ref, refs = refs[0], refs[1:]
    (mod_ref, gpre_ref, gpost_ref, win_ref, convw_ref, wco_ref, wpg_ref, pscale_ref, wpp_ref,
     wo_ref, o_ref, uext_ref, pext_ref) = refs
    j = pl.program_id(1)
    tm, d = x_ref.shape[1], x_ref.shape[2]
    d_pool = pext_ref.shape[1]

    @pl.when(j == 0)
    def _():
        uext_ref[0:CONV_HALO, :] = jnp.zeros((CONV_HALO, d), F32)
        pext_ref[0:POOL_HALO, :] = jnp.zeros((POOL_HALO, d_pool), F32)

    mod = mod_ref[0]
    sh1, sc1, gt1 = mod[0:1], mod[1:2], mod[2:3]
    cw = convw_ref[...]
    ts = tm // MIX_CHAINS
    for ch in range(MIX_CHAINS):
        r0 = ch * ts
        x = x_ref[0, r0:r0 + ts, :]
        if after_ffn:
            x = _ffn_residual(x, routed_ref, shared_ref[0, r0:r0 + ts, :], modp_ref[0][5:6],
                              gpp_ref[...], row0=r0)
        h = x * _rms_scale(x) * gpre_ref[...] * (1.0 + sc1) + sh1
        hb = h.astype(BF16)

        def proj(lo, hi):
            return jnp.dot(hb, win_ref[:, lo:hi], preferred_element_type=F32)

        u = proj(d, 2 * d) * proj(2 * d, 3 * d)
        u0 = CONV_HALO + r0
        uext_ref[u0:u0 + ts, :] = u
        conv = (cw[2:3] * u
                + cw[1:2] * uext_ref[u0 - 1:u0 - 1 + ts, :]
                + cw[0:1] * uext_ref[u0 - 2:u0 - 2 + ts, :])
        y_conv = jnp.dot((proj(0, d) * conv).astype(BF16), wco_ref[...],
                         preferred_element_type=F32)

        up = proj(3 * d, 3 * d + d_pool)
        p0 = POOL_HALO + r0
        pext_ref[p0:p0 + ts, :] = up
        pos = j * tm + r0 + lax.broadcasted_iota(jnp.int32, (ts, 1), 0)
        zs = []
        for g, w in enumerate(POOL_WINDOWS):
            c0 = g * POOL_GROUP_DIM
            ug = up[:, c0:c0 + POOL_GROUP_DIM]
            acc = ug
            for k in range(1, w):
                acc = acc + pext_ref[p0 - k:p0 - k + ts, c0:c0 + POOL_GROUP_DIM]
            inv_cnt = 1.0 / jnp.minimum(pos + 1, w).astype(F32)
            diff = acc * inv_cnt - ug
            zs.append(jnp.dot(diff.astype(BF16), wpg_ref[g], preferred_element_type=F32))
        z = jnp.concatenate(zs, axis=1) * pscale_ref[...]
        y_pool = jnp.dot(z.astype(BF16), wpp_ref[...], preferred_element_type=F32)

        a_conv = proj(3 * d + d_pool, 4 * d + d_pool)
        a_pool = proj(4 * d + d_pool, 5 * d + d_pool)
        merged = jax.nn.sigmoid(a_conv) * y_conv + jax.nn.sigmoid(a_pool) * y_pool
        y = jnp.dot(merged.astype(BF16), wo_ref[...], preferred_element_type=F32)
        o_ref[0, r0:r0 + ts, :] = x + gt1 * (y * _rms_scale(y) * gpost_ref[...])

    uext_ref[0:CONV_HALO, :] = uext_ref[tm:tm + CONV_HALO, :]
    pext_ref[0:POOL_HALO, :] = pext_ref[tm:tm + POOL_HALO, :]


def _token_mixer(x, pending_ffn, mod, g_pre, g_post, w_in, conv_w, w_conv_out, w_pool_group,
                 pool_scale, w_pool_proj, w_o):
    b, s, d = x.shape
    d_in = w_in.shape[1]
    d_pool = w_pool_proj.shape[0]
    n_chunk = d // LANES
    tm = min(MIX_TILE, s)
    const2 = lambda bi, j: (0, 0)
    const3 = lambda bi, j: (0, 0, 0)
    rows = pl.BlockSpec((1, tm, d), lambda bi, j: (bi, j, 0))
    per_batch = pl.BlockSpec((1, 6, d), lambda bi, j: (bi, 0, 0))
    prev_specs, prev_args = [], []
    if pending_ffn is not None:
        routed, shared, mod_p, g_post_p, tb = pending_ffn
        tiles = tb // tm
        prev_specs = [
            pl.BlockSpec((1, tm * n_chunk, LANES),
                         lambda bi, j: ((bi * (s // tm) + j) // tiles, (bi * (s // tm) + j) % tiles, 0)),
            rows, per_batch, pl.BlockSpec((1, d), const2)]
        prev_args = [routed, shared.reshape(b, s, d), mod_p, g_post_p.reshape(1, d)]
    return pl.pallas_call(
        functools.partial(_mixer_kernel, after_ffn=pending_ffn is not None),
        grid=(b, s // tm),
        in_specs=[
            rows,
            *prev_specs,
            per_batch,
            pl.BlockSpec((1, d), const2),
            pl.BlockSpec((1, d), const2),
            pl.BlockSpec((d, d_in), const2),
            pl.BlockSpec((3, d), const2),
            pl.BlockSpec((d, d), const2),
            pl.BlockSpec(w_pool_group.shape, const3),
            pl.BlockSpec((1, d_pool), const2),
            pl.BlockSpec((d_pool, d), const2),
            pl.BlockSpec((d, d), const2),
        ],
        out_specs=pl.BlockSpec((1, tm, d), lambda bi, j: (bi, j, 0)),
        out_shape=jax.ShapeDtypeStruct(x.shape, F32),
        scratch_shapes=[
            pltpu.VMEM((CONV_HALO + tm, d), F32),
            pltpu.VMEM((POOL_HALO + tm, d_pool), F32),
        ],
        compiler_params=pltpu.CompilerParams(
            dimension_semantics=("arbitrary", "arbitrary"), vmem_limit_bytes=VMEM_LIMIT),
    )(x, *prev_args, mod, g_pre.reshape(1, d), g_post.reshape(1, d), w_in.astype(BF16), conv_w,
      w_conv_out.astype(BF16), w_pool_group.astype(BF16), pool_scale.reshape(1, d_pool),
      w_pool_proj.astype(BF16), w_o.astype(BF16))


def _wins(other, v, tie_i):
    return jnp.where(other > v, 1, 0) + jnp.where(other == v, tie_i, 0)


def _route_kernel(x_ref, mod_ref, g_ref, wrt_ref, bias_ref, wsg_ref, wsu_ref, wsd_ref,
                  h2g_ref, shared_ref, rowl_ref, wl_ref, nch_ref, q0_ref,
                  lg_ref, pre_ref, rho_ref, wd_ref):
    i = pl.program_id(1)
    lc, d = x_ref.shape
    n_sub = lg_ref.shape[0]
    n_chunk = d // LANES

    @pl.when(i < n_sub)
    def _():
        mod = mod_ref[0]
        sh2, sc2 = mod[3:4], mod[4:5]

        @pl.when(i == 0)
        def _():
            spare = n_sub * lc * n_chunk
            h2g_ref[0, spare:spare + n_chunk, :] = jnp.zeros((n_chunk, LANES), F32)

        x = x_ref[...]
        h = x * _rms_scale(x) * g_ref[...] * (1.0 + sc2) + sh2
        for c in range(n_chunk):
            h2g_ref[0, pl.ds(i * lc * n_chunk + c, lc, stride=n_chunk), :] = (
                h[:, c * LANES:(c + 1) * LANES])
        hb = h.astype(BF16)
        act = (_silu(jnp.dot(hb, wsg_ref[...], preferred_element_type=F32))
               * jnp.dot(hb, wsu_ref[...], preferred_element_type=F32))
        shared_ref[...] = jnp.dot(act.astype(BF16), wsd_ref[...], preferred_element_type=F32)
        lg_ref[i] = lax.dot_general(
            wrt_ref[...], hb, (((1,), (1,)), ((), ())), preferred_element_type=F32)

    @pl.when(i == n_sub)
    def _():
        _route_plan(bias_ref, rowl_ref, wl_ref, nch_ref, q0_ref, lg_ref, pre_ref, rho_ref, wd_ref,
                    lc, n_chunk)


def _route_plan(bias_ref, rowl_ref, wl_ref, nch_ref, q0_ref, lg_ref, pre_ref, rho_ref, wd_ref,
                lc, n_chunk):
    n_sub = lg_ref.shape[0]
    tb = n_sub * lc
    gidx = lax.broadcasted_iota(jnp.int32, (N_GROUPS, lc), 0)
    tie = [None] + [jnp.where(gidx >= r, 1, 0) for r in range(1, N_GROUPS)]
    tri = (lax.broadcasted_iota(jnp.int32, (lc, lc), 0)
           < lax.broadcasted_iota(jnp.int32, (lc, lc), 1)).astype(BF16)
    carry = jnp.zeros((N_EXPERTS, 1), F32)
    neg_inf = jnp.float32(-jnp.inf)
    for ci in range(tb // lc):
        c0 = ci * lc
        s_all = jax.nn.sigmoid(lg_ref[ci])
        aff = [s_all[GROUP_SIZE * jj:GROUP_SIZE * (jj + 1), :] for jj in range(GROUP_SIZE)]
        sel = [aff[jj] + bias_ref[GROUP_SIZE * jj:GROUP_SIZE * (jj + 1), :]
               for jj in range(GROUP_SIZE)]
        m1, m2 = sel[0], jnp.full_like(sel[0], neg_inf)
        for jj in range(1, GROUP_SIZE):
            m2 = jnp.maximum(m2, jnp.minimum(m1, sel[jj]))
            m1 = jnp.maximum(m1, sel[jj])
        gs = m1 + m2
        beaten = jnp.zeros((N_GROUPS, lc), jnp.int32)
        for r in range(1, N_GROUPS):
            other = pltpu.roll(gs, r, axis=0)
            beaten = beaten + _wins(other, gs, tie[r])
        gmask = beaten < TOPK_GROUPS
        masked = [jnp.where(gmask, sel[jj], neg_inf) for jj in range(GROUP_SIZE)]
        rolled = [[masked[jj]] + [pltpu.roll(masked[jj], r, axis=0) for r in range(1, N_GROUPS)]
                  for jj in range(GROUP_SIZE)]
        rho = []
        for jj in range(GROUP_SIZE):
            v = masked[jj]
            cnt = jnp.zeros((N_GROUPS, lc), jnp.int32)
            for j2 in range(GROUP_SIZE):
                for r in range(N_GROUPS):
                    if r == 0 and j2 == jj:
                        continue
                    other = rolled[j2][r]
                    if r == 0:
                        wins = (other >= v) if j2 < jj else (other > v)
                        cnt = cnt + jnp.where(wins, 1, 0)
                    else:
                        cnt = cnt + _wins(other, v, tie[r])
            rho.append(cnt)
        chosen = [rho[jj] < TOP_K for jj in range(GROUP_SIZE)]
        ssum = jnp.zeros((N_GROUPS, lc), F32)
        for jj in range(GROUP_SIZE):
            ssum = ssum + jnp.where(chosen[jj], aff[jj], 0.0)
        ssum = jnp.sum(ssum, axis=0, keepdims=True)
        wdense = [jnp.where(chosen[jj], aff[jj] / ssum * ROUTED_SCALE, 0.0)
                  for jj in range(GROUP_SIZE)]
        chosen_f = jnp.concatenate([c.astype(F32) for c in chosen], axis=0)
        prefix = jnp.dot(chosen_f.astype(BF16), tri, preferred_element_type=F32) + carry
        carry = carry + jnp.sum(chosen_f, axis=1, keepdims=True)
        pre_ref[:, c0:c0 + lc] = prefix
        rho_ref[:, c0:c0 + lc] = jnp.concatenate(rho, axis=0)
        wd_ref[:, c0:c0 + lc] = jnp.concatenate(wdense, axis=0)

    m = MOE_CHUNK
    nch_b = jnp.broadcast_to(jnp.floor((carry + (m - 0.5)) * (1.0 / m)), (N_EXPERTS, LANES))
    lower = (lax.broadcasted_iota(jnp.int32, (N_EXPERTS, N_EXPERTS), 1)
             < lax.broadcasted_iota(jnp.int32, (N_EXPERTS, N_EXPERTS), 0)).astype(F32)
    q0_b = jnp.dot(lower, nch_b, preferred_element_type=F32, precision=lax.Precision.HIGHEST)
    nch_ref[0] = nch_b.astype(jnp.int32)
    q0_ref[0] = q0_b.astype(jnp.int32)
    offs_col = q0_b[:, 0:1] * m

    nq = rowl_ref.shape[1]
    iota_q = lax.broadcasted_iota(jnp.int32, (nq, lc), 0).astype(F32)
    iota_r = lax.broadcasted_iota(jnp.int32, (m, lc), 0).astype(F32)
    lists = jnp.zeros((nq, 5 * m), F32)
    for ci in range(tb // lc):
        c0 = ci * lc
        dest_dense = pre_ref[:, c0:c0 + lc] + offs_col
        rho_c = rho_ref[:, c0:c0 + lc]
        w_c = wd_ref[:, c0:c0 + lc]
        tokv = (c0 + 1 + lax.broadcasted_iota(jnp.int32, (1, lc), 1)).astype(F32)
        tok_hi = jnp.floor(tokv * (1.0 / TOKEN_RADIX))
        tok_lo = tokv - TOKEN_RADIX * tok_hi
        for k in range(TOP_K):
            hit = rho_c == k
            dk = jnp.sum(jnp.where(hit, dest_dense, 0.0), axis=0, keepdims=True)
            wk = jnp.sum(jnp.where(hit, w_c, 0.0), axis=0, keepdims=True)
            qk = jnp.floor((dk + 0.5) * (1.0 / m))
            rk = dk - m * qk
            w_hi = wk.astype(BF16).astype(F32)
            w_mid = (wk - w_hi).astype(BF16).astype(F32)
            w_lo = wk - w_hi - w_mid
            onehot_q = jnp.where(iota_q == qk + LIST_LEAD, 1.0, 0.0).astype(BF16)
            rmask = iota_r == rk
            vals = jnp.concatenate(
                [jnp.where(rmask, piece, 0.0) for piece in (tok_hi, tok_lo, w_hi, w_mid, w_lo)],
                axis=0).astype(BF16)
            lists = lists + lax.dot_general(onehot_q, vals, (((1,), (1,)), ((), ())),
                                            preferred_element_type=F32)
    tok = lists[:, 0:m] * TOKEN_RADIX + lists[:, m:2 * m]
    tile = jnp.where(tok == 0.0, float(tb), tok - 1.0)
    rowl_ref[0] = (tile * n_chunk).astype(jnp.int32)
    wl_ref[0] = (lists[:, 2 * m:3 * m] + lists[:, 3 * m:4 * m]) + lists[:, 4 * m:5 * m]


def _route(x1, mod, g_pre, w_router, router_bias, w_sh_gate, w_sh_up, w_sh_down, seq, tb):
    t, d = x1.shape
    nb = t // tb
    n_chunk = d // LANES
    d_sh = w_sh_gate.shape[1]
    perm = jnp.arange(N_EXPERTS).reshape(N_GROUPS, GROUP_SIZE).T.reshape(-1)
    wrt = w_router.T[perm].astype(BF16)
    bias = router_bias[perm].reshape(N_EXPERTS, 1)
    nq = _num_list_rows(tb)
    lc = min(ROUTE_LANES, tb)
    n_sub = tb // lc
    assert tb <= 256 * TOKEN_RADIX and TOKEN_RADIX <= 256, "token digits must be exact in bf16"
    const2 = lambda bi, i: (0, 0)
    per_block = lambda bi, i: (bi, 0, 0)
    sub_tile = lambda bi, i: (bi * n_sub + jnp.minimum(i, n_sub - 1), 0)
    outs = pl.pallas_call(
        _route_kernel,
        grid=(nb, n_sub + 1),
        in_specs=[
            pl.BlockSpec((lc, d), sub_tile),
            pl.BlockSpec((1, 6, d), lambda bi, i: (bi * tb // seq, 0, 0)),
            pl.BlockSpec((1, d), const2),
            pl.BlockSpec((N_EXPERTS, d), const2),
            pl.BlockSpec((N_EXPERTS, 1), const2),
            pl.BlockSpec((d, d_sh), const2),
            pl.BlockSpec((d, d_sh), const2),
            pl.BlockSpec((d_sh, d), const2),
        ],
        out_specs=[
            pl.BlockSpec((1, (tb + 1) * n_chunk, LANES), per_block),
            pl.BlockSpec((lc, d), sub_tile),
            pl.BlockSpec((1, nq, MOE_CHUNK), per_block),
            pl.BlockSpec((1, nq, MOE_CHUNK), per_block),
            pl.BlockSpec((1, N_EXPERTS, LANES), per_block),
            pl.BlockSpec((1, N_EXPERTS, LANES), per_block),
        ],
        out_shape=[
            jax.ShapeDtypeStruct((nb, (tb + 1) * n_chunk, LANES), F32),
            jax.ShapeDtypeStruct((t, d), F32),
            jax.ShapeDtypeStruct((nb, nq, MOE_CHUNK), jnp.int32),
            jax.ShapeDtypeStruct((nb, nq, MOE_CHUNK), F32),
            jax.ShapeDtypeStruct((nb, N_EXPERTS, LANES), jnp.int32),
            jax.ShapeDtypeStruct((nb, N_EXPERTS, LANES), jnp.int32),
        ],
        scratch_shapes=[
            pltpu.VMEM((n_sub, N_EXPERTS, lc), F32),
            pltpu.VMEM((N_EXPERTS, tb), F32),
            pltpu.VMEM((N_EXPERTS, tb), jnp.int32),
            pltpu.VMEM((N_EXPERTS, tb), F32),
        ],
        compiler_params=pltpu.CompilerParams(
            dimension_semantics=("arbitrary", "arbitrary"), vmem_limit_bytes=VMEM_LIMIT),
    )(x1, mod, g_pre.reshape(1, d), wrt, bias, w_sh_gate.astype(BF16), w_sh_up.astype(BF16),
      w_sh_down.astype(BF16))
    return outs


def _num_list_rows(tb):
    rows = -(-TOP_K * tb // MOE_CHUNK) + N_EXPERTS + 2 * LIST_LEAD
    return -(-rows // SUBLANES) * SUBLANES


def _moe_kernel(nch_sm, q0_sm, h2g_hbm, rowl_hbm, wl_ref, wg_ref, wu_ref, wd_ref, o_hbm,
                h2g_ref, acc_ref, xt0_ref, xt1_ref, yt0_ref, yt1_ref, act0_ref, act1_ref,
                wgb_ref, wub_ref, wdb_ref, rowl_sm, experts_sm, sems):
    bi = pl.program_id(0)
    r = pl.program_id(1)
    n_chunk = xt0_ref.shape[0] // (MOE_CHUNK + SUBLANES)
    m = MOE_CHUNK
    stride = m + SUBLANES
    eye = (lax.broadcasted_iota(jnp.int32, (m, m), 0)
           == lax.broadcasted_iota(jnp.int32, (m, m), 1))
    xts, yts, acts = (xt0_ref, xt1_ref), (yt0_ref, yt1_ref), (act0_ref, act1_ref)

    def gather(lrow, xt_ref):
        base = lrow * m
        for mi in range(m):
            row = pl.multiple_of(rowl_sm[base + mi], n_chunk)
            xt_ref[pl.ds(mi, n_chunk, stride=stride), :] = h2g_ref[pl.ds(row, n_chunk), :]

    def scatter(lrow, yt_ref):
        base = lrow * m
        for g0 in range(0, m, SCATTER_GROUP):
            rows, vals = [], []
            for mi in range(g0, g0 + SCATTER_GROUP):
                row = pl.multiple_of(rowl_sm[base + mi], n_chunk)
                rows.append(row)
                vals.append(acc_ref[pl.ds(row, n_chunk), :]
                            + yt_ref[pl.ds(mi, n_chunk, stride=stride), :])
            for row, val in zip(rows, vals):
                acc_ref[pl.ds(row, n_chunk), :] = val

    def gate_up(xt_ref, act_ref):
        xs = jnp.concatenate([xt_ref[c * stride:c * stride + m, :] for c in range(n_chunk)],
                             axis=1).astype(BF16)
        act = (_silu(jnp.dot(xs, wgb_ref[...], preferred_element_type=F32))
               * jnp.dot(xs, wub_ref[...], preferred_element_type=F32))
        act_ref[...] = act.astype(BF16)

    def down(lrow, act_ref, slot, yt_ref):
        y = jnp.dot(act_ref[...], wdb_ref[slot], preferred_element_type=F32)
        w_row = wl_ref[0, pl.ds(lrow, 1), :]
        w_col = jnp.sum(jnp.where(eye, w_row, 0.0), axis=1, keepdims=True)
        y = y * w_col
        for c in range(n_chunk):
            yt_ref[c * stride:c * stride + m, :] = y[:, c * LANES:(c + 1) * LANES]

    def step(q, par, down_slot):
        cur, oth = par, 1 - par
        gather(q + 1 + LIST_LEAD, xts[oth])
        gate_up(xts[cur], acts[cur])
        down(q - 1 + LIST_LEAD, acts[oth], down_slot, yts[oth])
        scatter(q - 2 + LIST_LEAD, yts[cur])

    def by_parity(q, fn):
        for par in (0, 1):
            @pl.when((q & 1) == par)
            def _():
                fn(par)

    def block_loads():
        return (pltpu.make_async_copy(rowl_hbm.at[bi], rowl_sm, sems.at[0]),
                pltpu.make_async_copy(h2g_hbm.at[bi], h2g_ref, sems.at[1]))

    def block_store():
        return pltpu.make_async_copy(acc_ref, o_hbm.at[bi], sems.at[2])

    @pl.when(r == 0)
    def _():
        for cp in block_loads():
            cp.start()
        acc_ref[...] = jnp.zeros(acc_ref.shape, F32)
        act1_ref[...] = jnp.zeros(act1_ref.shape, BF16)
        yt0_ref[...] = jnp.zeros(yt0_ref.shape, F32)
        wdb_ref[...] = jnp.zeros(wdb_ref.shape, BF16)
        experts_sm[0] = 0
        for cp in block_loads():
            cp.wait()
        gather(LIST_LEAD, xt0_ref)

    n_chunks = nch_sm[bi * N_EXPERTS + r]
    q_first = q0_sm[bi * N_EXPERTS + r]
    slot = experts_sm[0] & 1

    @pl.when(n_chunks > 0)
    def _():
        wgb_ref[...] = wg_ref[0, 0].astype(BF16)
        wub_ref[...] = wu_ref[0, 0].astype(BF16)
        wdb_ref[slot] = wd_ref[0, 0].astype(BF16)
        by_parity(q_first, lambda par: step(q_first, par, 1 - slot))
        experts_sm[0] = experts_sm[0] + 1

    def chunk(ci, carry):
        q = q_first + ci
        by_parity(q, lambda par: step(q, par, slot))
        return carry

    lax.fori_loop(1, n_chunks, chunk, 0)

    @pl.when(r == N_EXPERTS - 1)
    def _():
        q_last = q_first + n_chunks - 1
        last_slot = (experts_sm[0] - 1) & 1

        def drain(par):
            scatter(q_last - 1 + LIST_LEAD, yts[1 - par])
            down(q_last + LIST_LEAD, acts[par], last_slot, yts[par])
            scatter(q_last + LIST_LEAD, yts[par])

        by_parity(q_last, drain)

        block_store().start()
        block_store().wait()


def _moe(h2g, rowl, wl, nch, q0, w_gate, w_up, w_down, layer, t, d, tb):
    nb = t // tb
    n_chunk = d // LANES
    d_e = w_gate.shape[3]
    m = MOE_CHUNK
    stride = m + SUBLANES
    nq = _num_list_rows(tb)

    def expert_of(r):
        return (r % N_GROUPS) * GROUP_SIZE + r // N_GROUPS

    cur_w = lambda bi, r, c, o: (layer, expert_of(r), 0, 0)
    per_block = lambda bi, r, c, o: (bi, 0, 0)
    staging = pltpu.VMEM((n_chunk * stride, LANES), F32)
    grid_spec = pltpu.PrefetchScalarGridSpec(
        num_scalar_prefetch=2,
        grid=(nb, N_EXPERTS),
        in_specs=[
            pl.BlockSpec(memory_space=pl.ANY),
            pl.BlockSpec(memory_space=pl.ANY),
            pl.BlockSpec((1, nq, m), per_block),
            pl.BlockSpec((1, 1, d, d_e), cur_w),
            pl.BlockSpec((1, 1, d, d_e), cur_w),
            pl.BlockSpec((1, 1, d_e, d), cur_w),
        ],
        out_specs=pl.BlockSpec(memory_space=pl.ANY),
        scratch_shapes=[
            pltpu.VMEM(((tb + 1) * n_chunk, LANES), F32),
            pltpu.VMEM(((tb + 1) * n_chunk, LANES), F32),
            staging, staging, staging, staging,
            pltpu.VMEM((m, d_e), BF16),
            pltpu.VMEM((m, d_e), BF16),
            pltpu.VMEM((d, d_e), BF16),
            pltpu.VMEM((d, d_e), BF16),
            pltpu.VMEM((2, d_e, d), BF16),
            pltpu.SMEM((nq * m,), jnp.int32),
            pltpu.SMEM((1,), jnp.int32),
            pltpu.SemaphoreType.DMA((3,)),
        ],
    )
    return pl.pallas_call(
        _moe_kernel,
        grid_spec=grid_spec,
        out_shape=jax.ShapeDtypeStruct((nb, (tb + 1) * n_chunk, LANES), F32),
        compiler_params=pltpu.CompilerParams(
            dimension_semantics=("arbitrary", "arbitrary"), vmem_limit_bytes=VMEM_LIMIT),
    )(nch.reshape(-1), q0.reshape(-1), h2g, rowl.reshape(nb, nq * m), wl, w_gate, w_up, w_down)


def _epilogue_kernel(x_ref, routed_ref, shared_ref, mod_ref, g_ref, o_ref):
    o_ref[...] = _ffn_residual(x_ref[...], routed_ref, shared_ref[...], mod_ref[0][5:6], g_ref[...])


def _epilogue(x1, routed, shared, mod, g_post, seq, tb):
    t, d = x1.shape
    n_chunk = d // LANES
    te = min(EPI_TILE, tb)
    n_sub = tb // te
    row = pl.BlockSpec((te, d), lambda bi, i: (bi * n_sub + i, 0))
    return pl.pallas_call(
        _epilogue_kernel,
        grid=(t // tb, n_sub),
        in_specs=[row,
                  pl.BlockSpec((1, te * n_chunk, LANES), lambda bi, i: (bi, i, 0)),
                  row,
                  pl.BlockSpec((1, 6, d), lambda bi, i: (bi * tb // seq, 0, 0)),
                  pl.BlockSpec((1, d), lambda bi, i: (0, 0))],
        out_specs=row,
        out_shape=jax.ShapeDtypeStruct((t, d), F32),
    )(x1, routed, shared, mod, g_post.reshape(1, d))


def kernel(x, c, w_ada, b_ada, g_pre_mix, g_post_mix, g_pre_ffn, g_post_ffn, w_in, conv_w,
           w_conv_out, w_pool_group, pool_scale, w_pool_proj, w_o, w_router, router_bias,
           w_exp_gate, w_exp_up, w_exp_down, w_sh_gate, w_sh_up, w_sh_down):
    b, s, d = x.shape
    depth = w_ada.shape[0]
    t = b * s
    tb = min(MOE_BLOCK, s)
    mods = _ada_mod(c, w_ada, b_ada).reshape(depth, b, 6, d)
    pending_ffn = None
    for l in range(depth):
        mod = mods[l]
        x = _token_mixer(x, pending_ffn, mod, g_pre_mix[l], g_post_mix[l], w_in[l], conv_w[l],
                         w_conv_out[l], w_pool_group[l], pool_scale[l], w_pool_proj[l], w_o[l])
        x1 = x.reshape(t, d)
        h2g, shared, rowl, wl, nch, q0 = _route(
            x1, mod, g_pre_ffn[l], w_router[l], router_bias[l], w_sh_gate[l], w_sh_up[l],
            w_sh_down[l], s, tb)
        routed = _moe(h2g, rowl, wl, nch[:, :, 0], q0[:, :, 0],
                      w_exp_gate, w_exp_up, w_exp_down, l, t, d, tb)
        pending_ffn = (routed, shared, mod, g_post_ffn[l], tb)
    routed, shared, mod, g_post, tb = pending_ffn
    return _epilogue(x.reshape(t, d), routed, shared, mod, g_post, s, tb).reshape(b, s, d)
```

```python
import functools

import jax
import jax.numpy as jnp
from jax import lax
from jax.experimental import pallas as pl
from jax.experimental.pallas import tpu as pltpu

F32 = jnp.float32
BF16 = jnp.bfloat16

EPS = 1e-6
POOL_WINDOWS = (2, 4, 8, 16)
POOL_GROUP_DIM = 128
N_EXPERTS = 64
N_GROUPS = 8
GROUP_SIZE = 8
TOPK_GROUPS = 4
TOP_K = 8
ROUTED_SCALE = 2.5

LANES = 128
SUBLANES = 8
CONV_HALO = 8
POOL_HALO = 16
VMEM_LIMIT = 56 * 1024 * 1024

ADA_COLS = 2048
MIX_TILE = 512
MIX_CHAINS = 2
ROUTE_LANES = 512
MOE_BLOCK = 4096
MOE_CHUNK = 192
SCATTER_GROUP = 8
TOKEN_RADIX = 64.0
LIST_LEAD = 2
EPI_TILE = 512


def _silu(v):
    return v * jax.nn.sigmoid(v)


def _rms_scale(v):
    return lax.rsqrt(jnp.mean(v * v, axis=-1, keepdims=True) + EPS)


def _ada_kernel(c_ref, w_ref, b_ref, o_ref):
    cond = _silu(c_ref[...]).astype(BF16)
    o_ref[0] = jnp.dot(cond, w_ref[0].astype(BF16), preferred_element_type=F32) + b_ref[0]


def _ada_mod(c, w_ada, b_ada):
    depth, d, d6 = w_ada.shape
    b = c.shape[0]
    cols = ADA_COLS
    return pl.pallas_call(
        _ada_kernel,
        grid=(depth, d6 // cols),
        in_specs=[
            pl.BlockSpec((b, d), lambda l, n: (0, 0)),
            pl.BlockSpec((1, d, cols), lambda l, n: (l, 0, n)),
            pl.BlockSpec((1, 1, cols), lambda l, n: (l, 0, n)),
        ],
        out_specs=pl.BlockSpec((1, b, cols), lambda l, n: (l, 0, n)),
        out_shape=jax.ShapeDtypeStruct((depth, b, d6), F32),
    )(c, w_ada, b_ada.reshape(depth, 1, d6))


def _ffn_residual(x, routed_ref, shared, gt2, g_post, row0=0):
    rows, d = x.shape
    n_chunk = d // LANES
    routed = jnp.concatenate(
        [routed_ref[0, pl.ds(row0 * n_chunk + c, rows, stride=n_chunk), :]
         for c in range(n_chunk)], axis=1)
    y = routed + shared
    return x + gt2 * (y * _rms_scale(y) * g_post)


def _mixer_kernel(*refs, after_ffn):
    if after_ffn:
        (x_ref, routed_ref, shared_ref, modp_ref, gpp_ref), refs = refs[:5], refs[5:]
    else:
        x_ref, refs = refs[0], refs[1:]
    (mod_ref, gpre_ref, gpost_ref, win_ref, convw_ref, wco_ref, wpg_ref, pscale_ref, wpp_ref,
     wo_ref, o_ref, uext_ref, pext_ref) = refs
    j = pl.program_id(1)
    tm, d = x_ref.shape[1], x_ref.shape[2]
    d_pool = pext_ref.shape[1]

    @pl.when(j == 0)
    def _():
        uext_ref[0:CONV_HALO, :] = jnp.zeros((CONV_HALO, d), F32)
        pext_ref[0:POOL_HALO, :] = jnp.zeros((POOL_HALO, d_pool), F32)

    mod = mod_ref[0]
    sh1, sc1, gt1 = mod[0:1], mod[1:2], mod[2:3]
    cw = convw_ref[...]
    ts = tm // MIX_CHAINS
    for ch in range(MIX_CHAINS):
        r0 = ch * ts
        x = x_ref[0, r0:r0 + ts, :]
        if after_ffn:
            x = _ffn_residual(x, routed_ref, shared_ref[0, r0:r0 + ts, :], modp_ref[0][5:6],
                              gpp_ref[...], row0=r0)
        h = x * _rms_scale(x) * gpre_ref[...] * (1.0 + sc1) + sh1
        hb = h.astype(BF16)

        def proj(lo, hi):
            return jnp.dot(hb, win_ref[:, lo:hi], preferred_element_type=F32)

        u = proj(d, 2 * d) * proj(2 * d, 3 * d)
        u0 = CONV_HALO + r0
        uext_ref[u0:u0 + ts, :] = u
        conv = (cw[2:3] * u
                + cw[1:2] * uext_ref[u0 - 1:u0 - 1 + ts, :]
                + cw[0:1] * uext_ref[u0 - 2:u0 - 2 + ts, :])
        y_conv = jnp.dot((proj(0, d) * conv).astype(BF16), wco_ref[...],
                         preferred_element_type=F32)

        up = proj(3 * d, 3 * d + d_pool)
        p0 = POOL_HALO + r0
        pext_ref[p0:p0 + ts, :] = up
        pos = j * tm + r0 + lax.broadcasted_iota(jnp.int32, (ts, 1), 0)
        zs = []
        for g, w in enumerate(POOL_WINDOWS):
            c0 = g * POOL_GROUP_DIM
            ug = up[:, c0:c0 + POOL_GROUP_DIM]
            acc = ug
            for k in range(1, w):
                acc = acc + pext_ref[p0 - k:p0 - k + ts, c0:c0 + POOL_GROUP_DIM]
            inv_cnt = 1.0 / jnp.minimum(pos + 1, w).astype(F32)
            diff = acc * inv_cnt - ug
            zs.append(jnp.dot(diff.astype(BF16), wpg_ref[g], preferred_element_type=F32))
        z = jnp.concatenate(zs, axis=1) * pscale_ref[...]
        y_pool = jnp.dot(z.astype(BF16), wpp_ref[...], preferred_element_type=F32)

        a_conv = proj(3 * d + d_pool, 4 * d + d_pool)
        a_pool = proj(4 * d + d_pool, 5 * d + d_pool)
        merged = jax.nn.sigmoid(a_conv) * y_conv + jax.nn.sigmoid(a_pool) * y_pool
        y = jnp.dot(merged.astype(BF16), wo_ref[...], preferred_element_type=F32)
        o_ref[0, r0:r0 + ts, :] = x + gt1 * (y * _rms_scale(y) * gpost_ref[...])

    uext_ref[0:CONV_HALO, :] = uext_ref[tm:tm + CONV_HALO, :]
    pext_ref[0:POOL_HALO, :] = pext_ref[tm:tm + POOL_HALO, :]


def _token_mixer(x, pending_ffn, mod, g_pre, g_post, w_in, conv_w, w_conv_out, w_pool_group,
                 pool_scale, w_pool_proj, w_o):
    b, s, d = x.shape
    d_in = w_in.shape[1]
    d_pool = w_pool_proj.shape[0]
    n_chunk = d // LANES
    tm = min(MIX_TILE, s)
    const2 = lambda bi, j: (0, 0)
    const3 = lambda bi, j: (0, 0, 0)
    rows = pl.BlockSpec((1, tm, d), lambda bi, j: (bi, j, 0))
    per_batch = pl.BlockSpec((1, 6, d), lambda bi, j: (bi, 0, 0))
    prev_specs, prev_args = [], []
    if pending_ffn is not None:
        routed, shared, mod_p, g_post_p, tb = pending_ffn
        tiles = tb // tm
        prev_specs = [
            pl.BlockSpec((1, tm * n_chunk, LANES),
                         lambda bi, j: ((bi * (s // tm) + j) // tiles, (bi * (s // tm) + j) % tiles, 0)),
            rows, per_batch, pl.BlockSpec((1, d), const2)]
        prev_args = [routed, shared.reshape(b, s, d), mod_p, g_post_p.reshape(1, d)]
    return pl.pallas_call(
        functools.partial(_mixer_kernel, after_ffn=pending_ffn is not None),
        grid=(b, s // tm),
        in_specs=[
            rows,
            *prev_specs,
            per_batch,
            pl.BlockSpec((1, d), const2),
            pl.BlockSpec((1, d), const2),
            pl.BlockSpec((d, d_in), const2),
            pl.BlockSpec((3, d), const2),
            pl.BlockSpec((d, d), const2),
            pl.BlockSpec(w_pool_group.shape, const3),
            pl.BlockSpec((1, d_pool), const2),
            pl.BlockSpec((d_pool, d), const2),
            pl.BlockSpec((d, d), const2),
        ],
        out_specs=pl.BlockSpec((1, tm, d), lambda bi, j: (bi, j, 0)),
        out_shape=jax.ShapeDtypeStruct(x.shape, F32),
        scratch_shapes=[
            pltpu.VMEM((CONV_HALO + tm, d), F32),
            pltpu.VMEM((POOL_HALO + tm, d_pool), F32),
        ],
        compiler_params=pltpu.CompilerParams(
            dimension_semantics=("arbitrary", "arbitrary"), vmem_limit_bytes=VMEM_LIMIT),
    )(x, *prev_args, mod, g_pre.reshape(1, d), g_post.reshape(1, d), w_in.astype(BF16), conv_w,
      w_conv_out.astype(BF16), w_pool_group.astype(BF16), pool_scale.reshape(1, d_pool),
      w_pool_proj.astype(BF16), w_o.astype(BF16))


def _wins(other, v, tie_i):
    return jnp.where(other > v, 1, 0) + jnp.where(other == v, tie_i, 0)


def _route_kernel(x_ref, mod_ref, g_ref, wrt_ref, bias_ref, wsg_ref, wsu_ref, wsd_ref,
                  h2g_ref, shared_ref, rowl_ref, wl_ref, nch_ref, q0_ref,
                  lg_ref, pre_ref, rho_ref, wd_ref):
    i = pl.program_id(1)
    lc, d = x_ref.shape
    n_sub = lg_ref.shape[0]
    n_chunk = d // LANES

    @pl.when(i < n_sub)
    def _():
        mod = mod_ref[0]
        sh2, sc2 = mod[3:4], mod[4:5]

        @pl.when(i == 0)
        def _():
            spare = n_sub * lc * n_chunk
            h2g_ref[0, spare:spare + n_chunk, :] = jnp.zeros((n_chunk, LANES), F32)

        x = x_ref[...]
        h = x * _rms_scale(x) * g_ref[...] * (1.0 + sc2) + sh2
        for c in range(n_chunk):
            h2g_ref[0, pl.ds(i * lc * n_chunk + c, lc, stride=n_chunk), :] = (
                h[:, c * LANES:(c + 1) * LANES])
        hb = h.astype(BF16)
        act = (_silu(jnp.dot(hb, wsg_ref[...], preferred_element_type=F32))
               * jnp.dot(hb, wsu_ref[...], preferred_element_type=F32))
        shared_ref[...] = jnp.dot(act.astype(BF16), wsd_ref[...], preferred_element_type=F32)
        lg_ref[i] = lax.dot_general(
            wrt_ref[...], hb, (((1,), (1,)), ((), ())), preferred_element_type=F32)

    @pl.when(i == n_sub)
    def _():
        _route_plan(bias_ref, rowl_ref, wl_ref, nch_ref, q0_ref, lg_ref, pre_ref, rho_ref, wd_ref,
                    lc, n_chunk)


def _route_plan(bias_ref, rowl_ref, wl_ref, nch_ref, q0_ref, lg_ref, pre_ref, rho_ref, wd_ref,
                lc, n_chunk):
    n_sub = lg_ref.shape[0]
    tb = n_sub * lc
    gidx = lax.broadcasted_iota(jnp.int32, (N_GROUPS, lc), 0)
    tie = [None] + [jnp.where(gidx >= r, 1, 0) for r in range(1, N_GROUPS)]
    tri = (lax.broadcasted_iota(jnp.int32, (lc, lc), 0)
           < lax.broadcasted_iota(jnp.int32, (lc, lc), 1)).astype(BF16)
    carry = jnp.zeros((N_EXPERTS, 1), F32)
    neg_inf = jnp.float32(-jnp.inf)
    for ci in range(tb // lc):
        c0 = ci * lc
        s_all = jax.nn.sigmoid(lg_ref[ci])
        aff = [s_all[GROUP_SIZE * jj:GROUP_SIZE * (jj + 1), :] for jj in range(GROUP_SIZE)]
        sel = [aff[jj] + bias_ref[GROUP_SIZE * jj:GROUP_SIZE * (jj + 1), :]
               for jj in range(GROUP_SIZE)]
        m1, m2 = sel[0], jnp.full_like(sel[0], neg_inf)
        for jj in range(1, GROUP_SIZE):
            m2 = jnp.maximum(m2, jnp.minimum(m1, sel[jj]))
            m1 = jnp.maximum(m1, sel[jj])
        gs = m1 + m2
        beaten = jnp.zeros((N_GROUPS, lc), jnp.int32)
        for r in range(1, N_GROUPS):
            other = pltpu.roll(gs, r, axis=0)
            beaten = beaten + _wins(other, gs, tie[r])
        gmask = beaten < TOPK_GROUPS
        masked = [jnp.where(gmask, sel[jj], neg_inf) for jj in range(GROUP_SIZE)]
        rolled = [[masked[jj]] + [pltpu.roll(masked[jj], r, axis=0) for r in range(1, N_GROUPS)]
                  for jj in range(GROUP_SIZE)]
        rho = []
        for jj in range(GROUP_SIZE):
            v = masked[jj]
            cnt = jnp.zeros((N_GROUPS, lc), jnp.int32)
            for j2 in range(GROUP_SIZE):
                for r in range(N_GROUPS):
                    if r == 0 and j2 == jj:
                        continue
                    other = rolled[j2][r]
                    if r == 0:
                        wins = (other >= v) if j2 < jj else (other > v)
                        cnt = cnt + jnp.where(wins, 1, 0)
                    else:
                        cnt = cnt + _wins(other, v, tie[r])
            rho.append(cnt)
        chosen = [rho[jj] < TOP_K for jj in range(GROUP_SIZE)]
        ssum = jnp.zeros((N_GROUPS, lc), F32)
        for jj in range(GROUP_SIZE):
            ssum = ssum + jnp.where(chosen[jj], aff[jj], 0.0)
        ssum = jnp.sum(ssum, axis=0, keepdims=True)
        wdense = [jnp.where(chosen[jj], aff[jj] / ssum * ROUTED_SCALE, 0.0)
                  for jj in range(GROUP_SIZE)]
        chosen_f = jnp.concatenate([c.astype(F32) for c in chosen], axis=0)
        prefix = jnp.dot(chosen_f.astype(BF16), tri, preferred_element_type=F32) + carry
        carry = carry + jnp.sum(chosen_f, axis=1, keepdims=True)
        pre_ref[:, c0:c0 + lc] = prefix
        rho_ref[:, c0:c0 + lc] = jnp.concatenate(rho, axis=0)
        wd_ref[:, c0:c0 + lc] = jnp.concatenate(wdense, axis=0)

    m = MOE_CHUNK
    nch_b = jnp.broadcast_to(jnp.floor((carry + (m - 0.5)) * (1.0 / m)), (N_EXPERTS, LANES))
    lower = (lax.broadcasted_iota(jnp.int32, (N_EXPERTS, N_EXPERTS), 1)
             < lax.broadcasted_iota(jnp.int32, (N_EXPERTS, N_EXPERTS), 0)).astype(F32)
    q0_b = jnp.dot(lower, nch_b, preferred_element_type=F32, precision=lax.Precision.HIGHEST)
    nch_ref[0] = nch_b.astype(jnp.int32)
    q0_ref[0] = q0_b.astype(jnp.int32)
    offs_col = q0_b[:, 0:1] * m

    nq = rowl_ref.shape[1]
    iota_q = lax.broadcasted_iota(jnp.int32, (nq, lc), 0).astype(F32)
    iota_r = lax.broadcasted_iota(jnp.int32, (m, lc), 0).astype(F32)
    lists = jnp.zeros((nq, 5 * m), F32)
    for ci in range(tb // lc):
        c0 = ci * lc
        dest_dense = pre_ref[:, c0:c0 + lc] + offs_col
        rho_c = rho_ref[:, c0:c0 + lc]
        w_c = wd_ref[:, c0:c0 + lc]
        tokv = (c0 + 1 + lax.broadcasted_iota(jnp.int32, (1, lc), 1)).astype(F32)
        tok_hi = jnp.floor(tokv * (1.0 / TOKEN_RADIX))
        tok_lo = tokv - TOKEN_RADIX * tok_hi
        for k in range(TOP_K):
            hit = rho_c == k
            dk = jnp.sum(jnp.where(hit, dest_dense, 0.0), axis=0, keepdims=True)
            wk = jnp.sum(jnp.where(hit, w_c, 0.0), axis=0, keepdims=True)
            qk = jnp.floor((dk + 0.5) * (1.0 / m))
            rk = dk - m * qk
            w_hi = wk.astype(BF16).astype(F32)
            w_mid = (wk - w_hi).astype(BF16).astype(F32)
            w_lo = wk - w_hi - w_mid
            onehot_q = jnp.where(iota_q == qk + LIST_LEAD, 1.0, 0.0).astype(BF16)
            rmask = iota_r == rk
            vals = jnp.concatenate(
                [jnp.where(rmask, piece, 0.0) for piece in (tok_hi, tok_lo, w_hi, w_mid, w_lo)],
                axis=0).astype(BF16)
            lists = lists + lax.dot_general(onehot_q, vals, (((1,), (1,)), ((), ())),
                                            preferred_element_type=F32)
    tok = lists[:, 0:m] * TOKEN_RADIX + lists[:, m:2 * m]
    tile = jnp.where(tok == 0.0, float(tb), tok - 1.0)
    rowl_ref[0] = (tile * n_chunk).astype(jnp.int32)
    wl_ref[0] = (lists[:, 2 * m:3 * m] + lists[:, 3 * m:4 * m]) + lists[:, 4 * m:5 * m]


def _route(x1, mod, g_pre, w_router, router_bias, w_sh_gate, w_sh_up, w_sh_down, seq, tb):
    t, d = x1.shape
    nb = t // tb
    n_chunk = d // LANES
    d_sh = w_sh_gate.shape[1]
    perm = jnp.arange(N_EXPERTS).reshape(N_GROUPS, GROUP_SIZE).T.reshape(-1)
    wrt = w_router.T[perm].astype(BF16)
    bias = router_bias[perm].reshape(N_EXPERTS, 1)
    nq = _num_list_rows(tb)
    lc = min(ROUTE_LANES, tb)
    n_sub = tb // lc
    assert tb <= 256 * TOKEN_RADIX and TOKEN_RADIX <= 256, "token digits must be exact in bf16"
    const2 = lambda bi, i: (0, 0)
    per_block = lambda bi, i: (bi, 0, 0)
    sub_tile = lambda bi, i: (bi * n_sub + jnp.minimum(i, n_sub - 1), 0)
    outs = pl.pallas_call(
        _route_kernel,
        grid=(nb, n_sub + 1),
        in_specs=[
            pl.BlockSpec((lc, d), sub_tile),
            pl.BlockSpec((1, 6, d), lambda bi, i: (bi * tb // seq, 0, 0)),
            pl.BlockSpec((1, d), const2),
            pl.BlockSpec((N_EXPERTS, d), const2),
            pl.BlockSpec((N_EXPERTS, 1), const2),
            pl.BlockSpec((d, d_sh), const2),
            pl.BlockSpec((d, d_sh), const2),
            pl.BlockSpec((d_sh, d), const2),
        ],
        out_specs=[
            pl.BlockSpec((1, (tb + 1) * n_chunk, LANES), per_block),
            pl.BlockSpec((lc, d), sub_tile),
            pl.BlockSpec((1, nq, MOE_CHUNK), per_block),
            pl.BlockSpec((1, nq, MOE_CHUNK), per_block),
            pl.BlockSpec((1, N_EXPERTS, LANES), per_block),
            pl.BlockSpec((1, N_EXPERTS, LANES), per_block),
        ],
        out_shape=[
            jax.ShapeDtypeStruct((nb, (tb + 1) * n_chunk, LANES), F32),
            jax.ShapeDtypeStruct((t, d), F32),
            jax.ShapeDtypeStruct((nb, nq, MOE_CHUNK), jnp.int32),
            jax.ShapeDtypeStruct((nb, nq, MOE_CHUNK), F32),
            jax.ShapeDtypeStruct((nb, N_EXPERTS, LANES), jnp.int32),
            jax.ShapeDtypeStruct((nb, N_EXPERTS, LANES), jnp.int32),
        ],
        scratch_shapes=[
            pltpu.VMEM((n_sub, N_EXPERTS, lc), F32),
            pltpu.VMEM((N_EXPERTS, tb), F32),
            pltpu.VMEM((N_EXPERTS, tb), jnp.int32),
            pltpu.VMEM((N_EXPERTS, tb), F32),
        ],
        compiler_params=pltpu.CompilerParams(
            dimension_semantics=("arbitrary", "arbitrary"), vmem_limit_bytes=VMEM_LIMIT),
    )(x1, mod, g_pre.reshape(1, d), wrt, bias, w_sh_gate.astype(BF16), w_sh_up.astype(BF16),
      w_sh_down.astype(BF16))
    return outs


def _num_list_rows(tb):
    rows = -(-TOP_K * tb // MOE_CHUNK) + N_EXPERTS + 2 * LIST_LEAD
    return -(-rows // SUBLANES) * SUBLANES


def _moe_kernel(nch_sm, q0_sm, h2g_hbm, rowl_hbm, wl_ref, wg_ref, wu_ref, wd_ref, o_hbm,
                h2g_ref, acc_ref, xt_ref, xb0_ref, xb1_ref, yt0_ref, yt1_ref, act0_ref, act1_ref,
                wgb_ref, wub_ref, wdb_ref, rowl_sm, experts_sm, sems):
    bi = pl.program_id(0)
    r = pl.program_id(1)
    n_chunk = xt_ref.shape[0] // (MOE_CHUNK + SUBLANES)
    m = MOE_CHUNK
    stride = m + SUBLANES
    eye = (lax.broadcasted_iota(jnp.int32, (m, m), 0)
           == lax.broadcasted_iota(jnp.int32, (m, m), 1))
    xbs, yts, acts = (xb0_ref, xb1_ref), (yt0_ref, yt1_ref), (act0_ref, act1_ref)

    def gather(lrow, xb_ref):
        base = lrow * m
        for mi in range(m):
            row = pl.multiple_of(rowl_sm[base + mi], n_chunk)
            xt_ref[pl.ds(mi, n_chunk, stride=stride), :] = h2g_ref[pl.ds(row, n_chunk), :]
        for c in range(n_chunk):
            xb_ref[:, c * LANES:(c + 1) * LANES] = xt_ref[c * stride:c * stride + m, :].astype(BF16)

    def scatter(lrow, yt_ref):
        base = lrow * m
        for g0 in range(0, m, SCATTER_GROUP):
            rows, vals = [], []
            for mi in range(g0, g0 + SCATTER_GROUP):
                row = pl.multiple_of(rowl_sm[base + mi], n_chunk)
                rows.append(row)
                vals.append(acc_ref[pl.ds(row, n_chunk), :]
                            + yt_ref[pl.ds(mi, n_chunk, stride=stride), :])
            for row, val in zip(rows, vals):
                acc_ref[pl.ds(row, n_chunk), :] = val

    def gate_up(xb_ref, act_ref):
        xs = xb_ref[...]
        act = (_silu(jnp.dot(xs, wgb_ref[...], preferred_element_type=F32))
               * jnp.dot(xs, wub_ref[...], preferred_element_type=F32))
        act_ref[...] = act.astype(BF16)

    def down(lrow, act_ref, slot, yt_ref):
        y = jnp.dot(act_ref[...], wdb_ref[slot], preferred_element_type=F32)
        w_row = wl_ref[0, pl.ds(lrow, 1), :]
        w_col = jnp.sum(jnp.where(eye, w_row, 0.0), axis=1, keepdims=True)
        y = y * w_col
        for c in range(n_chunk):
            yt_ref[c * stride:c * stride + m, :] = y[:, c * LANES:(c + 1) * LANES]

    def step(q, par, down_slot):
        cur, oth = par, 1 - par
        gather(q + 1 + LIST_LEAD, xbs[oth])
        gate_up(xbs[cur], acts[cur])
        down(q - 1 + LIST_LEAD, acts[oth], down_slot, yts[oth])
        scatter(q - 2 + LIST_LEAD, yts[cur])

    def by_parity(q, fn):
        for par in (0, 1):
            @pl.when((q & 1) == par)
            def _():
                fn(par)

    def block_loads():
        return (pltpu.make_async_copy(rowl_hbm.at[bi], rowl_sm, sems.at[0]),
                pltpu.make_async_copy(h2g_hbm.at[bi], h2g_ref, sems.at[1]))

    def block_store():
        return pltpu.make_async_copy(acc_ref, o_hbm.at[bi], sems.at[2])

    @pl.when(r == 0)
    def _():
        for cp in block_loads():
            cp.start()
        acc_ref[...] = jnp.zeros(acc_ref.shape, F32)
        act1_ref[...] = jnp.zeros(act1_ref.shape, BF16)
        yt0_ref[...] = jnp.zeros(yt0_ref.shape, F32)
        wdb_ref[...] = jnp.zeros(wdb_ref.shape, BF16)
        experts_sm[0] = 0
        for cp in block_loads():
            cp.wait()
        gather(LIST_LEAD, xb0_ref)

    n_chunks = nch_sm[bi * N_EXPERTS + r]
    q_first = q0_sm[bi * N_EXPERTS + r]
    slot = experts_sm[0] & 1

    @pl.when(n_chunks > 0)
    def _():
        wgb_ref[...] = wg_ref[0, 0].astype(BF16)
        wub_ref[...] = wu_ref[0, 0].astype(BF16)
        wdb_ref[slot] = wd_ref[0, 0].astype(BF16)
        by_parity(q_first, lambda par: step(q_first, par, 1 - slot))
        experts_sm[0] = experts_sm[0] + 1

    def chunk(ci, carry):
        q = q_first + ci
        by_parity(q, lambda par: step(q, par, slot))
        return carry

    lax.fori_loop(1, n_chunks, chunk, 0)

    @pl.when(r == N_EXPERTS - 1)
    def _():
        q_last = q_first + n_chunks - 1
        last_slot = (experts_sm[0] - 1) & 1

        def drain(par):
            scatter(q_last - 1 + LIST_LEAD, yts[1 - par])
            down(q_last + LIST_LEAD, acts[par], last_slot, yts[par])
            scatter(q_last + LIST_LEAD, yts[par])

        by_parity(q_last, drain)

        block_store().start()
        block_store().wait()


def _moe(h2g, rowl, wl, nch, q0, w_gate, w_up, w_down, layer, t, d, tb):
    nb = t // tb
    n_chunk = d // LANES
    d_e = w_gate.shape[3]
    m = MOE_CHUNK
    stride = m + SUBLANES
    nq = _num_list_rows(tb)

    def expert_of(r):
        return (r % N_GROUPS) * GROUP_SIZE + r // N_GROUPS

    cur_w = lambda bi, r, c, o: (layer, expert_of(r), 0, 0)
    per_block = lambda bi, r, c, o: (bi, 0, 0)
    staging = pltpu.VMEM((n_chunk * stride, LANES), F32)
    grid_spec = pltpu.PrefetchScalarGridSpec(
        num_scalar_prefetch=2,
        grid=(nb, N_EXPERTS),
        in_specs=[
            pl.BlockSpec(memory_space=pl.ANY),
            pl.BlockSpec(memory_space=pl.ANY),
            pl.BlockSpec((1, nq, m), per_block),
            pl.BlockSpec((1, 1, d, d_e), cur_w),
            pl.BlockSpec((1, 1, d, d_e), cur_w),
            pl.BlockSpec((1, 1, d_e, d), cur_w),
        ],
        out_specs=pl.BlockSpec(memory_space=pl.ANY),
        scratch_shapes=[
            pltpu.VMEM(((tb + 1) * n_chunk, LANES), F32),
            pltpu.VMEM(((tb + 1) * n_chunk, LANES), F32),
            staging,
            pltpu.VMEM((m, d), BF16),
            pltpu.VMEM((m, d), BF16),
            staging, staging,
            pltpu.VMEM((m, d_e), BF16),
            pltpu.VMEM((m, d_e), BF16),
            pltpu.VMEM((d, d_e), BF16),
            pltpu.VMEM((d, d_e), BF16),
            pltpu.VMEM((2, d_e, d), BF16),
            pltpu.SMEM((nq * m,), jnp.int32),
            pltpu.SMEM((1,), jnp.int32),
            pltpu.SemaphoreType.DMA((3,)),
        ],
    )
    return pl.pallas_call(
        _moe_kernel,
        grid_spec=grid_spec,
        out_shape=jax.ShapeDtypeStruct((nb, (tb + 1) * n_chunk, LANES), F32),
        compiler_params=pltpu.CompilerParams(
            dimension_semantics=("arbitrary", "arbitrary"), vmem_limit_bytes=VMEM_LIMIT),
    )(nch.reshape(-1), q0.reshape(-1), h2g, rowl.reshape(nb, nq * m), wl, w_gate, w_up, w_down)


def _epilogue_kernel(x_ref, routed_ref, shared_ref, mod_ref, g_ref, o_ref):
    o_ref[...] = _ffn_residual(x_ref[...], routed_ref, shared_ref[...], mod_ref[0][5:6], g_ref[...])


def _epilogue(x1, routed, shared, mod, g_post, seq, tb):
    t, d = x1.shape
    n_chunk = d // LANES
    te = min(EPI_TILE, tb)
    n_sub = tb // te
    row = pl.BlockSpec((te, d), lambda bi, i: (bi * n_sub + i, 0))
    return pl.pallas_call(
        _epilogue_kernel,
        grid=(t // tb, n_sub),
        in_specs=[row,
                  pl.BlockSpec((1, te * n_chunk, LANES), lambda bi, i: (bi, i, 0)),
                  row,
                  pl.BlockSpec((1, 6, d), lambda bi, i: (bi * tb // seq, 0, 0)),
                  pl.BlockSpec((1, d), lambda bi, i: (0, 0))],
        out_specs=row,
        out_shape=jax.ShapeDtypeStruct((t, d), F32),
    )(x1, routed, shared, mod, g_post.reshape(1, d))


def kernel(x, c, w_ada, b_ada, g_pre_mix, g_post_mix, g_pre_ffn, g_post_ffn, w_in, conv_w,
           w_conv_out, w_pool_group, pool_scale, w_pool_proj, w_o, w_router, router_bias,
           w_exp_gate, w_exp_up, w_exp_down, w_sh_gate, w_sh_up, w_sh_down):
    b, s, d = x.shape
    depth = w_ada.shape[0]
    t = b * s
    tb = min(MOE_BLOCK, s)
    mods = _ada_mod(c, w_ada, b_ada).reshape(depth, b, 6, d)
    pending_ffn = None
    for l in range(depth):
        mod = mods[l]
        x = _token_mixer(x, pending_ffn, mod, g_pre_mix[l], g_post_mix[l], w_in[l], conv_w[l],
                         w_conv_out[l], w_pool_group[l], pool_scale[l], w_pool_proj[l], w_o[l])
        x1 = x.reshape(t, d)
        h2g, shared, rowl, wl, nch, q0 = _route(
            x1, mod, g_pre_ffn[l], w_router[l], router_bias[l], w_sh_gate[l], w_sh_up[l],
            w_sh_down[l], s, tb)
        routed = _moe(h2g, rowl, wl, nch[:, :, 0], q0[:, :, 0],
                      w_exp_gate, w_exp_up, w_exp_down, l, t, d, tb)
        pending_ffn = (routed, shared, mod, g_post_ffn[l], tb)
    routed, shared, mod, g_post, tb = pending_ffn
    return _epilogue(x.reshape(t, d), routed, shared, mod, g_post, s, tb).reshape(b, s, d)
```

```python
import functools

import jax
import jax.numpy as jnp
from jax import lax
from jax.experimental import pallas as pl
from jax.experimental.pallas import tpu as pltpu

F32 = jnp.float32
BF16 = jnp.bfloat16

EPS = 1e-6
POOL_WINDOWS = (2, 4, 8, 16)
POOL_GROUP_DIM = 128
N_EXPERTS = 64
N_GROUPS = 8
GROUP_SIZE = 8
TOPK_GROUPS = 4
TOP_K = 8
ROUTED_SCALE = 2.5

LANES = 128
SUBLANES = 8
CONV_HALO = 8
POOL_HALO = 16
VMEM_LIMIT = 56 * 1024 * 1024

ADA_COLS = 2048
MIX_TILE = 512
MIX_CHAINS = 2
ROUTE_LANES = 256
MOE_BLOCK = 4096
MOE_CHUNK = 192
SCATTER_GROUP = 8
TOKEN_RADIX = 64.0
LIST_LEAD = 2
EPI_TILE = 512


def _silu(v):
    return v * jax.nn.sigmoid(v)


def _rms_scale(v):
    return lax.rsqrt(jnp.mean(v * v, axis=-1, keepdims=True) + EPS)


def _ada_kernel(c_ref, w_ref, b_ref, o_ref):
    cond = _silu(c_ref[...]).astype(BF16)
    o_ref[0] = jnp.dot(cond, w_ref[0].astype(BF16), preferred_element_type=F32) + b_ref[0]


def _ada_mod(c, w_ada, b_ada):
    depth, d, d6 = w_ada.shape
    b = c.shape[0]
    cols = ADA_COLS
    return pl.pallas_call(
        _ada_kernel,
        grid=(depth, d6 // cols),
        in_specs=[
            pl.BlockSpec((b, d), lambda l, n: (0, 0)),
            pl.BlockSpec((1, d, cols), lambda l, n: (l, 0, n)),
            pl.BlockSpec((1, 1, cols), lambda l, n: (l, 0, n)),
        ],
        out_specs=pl.BlockSpec((1, b, cols), lambda l, n: (l, 0, n)),
        out_shape=jax.ShapeDtypeStruct((depth, b, d6), F32),
    )(c, w_ada, b_ada.reshape(depth, 1, d6))


def _ffn_residual(x, routed_ref, shared, gt2, g_post, row0=0):
    rows, d = x.shape
    n_chunk = d // LANES
    routed = jnp.concatenate(
        [routed_ref[0, pl.ds(row0 * n_chunk + c, rows, stride=n_chunk), :]
         for c in range(n_chunk)], axis=1)
    y = routed + shared
    return x + gt2 * (y * _rms_scale(y) * g_post)


def _mixer_kernel(*refs, after_ffn):
    if after_ffn:
        (x_ref, routed_ref, shared_ref, modp_ref, gpp_ref), refs = refs[:5], refs[5:]
    else:
        x_ref, refs = refs[0], refs[1:]
    (mod_ref, gpre_ref, gpost_ref, win_ref, convw_ref, wco_ref, wpg_ref, pscale_ref, wpp_ref,
     wo_ref, o_ref, uext_ref, pext_ref) = refs
    j = pl.program_id(1)
    tm, d = x_ref.shape[1], x_ref.shape[2]
    d_pool = pext_ref.shape[1]

    @pl.when(j == 0)
    def _():
        uext_ref[0:CONV_HALO, :] = jnp.zeros((CONV_HALO, d), F32)
        pext_ref[0:POOL_HALO, :] = jnp.zeros((POOL_HALO, d_pool), F32)

    mod = mod_ref[0]
    sh1, sc1, gt1 = mod[0:1], mod[1:2], mod[2:3]
    cw = convw_ref[...]
    ts = tm // MIX_CHAINS
    for ch in range(MIX_CHAINS):
        r0 = ch * ts
        x = x_ref[0, r0:r0 + ts, :]
        if after_ffn:
            x = _ffn_residual(x, routed_ref, shared_ref[0, r0:r0 + ts, :], modp_ref[0][5:6],
                              gpp_ref[...], row0=r0)
        h = x * _rms_scale(x) * gpre_ref[...] * (1.0 + sc1) + sh1
        hb = h.astype(BF16)

        def proj(lo, hi):
            return jnp.dot(hb, win_ref[:, lo:hi], preferred_element_type=F32)

        u = proj(d, 2 * d) * proj(2 * d, 3 * d)
        u0 = CONV_HALO + r0
        uext_ref[u0:u0 + ts, :] = u
        conv = (cw[2:3] * u
                + cw[1:2] * uext_ref[u0 - 1:u0 - 1 + ts, :]
                + cw[0:1] * uext_ref[u0 - 2:u0 - 2 + ts, :])
        y_conv = jnp.dot((proj(0, d) * conv).astype(BF16), wco_ref[...],
                         preferred_element_type=F32)

        up = proj(3 * d, 3 * d + d_pool)
        p0 = POOL_HALO + r0
        pext_ref[p0:p0 + ts, :] = up
        pos = j * tm + r0 + lax.broadcasted_iota(jnp.int32, (ts, 1), 0)
        zs = []
        for g, w in enumerate(POOL_WINDOWS):
            c0 = g * POOL_GROUP_DIM
            ug = up[:, c0:c0 + POOL_GROUP_DIM]
            acc = ug
            for k in range(1, w):
                acc = acc + pext_ref[p0 - k:p0 - k + ts, c0:c0 + POOL_GROUP_DIM]
            inv_cnt = 1.0 / jnp.minimum(pos + 1, w).astype(F32)
            diff = acc * inv_cnt - ug
            zs.append(jnp.dot(diff.astype(BF16), wpg_ref[g], preferred_element_type=F32))
        z = jnp.concatenate(zs, axis=1) * pscale_ref[...]
        y_pool = jnp.dot(z.astype(BF16), wpp_ref[...], preferred_element_type=F32)

        a_conv = proj(3 * d + d_pool, 4 * d + d_pool)
        a_pool = proj(4 * d + d_pool, 5 * d + d_pool)
        merged = jax.nn.sigmoid(a_conv) * y_conv + jax.nn.sigmoid(a_pool) * y_pool
        y = jnp.dot(merged.astype(BF16), wo_ref[...], preferred_element_type=F32)
        o_ref[0, r0:r0 + ts, :] = x + gt1 * (y * _rms_scale(y) * gpost_ref[...])

    uext_ref[0:CONV_HALO, :] = uext_ref[tm:tm + CONV_HALO, :]
    pext_ref[0:POOL_HALO, :] = pext_ref[tm:tm + POOL_HALO, :]


def _token_mixer(x, pending_ffn, mod, g_pre, g_post, w_in, conv_w, w_conv_out, w_pool_group,
                 pool_scale, w_pool_proj, w_o):
    b, s, d = x.shape
    d_in = w_in.shape[1]
    d_pool = w_pool_proj.shape[0]
    n_chunk = d // LANES
    tm = min(MIX_TILE, s)
    const2 = lambda bi, j: (0, 0)
    const3 = lambda bi, j: (0, 0, 0)
    rows = pl.BlockSpec((1, tm, d), lambda bi, j: (bi, j, 0))
    per_batch = pl.BlockSpec((1, 6, d), lambda bi, j: (bi, 0, 0))
    prev_specs, prev_args = [], []
    if pending_ffn is not None:
        routed, shared, mod_p, g_post_p, tb = pending_ffn
        tiles = tb // tm
        prev_specs = [
            pl.BlockSpec((1, tm * n_chunk, LANES),
                         lambda bi, j: ((bi * (s // tm) + j) // tiles, (bi * (s // tm) + j) % tiles, 0)),
            rows, per_batch, pl.BlockSpec((1, d), const2)]
        prev_args = [routed, shared.reshape(b, s, d), mod_p, g_post_p.reshape(1, d)]
    return pl.pallas_call(
        functools.partial(_mixer_kernel, after_ffn=pending_ffn is not None),
        grid=(b, s // tm),
        in_specs=[
            rows,
            *prev_specs,
            per_batch,
            pl.BlockSpec((1, d), const2),
            pl.BlockSpec((1, d), const2),
            pl.BlockSpec((d, d_in), const2),
            pl.BlockSpec((3, d), const2),
            pl.BlockSpec((d, d), const2),
            pl.BlockSpec(w_pool_group.shape, const3),
            pl.BlockSpec((1, d_pool), const2),
            pl.BlockSpec((d_pool, d), const2),
            pl.BlockSpec((d, d), const2),
        ],
        out_specs=pl.BlockSpec((1, tm, d), lambda bi, j: (bi, j, 0)),
        out_shape=jax.ShapeDtypeStruct(x.shape, F32),
        scratch_shapes=[
            pltpu.VMEM((CONV_HALO + tm, d), F32),
            pltpu.VMEM((POOL_HALO + tm, d_pool), F32),
        ],
        compiler_params=pltpu.CompilerParams(
            dimension_semantics=("arbitrary", "arbitrary"), vmem_limit_bytes=VMEM_LIMIT),
    )(x, *prev_args, mod, g_pre.reshape(1, d), g_post.reshape(1, d), w_in.astype(BF16), conv_w,
      w_conv_out.astype(BF16), w_pool_group.astype(BF16), pool_scale.reshape(1, d_pool),
      w_pool_proj.astype(BF16), w_o.astype(BF16))


def _wins(other, v, tie_i):
    return jnp.where(other > v, 1, 0) + jnp.where(other == v, tie_i, 0)


def _route_kernel(x_ref, mod_ref, g_ref, wrt_ref, bias_ref, wsg_ref, wsu_ref, wsd_ref,
                  h2g_ref, shared_ref, rowl_ref, wl_ref, nch_ref, q0_ref,
                  lg_ref, pre_ref, rho_ref, wd_ref):
    i = pl.program_id(1)
    lc, d = x_ref.shape
    n_sub = lg_ref.shape[0]
    n_chunk = d // LANES

    @pl.when(i < n_sub)
    def _():
        mod = mod_ref[0]
        sh2, sc2 = mod[3:4], mod[4:5]

        @pl.when(i == 0)
        def _():
            spare = n_sub * lc * n_chunk
            h2g_ref[0, spare:spare + n_chunk, :] = jnp.zeros((n_chunk, LANES), F32)

        x = x_ref[...]
        h = x * _rms_scale(x) * g_ref[...] * (1.0 + sc2) + sh2
        for c in range(n_chunk):
            h2g_ref[0, pl.ds(i * lc * n_chunk + c, lc, stride=n_chunk), :] = (
                h[:, c * LANES:(c + 1) * LANES])
        hb = h.astype(BF16)
        act = (_silu(jnp.dot(hb, wsg_ref[...], preferred_element_type=F32))
               * jnp.dot(hb, wsu_ref[...], preferred_element_type=F32))
        shared_ref[...] = jnp.dot(act.astype(BF16), wsd_ref[...], preferred_element_type=F32)
        lg_ref[i] = lax.dot_general(
            wrt_ref[...], hb, (((1,), (1,)), ((), ())), preferred_element_type=F32)

    @pl.when(i == n_sub)
    def _():
        _route_plan(bias_ref, rowl_ref, wl_ref, nch_ref, q0_ref, lg_ref, pre_ref, rho_ref, wd_ref,
                    lc, n_chunk)


def _route_plan(bias_ref, rowl_ref, wl_ref, nch_ref, q0_ref, lg_ref, pre_ref, rho_ref, wd_ref,
                lc, n_chunk):
    n_sub = lg_ref.shape[0]
    tb = n_sub * lc
    gidx = lax.broadcasted_iota(jnp.int32, (N_GROUPS, lc), 0)
    tie = [None] + [jnp.where(gidx >= r, 1, 0) for r in range(1, N_GROUPS)]
    tri = (lax.broadcasted_iota(jnp.int32, (lc, lc), 0)
           < lax.broadcasted_iota(jnp.int32, (lc, lc), 1)).astype(BF16)
    carry = jnp.zeros((N_EXPERTS, 1), F32)
    neg_inf = jnp.float32(-jnp.inf)
    for ci in range(tb // lc):
        c0 = ci * lc
        s_all = jax.nn.sigmoid(lg_ref[ci])
        aff = [s_all[GROUP_SIZE * jj:GROUP_SIZE * (jj + 1), :] for jj in range(GROUP_SIZE)]
        sel = [aff[jj] + bias_ref[GROUP_SIZE * jj:GROUP_SIZE * (jj + 1), :]
               for jj in range(GROUP_SIZE)]
        m1, m2 = sel[0], jnp.full_like(sel[0], neg_inf)
        for jj in range(1, GROUP_SIZE):
            m2 = jnp.maximum(m2, jnp.minimum(m1, sel[jj]))
            m1 = jnp.maximum(m1, sel[jj])
        gs = m1 + m2
        beaten = jnp.zeros((N_GROUPS, lc), jnp.int32)
        for r in range(1, N_GROUPS):
            other = pltpu.roll(gs, r, axis=0)
            beaten = beaten + _wins(other, gs, tie[r])
        gmask = beaten < TOPK_GROUPS
        masked = [jnp.where(gmask, sel[jj], neg_inf) for jj in range(GROUP_SIZE)]
        rolled = [[masked[jj]] + [pltpu.roll(masked[jj], r, axis=0) for r in range(1, N_GROUPS)]
                  for jj in range(GROUP_SIZE)]
        rho = []
        for jj in range(GROUP_SIZE):
            v = masked[jj]
            cnt = jnp.zeros((N_GROUPS, lc), jnp.int32)
            for j2 in range(GROUP_SIZE):
                for r in range(N_GROUPS):
                    if r == 0 and j2 == jj:
                        continue
                    other = rolled[j2][r]
                    if r == 0:
                        wins = (other >= v) if j2 < jj else (other > v)
                        cnt = cnt + jnp.where(wins, 1, 0)
                    else:
                        cnt = cnt + _wins(other, v, tie[r])
            rho.append(cnt)
        chosen = [rho[jj] < TOP_K for jj in range(GROUP_SIZE)]
        ssum = jnp.zeros((N_GROUPS, lc), F32)
        for jj in range(GROUP_SIZE):
            ssum = ssum + jnp.where(chosen[jj], aff[jj], 0.0)
        ssum = jnp.sum(ssum, axis=0, keepdims=True)
        wdense = [jnp.where(chosen[jj], aff[jj] / ssum * ROUTED_SCALE, 0.0)
                  for jj in range(GROUP_SIZE)]
        chosen_f = jnp.concatenate([c.astype(F32) for c in chosen], axis=0)
        prefix = jnp.dot(chosen_f.astype(BF16), tri, preferred_element_type=F32) + carry
        carry = carry + jnp.sum(chosen_f, axis=1, keepdims=True)
        pre_ref[:, c0:c0 + lc] = prefix
        rho_ref[:, c0:c0 + lc] = jnp.concatenate(rho, axis=0)
        wd_ref[:, c0:c0 + lc] = jnp.concatenate(wdense, axis=0)

    m = MOE_CHUNK
    nch_b = jnp.broadcast_to(jnp.floor((carry + (m - 0.5)) * (1.0 / m)), (N_EXPERTS, LANES))
    lower = (lax.broadcasted_iota(jnp.int32, (N_EXPERTS, N_EXPERTS), 1)
             < lax.broadcasted_iota(jnp.int32, (N_EXPERTS, N_EXPERTS), 0)).astype(F32)
    q0_b = jnp.dot(lower, nch_b, preferred_element_type=F32, precision=lax.Precision.HIGHEST)
    nch_ref[0] = nch_b.astype(jnp.int32)
    q0_ref[0] = q0_b.astype(jnp.int32)
    offs_col = q0_b[:, 0:1] * m

    nq = rowl_ref.shape[1]
    iota_q = lax.broadcasted_iota(jnp.int32, (nq, lc), 0).astype(F32)
    iota_r = lax.broadcasted_iota(jnp.int32, (m, lc), 0).astype(F32)
    lists = jnp.zeros((nq, 5 * m), F32)
    for ci in range(tb // lc):
        c0 = ci * lc
        dest_dense = pre_ref[:, c0:c0 + lc] + offs_col
        rho_c = rho_ref[:, c0:c0 + lc]
        w_c = wd_ref[:, c0:c0 + lc]
        tokv = (c0 + 1 + lax.broadcasted_iota(jnp.int32, (1, lc), 1)).astype(F32)
        tok_hi = jnp.floor(tokv * (1.0 / TOKEN_RADIX))
        tok_lo = tokv - TOKEN_RADIX * tok_hi
        for k in range(TOP_K):
            hit = rho_c == k
            dk = jnp.sum(jnp.where(hit, dest_dense, 0.0), axis=0, keepdims=True)
            wk = jnp.sum(jnp.where(hit, w_c, 0.0), axis=0, keepdims=True)
            qk = jnp.floor((dk + 0.5) * (1.0 / m))
            rk = dk - m * qk
            w_hi = wk.astype(BF16).astype(F32)
            w_mid = (wk - w_hi).astype(BF16).astype(F32)
            w_lo = wk - w_hi - w_mid
            onehot_q = jnp.where(iota_q == qk + LIST_LEAD, 1.0, 0.0).astype(BF16)
            rmask = iota_r == rk
            vals = jnp.concatenate(
                [jnp.where(rmask, piece, 0.0) for piece in (tok_hi, tok_lo, w_hi, w_mid, w_lo)],
                axis=0).astype(BF16)
            lists = lists + lax.dot_general(onehot_q, vals, (((1,), (1,)), ((), ())),
                                            preferred_element_type=F32)
    tok = lists[:, 0:m] * TOKEN_RADIX + lists[:, m:2 * m]
    tile = jnp.where(tok == 0.0, float(tb), tok - 1.0)
    rowl_ref[0] = (tile * n_chunk).astype(jnp.int32)
    wl_ref[0] = (lists[:, 2 * m:3 * m] + lists[:, 3 * m:4 * m]) + lists[:, 4 * m:5 * m]


def _route(x1, mod, g_pre, w_router, router_bias, w_sh_gate, w_sh_up, w_sh_down, seq, tb):
    t, d = x1.shape
    nb = t // tb
    n_chunk = d // LANES
    d_sh = w_sh_gate.shape[1]
    perm = jnp.arange(N_EXPERTS).reshape(N_GROUPS, GROUP_SIZE).T.reshape(-1)
    wrt = w_router.T[perm].astype(BF16)
    bias = router_bias[perm].reshape(N_EXPERTS, 1)
    nq = _num_list_rows(tb)
    lc = min(ROUTE_LANES, tb)
    n_sub = tb // lc
    assert tb <= 256 * TOKEN_RADIX and TOKEN_RADIX <= 256, "token digits must be exact in bf16"
    const2 = lambda bi, i: (0, 0)
    per_block = lambda bi, i: (bi, 0, 0)
    sub_tile = lambda bi, i: (bi * n_sub + jnp.minimum(i, n_sub - 1), 0)
    outs = pl.pallas_call(
        _route_kernel,
        grid=(nb, n_sub + 1),
        in_specs=[
            pl.BlockSpec((lc, d), sub_tile),
            pl.BlockSpec((1, 6, d), lambda bi, i: (bi * tb // seq, 0, 0)),
            pl.BlockSpec((1, d), const2),
            pl.BlockSpec((N_EXPERTS, d), const2),
            pl.BlockSpec((N_EXPERTS, 1), const2),
            pl.BlockSpec((d, d_sh), const2),
            pl.BlockSpec((d, d_sh), const2),
            pl.BlockSpec((d_sh, d), const2),
        ],
        out_specs=[
            pl.BlockSpec((1, (tb + 1) * n_chunk, LANES), per_block),
            pl.BlockSpec((lc, d), sub_tile),
            pl.BlockSpec((1, nq, MOE_CHUNK), per_block),
            pl.BlockSpec((1, nq, MOE_CHUNK), per_block),
            pl.BlockSpec((1, N_EXPERTS, LANES), per_block),
            pl.BlockSpec((1, N_EXPERTS, LANES), per_block),
        ],
        out_shape=[
            jax.ShapeDtypeStruct((nb, (tb + 1) * n_chunk, LANES), F32),
            jax.ShapeDtypeStruct((t, d), F32),
            jax.ShapeDtypeStruct((nb, nq, MOE_CHUNK), jnp.int32),
            jax.ShapeDtypeStruct((nb, nq, MOE_CHUNK), F32),
            jax.ShapeDtypeStruct((nb, N_EXPERTS, LANES), jnp.int32),
            jax.ShapeDtypeStruct((nb, N_EXPERTS, LANES), jnp.int32),
        ],
        scratch_shapes=[
            pltpu.VMEM((n_sub, N_EXPERTS, lc), F32),
            pltpu.VMEM((N_EXPERTS, tb), F32),
            pltpu.VMEM((N_EXPERTS, tb), jnp.int32),
            pltpu.VMEM((N_EXPERTS, tb), F32),
        ],
        compiler_params=pltpu.CompilerParams(
            dimension_semantics=("arbitrary", "arbitrary"), vmem_limit_bytes=VMEM_LIMIT),
    )(x1, mod, g_pre.reshape(1, d), wrt, bias, w_sh_gate.astype(BF16), w_sh_up.astype(BF16),
      w_sh_down.astype(BF16))
    return outs


def _num_list_rows(tb):
    rows = -(-TOP_K * tb // MOE_CHUNK) + N_EXPERTS + 2 * LIST_LEAD
    return -(-rows // SUBLANES) * SUBLANES


def _moe_kernel(nch_sm, q0_sm, h2g_hbm, rowl_hbm, wl_ref, wg_ref, wu_ref, wd_ref, o_hbm,
                h2g_ref, acc_ref, xt0_ref, xt1_ref, yt0_ref, yt1_ref, act0_ref, act1_ref,
                wgb_ref, wub_ref, wdb_ref, rowl_sm, experts_sm, sems):
    bi = pl.program_id(0)
    r = pl.program_id(1)
    n_chunk = xt0_ref.shape[0] // (MOE_CHUNK + SUBLANES)
    m = MOE_CHUNK
    stride = m + SUBLANES
    eye = (lax.broadcasted_iota(jnp.int32, (m, m), 0)
           == lax.broadcasted_iota(jnp.int32, (m, m), 1))
    xts, yts, acts = (xt0_ref, xt1_ref), (yt0_ref, yt1_ref), (act0_ref, act1_ref)

    def gather(lrow, xt_ref):
        base = lrow * m
        for mi in range(m):
            row = pl.multiple_of(rowl_sm[base + mi], n_chunk)
            xt_ref[pl.ds(mi, n_chunk, stride=stride), :] = h2g_ref[pl.ds(row, n_chunk), :]

    def scatter(lrow, yt_ref):
        base = lrow * m
        for g0 in range(0, m, SCATTER_GROUP):
            rows, vals = [], []
            for mi in range(g0, g0 + SCATTER_GROUP):
                row = pl.multiple_of(rowl_sm[base + mi], n_chunk)
                rows.append(row)
                vals.append(acc_ref[pl.ds(row, n_chunk), :]
                            + yt_ref[pl.ds(mi, n_chunk, stride=stride), :])
            for row, val in zip(rows, vals):
                acc_ref[pl.ds(row, n_chunk), :] = val

    def gate_up(xt_ref, act_ref):
        xs = jnp.concatenate([xt_ref[c * stride:c * stride + m, :] for c in range(n_chunk)],
                             axis=1).astype(BF16)
        act = (_silu(jnp.dot(xs, wgb_ref[...], preferred_element_type=F32))
               * jnp.dot(xs, wub_ref[...], preferred_element_type=F32))
        act_ref[...] = act.astype(BF16)

    def down(lrow, act_ref, slot, yt_ref):
        y = jnp.dot(act_ref[...], wdb_ref[slot], preferred_element_type=F32)
        w_row = wl_ref[0, pl.ds(lrow, 1), :]
        w_col = jnp.sum(jnp.where(eye, w_row, 0.0), axis=1, keepdims=True)
        y = y * w_col
        for c in range(n_chunk):
            yt_ref[c * stride:c * stride + m, :] = y[:, c * LANES:(c + 1) * LANES]

    def step(q, par, down_slot):
        cur, oth = par, 1 - par
        gather(q + 1 + LIST_LEAD, xts[oth])
        gate_up(xts[cur], acts[cur])
        down(q - 1 + LIST_LEAD, acts[oth], down_slot, yts[oth])
        scatter(q - 2 + LIST_LEAD, yts[cur])

    def by_parity(q, fn):
        for par in (0, 1):
            @pl.when((q & 1) == par)
            def _():
                fn(par)

    def block_loads():
        return (pltpu.make_async_copy(rowl_hbm.at[bi], rowl_sm, sems.at[0]),
                pltpu.make_async_copy(h2g_hbm.at[bi], h2g_ref, sems.at[1]))

    def block_store():
        return pltpu.make_async_copy(acc_ref, o_hbm.at[bi], sems.at[2])

    @pl.when(r == 0)
    def _():
        for cp in block_loads():
            cp.start()
        acc_ref[...] = jnp.zeros(acc_ref.shape, F32)
        act1_ref[...] = jnp.zeros(act1_ref.shape, BF16)
        yt0_ref[...] = jnp.zeros(yt0_ref.shape, F32)
        wdb_ref[...] = jnp.zeros(wdb_ref.shape, BF16)
        experts_sm[0] = 0
        for cp in block_loads():
            cp.wait()
        gather(LIST_LEAD, xt0_ref)

    n_chunks = nch_sm[bi * N_EXPERTS + r]
    q_first = q0_sm[bi * N_EXPERTS + r]
    slot = experts_sm[0] & 1

    @pl.when(n_chunks > 0)
    def _():
        wgb_ref[...] = wg_ref[0, 0].astype(BF16)
        wub_ref[...] = wu_ref[0, 0].astype(BF16)
        wdb_ref[slot] = wd_ref[0, 0].astype(BF16)
        by_parity(q_first, lambda par: step(q_first, par, 1 - slot))
        experts_sm[0] = experts_sm[0] + 1

    def chunk(ci, carry):
        q = q_first + ci
        by_parity(q, lambda par: step(q, par, slot))
        return carry

    lax.fori_loop(1, n_chunks, chunk, 0)

    @pl.when(r == N_EXPERTS - 1)
    def _():
        q_last = q_first + n_chunks - 1
        last_slot = (experts_sm[0] - 1) & 1

        def drain(par):
            scatter(q_last - 1 + LIST_LEAD, yts[1 - par])
            down(q_last + LIST_LEAD, acts[par], last_slot, yts[par])
            scatter(q_last + LIST_LEAD, yts[par])

        by_parity(q_last, drain)

        block_store().start()
        block_store().wait()


def _moe(h2g, rowl, wl, nch, q0, w_gate, w_up, w_down, layer, t, d, tb):
    nb = t // tb
    n_chunk = d // LANES
    d_e = w_gate.shape[3]
    m = MOE_CHUNK
    stride = m + SUBLANES
    nq = _num_list_rows(tb)

    def expert_of(r):
        return (r % N_GROUPS) * GROUP_SIZE + r // N_GROUPS

    cur_w = lambda bi, r, c, o: (layer, expert_of(r), 0, 0)
    per_block = lambda bi, r, c, o: (bi, 0, 0)
    staging = pltpu.VMEM((n_chunk * stride, LANES), F32)
    grid_spec = pltpu.PrefetchScalarGridSpec(
        num_scalar_prefetch=2,
        grid=(nb, N_EXPERTS),
        in_specs=[
            pl.BlockSpec(memory_space=pl.ANY),
            pl.BlockSpec(memory_space=pl.ANY),
            pl.BlockSpec((1, nq, m), per_block),
            pl.BlockSpec((1, 1, d, d_e), cur_w),
            pl.BlockSpec((1, 1, d, d_e), cur_w),
            pl.BlockSpec((1, 1, d_e, d), cur_w),
        ],
        out_specs=pl.BlockSpec(memory_space=pl.ANY),
        scratch_shapes=[
            pltpu.VMEM(((tb + 1) * n_chunk, LANES), F32),
            pltpu.VMEM(((tb + 1) * n_chunk, LANES), F32),
            staging, staging, staging, staging,
            pltpu.VMEM((m, d_e), BF16),
            pltpu.VMEM((m, d_e), BF16),
            pltpu.VMEM((d, d_e), BF16),
            pltpu.VMEM((d, d_e), BF16),
            pltpu.VMEM((2, d_e, d), BF16),
            pltpu.SMEM((nq * m,), jnp.int32),
            pltpu.SMEM((1,), jnp.int32),
            pltpu.SemaphoreType.DMA((3,)),
        ],
    )
    return pl.pallas_call(
        _moe_kernel,
        grid_spec=grid_spec,
        out_shape=jax.ShapeDtypeStruct((nb, (tb + 1) * n_chunk, LANES), F32),
        compiler_params=pltpu.CompilerParams(
            dimension_semantics=("arbitrary", "arbitrary"), vmem_limit_bytes=VMEM_LIMIT),
    )(nch.reshape(-1), q0.reshape(-1), h2g, rowl.reshape(nb, nq * m), wl, w_gate, w_up, w_down)


def _epilogue_kernel(x_ref, routed_ref, shared_ref, mod_ref, g_ref, o_ref):
    o_ref[...] = _ffn_residual(x_ref[...], routed_ref, shared_ref[...], mod_ref[0][5:6], g_ref[...])


def _epilogue(x1, routed, shared, mod, g_post, seq, tb):
    t, d = x1.shape
    n_chunk = d // LANES
    te = min(EPI_TILE, tb)
    n_sub = tb // te
    row = pl.BlockSpec((te, d), lambda bi, i: (bi * n_sub + i, 0))
    return pl.pallas_call(
        _epilogue_kernel,
        grid=(t // tb, n_sub),
        in_specs=[row,
                  pl.BlockSpec((1, te * n_chunk, LANES), lambda bi, i: (bi, i, 0)),
                  row,
                  pl.BlockSpec((1, 6, d), lambda bi, i: (bi * tb // seq, 0, 0)),
                  pl.BlockSpec((1, d), lambda bi, i: (0, 0))],
        out_specs=row,
        out_shape=jax.ShapeDtypeStruct((t, d), F32),
    )(x1, routed, shared, mod, g_post.reshape(1, d))


def kernel(x, c, w_ada, b_ada, g_pre_mix, g_post_mix, g_pre_ffn, g_post_ffn, w_in, conv_w,
           w_conv_out, w_pool_group, pool_scale, w_pool_proj, w_o, w_router, router_bias,
           w_exp_gate, w_exp_up, w_exp_down, w_sh_gate, w_sh_up, w_sh_down):
    b, s, d = x.shape
    depth = w_ada.shape[0]
    t = b * s
    tb = min(MOE_BLOCK, s)
    mods = _ada_mod(c, w_ada, b_ada).reshape(depth, b, 6, d)
    pending_ffn = None
    for l in range(depth):
        mod = mods[l]
        x = _token_mixer(x, pending_ffn, mod, g_pre_mix[l], g_post_mix[l], w_in[l], conv_w[l],
                         w_conv_out[l], w_pool_group[l], pool_scale[l], w_pool_proj[l], w_o[l])
        x1 = x.reshape(t, d)
        h2g, shared, rowl, wl, nch, q0 = _route(
            x1, mod, g_pre_ffn[l], w_router[l], router_bias[l], w_sh_gate[l], w_sh_up[l],
            w_sh_down[l], s, tb)
        routed = _moe(h2g, rowl, wl, nch[:, :, 0], q0[:, :, 0],
                      w_exp_gate, w_exp_up, w_exp_down, l, t, d, tb)
        pending_ffn = (routed, shared, mod, g_post_ffn[l], tb)
    routed, shared, mod, g_post, tb = pending_ffn
    return _epilogue(x.reshape(t, d), routed, shared, mod, g_post, s, tb).reshape(b, s, d)
```

```python
import functools

import jax
import jax.numpy as jnp
from jax import lax
from jax.experimental import pallas as pl
from jax.experimental.pallas import tpu as pltpu

F32 = jnp.float32
BF16 = jnp.bfloat16

EPS = 1e-6
POOL_WINDOWS = (2, 4, 8, 16)
POOL_GROUP_DIM = 128
N_EXPERTS = 64
N_GROUPS = 8
GROUP_SIZE = 8
TOPK_GROUPS = 4
TOP_K = 8
ROUTED_SCALE = 2.5

LANES = 128
SUBLANES = 8
CONV_HALO = 8
POOL_HALO = 16
VMEM_LIMIT = 56 * 1024 * 1024

ADA_COLS = 2048
MIX_TILE = 512
MIX_CHAINS = 2
ROUTE_LANES = 512
MOE_BLOCK = 4096
MOE_CHUNK = 192
STAGE_PAD = 4
SCATTER_GROUP = 8
TOKEN_RADIX = 64.0
LIST_LEAD = 2
EPI_TILE = 512


def _silu(v):
    return v * jax.nn.sigmoid(v)


def _rms_scale(v):
    return lax.rsqrt(jnp.mean(v * v, axis=-1, keepdims=True) + EPS)


def _ada_kernel(c_ref, w_ref, b_ref, o_ref):
    cond = _silu(c_ref[...]).astype(BF16)
    o_ref[0] = jnp.dot(cond, w_ref[0].astype(BF16), preferred_element_type=F32) + b_ref[0]


def _ada_mod(c, w_ada, b_ada):
    depth, d, d6 = w_ada.shape
    b = c.shape[0]
    cols = ADA_COLS
    return pl.pallas_call(
        _ada_kernel,
        grid=(depth, d6 // cols),
        in_specs=[
            pl.BlockSpec((b, d), lambda l, n: (0, 0)),
            pl.BlockSpec((1, d, cols), lambda l, n: (l, 0, n)),
            pl.BlockSpec((1, 1, cols), lambda l, n: (l, 0, n)),
        ],
        out_specs=pl.BlockSpec((1, b, cols), lambda l, n: (l, 0, n)),
        out_shape=jax.ShapeDtypeStruct((depth, b, d6), F32),
    )(c, w_ada, b_ada.reshape(depth, 1, d6))


def _ffn_residual(x, routed_ref, shared, gt2, g_post, row0=0):
    rows, d = x.shape
    n_chunk = d // LANES
    routed = jnp.concatenate(
        [routed_ref[0, pl.ds(row0 * n_chunk + c, rows, stride=n_chunk), :]
         for c in range(n_chunk)], axis=1)
    y = routed + shared
    return x + gt2 * (y * _rms_scale(y) * g_post)


def _mixer_kernel(*refs, after_ffn):
    if after_ffn:
        (x_ref, routed_ref, shared_ref, modp_ref, gpp_ref), refs = refs[:5], refs[5:]
    else:
        x_ref, refs = refs[0], refs[1:]
    (mod_ref, gpre_ref, gpost_ref, win_ref, convw_ref, wco_ref, wpg_ref, pscale_ref, wpp_ref,
     wo_ref, o_ref, uext_ref, pext_ref) = refs
    j = pl.program_id(1)
    tm, d = x_ref.shape[1], x_ref.shape[2]
    d_pool = pext_ref.shape[1]

    @pl.when(j == 0)
    def _():
        uext_ref[0:CONV_HALO, :] = jnp.zeros((CONV_HALO, d), F32)
        pext_ref[0:POOL_HALO, :] = jnp.zeros((POOL_HALO, d_pool), F32)

    mod = mod_ref[0]
    sh1, sc1, gt1 = mod[0:1], mod[1:2], mod[2:3]
    cw = convw_ref[...]
    ts = tm // MIX_CHAINS
    for ch in range(MIX_CHAINS):
        r0 = ch * ts
        x = x_ref[0, r0:r0 + ts, :]
        if after_ffn:
            x = _ffn_residual(x, routed_ref, shared_ref[0, r0:r0 + ts, :], modp_ref[0][5:6],
                              gpp_ref[...], row0=r0)
        h = x * _rms_scale(x) * gpre_ref[...] * (1.0 + sc1) + sh1
        hb = h.astype(BF16)

        def proj(lo, hi):
            return jnp.dot(hb, win_ref[:, lo:hi], preferred_element_type=F32)

        u = proj(d, 2 * d) * proj(2 * d, 3 * d)
        u0 = CONV_HALO + r0
        uext_ref[u0:u0 + ts, :] = u
        conv = (cw[2:3] * u
                + cw[1:2] * uext_ref[u0 - 1:u0 - 1 + ts, :]
                + cw[0:1] * uext_ref[u0 - 2:u0 - 2 + ts, :])
        y_conv = jnp.dot((proj(0, d) * conv).astype(BF16), wco_ref[...],
                         preferred_element_type=F32)

        up = proj(3 * d, 3 * d + d_pool)
        p0 = POOL_HALO + r0
        pext_ref[p0:p0 + ts, :] = up
        pos = j * tm + r0 + lax.broadcasted_iota(jnp.int32, (ts, 1), 0)
        zs = []
        for g, w in enumerate(POOL_WINDOWS):
            c0 = g * POOL_GROUP_DIM
            ug = up[:, c0:c0 + POOL_GROUP_DIM]
            acc = ug
            for k in range(1, w):
                acc = acc + pext_ref[p0 - k:p0 - k + ts, c0:c0 + POOL_GROUP_DIM]
            inv_cnt = 1.0 / jnp.minimum(pos + 1, w).astype(F32)
            diff = acc * inv_cnt - ug
            zs.append(jnp.dot(diff.astype(BF16), wpg_ref[g], preferred_element_type=F32))
        z = jnp.concatenate(zs, axis=1) * pscale_ref[...]
        y_pool = jnp.dot(z.astype(BF16), wpp_ref[...], preferred_element_type=F32)

        a_conv = proj(3 * d + d_pool, 4 * d + d_pool)
        a_pool = proj(4 * d + d_pool, 5 * d + d_pool)
        merged = jax.nn.sigmoid(a_conv) * y_conv + jax.nn.sigmoid(a_pool) * y_pool
        y = jnp.dot(merged.astype(BF16), wo_ref[...], preferred_element_type=F32)
        o_ref[0, r0:r0 + ts, :] = x + gt1 * (y * _rms_scale(y) * gpost_ref[...])

    uext_ref[0:CONV_HALO, :] = uext_ref[tm:tm + CONV_HALO, :]
    pext_ref[0:POOL_HALO, :] = pext_ref[tm:tm + POOL_HALO, :]


def _token_mixer(x, pending_ffn, mod, g_pre, g_post, w_in, conv_w, w_conv_out, w_pool_group,
                 pool_scale, w_pool_proj, w_o):
    b, s, d = x.shape
    d_in = w_in.shape[1]
    d_pool = w_pool_proj.shape[0]
    n_chunk = d // LANES
    tm = min(MIX_TILE, s)
    const2 = lambda bi, j: (0, 0)
    const3 = lambda bi, j: (0, 0, 0)
    rows = pl.BlockSpec((1, tm, d), lambda bi, j: (bi, j, 0))
    per_batch = pl.BlockSpec((1, 6, d), lambda bi, j: (bi, 0, 0))
    prev_specs, prev_args = [], []
    if pending_ffn is not None:
        routed, shared, mod_p, g_post_p, tb = pending_ffn
        tiles = tb // tm
        prev_specs = [
            pl.BlockSpec((1, tm * n_chunk, LANES),
                         lambda bi, j: ((bi * (s // tm) + j) // tiles, (bi * (s // tm) + j) % tiles, 0)),
            rows, per_batch, pl.BlockSpec((1, d), const2)]
        prev_args = [routed, shared.reshape(b, s, d), mod_p, g_post_p.reshape(1, d)]
    return pl.pallas_call(
        functools.partial(_mixer_kernel, after_ffn=pending_ffn is not None),
        grid=(b, s // tm),
        in_specs=[
            rows,
            *prev_specs,
            per_batch,
            pl.BlockSpec((1, d), const2),
            pl.BlockSpec((1, d), const2),
            pl.BlockSpec((d, d_in), const2),
            pl.BlockSpec((3, d), const2),
            pl.BlockSpec((d, d), const2),
            pl.BlockSpec(w_pool_group.shape, const3),
            pl.BlockSpec((1, d_pool), const2),
            pl.BlockSpec((d_pool, d), const2),
            pl.BlockSpec((d, d), const2),
        ],
        out_specs=pl.BlockSpec((1, tm, d), lambda bi, j: (bi, j, 0)),
        out_shape=jax.ShapeDtypeStruct(x.shape, F32),
        scratch_shapes=[
            pltpu.VMEM((CONV_HALO + tm, d), F32),
            pltpu.VMEM((POOL_HALO + tm, d_pool), F32),
        ],
        compiler_params=pltpu.CompilerParams(
            dimension_semantics=("arbitrary", "arbitrary"), vmem_limit_bytes=VMEM_LIMIT),
    )(x, *prev_args, mod, g_pre.reshape(1, d), g_post.reshape(1, d), w_in.astype(BF16), conv_w,
      w_conv_out.astype(BF16), w_pool_group.astype(BF16), pool_scale.reshape(1, d_pool),
      w_pool_proj.astype(BF16), w_o.astype(BF16))


def _wins(other, v, tie_i):
    return jnp.where(other > v, 1, 0) + jnp.where(other == v, tie_i, 0)


def _route_kernel(x_ref, mod_ref, g_ref, wrt_ref, bias_ref, wsg_ref, wsu_ref, wsd_ref,
                  h2g_ref, shared_ref, rowl_ref, wl_ref, nch_ref, q0_ref,
                  lg_ref, pre_ref, rho_ref, wd_ref):
    i = pl.program_id(1)
    lc, d = x_ref.shape
    n_sub = lg_ref.shape[0]
    n_chunk = d // LANES

    @pl.when(i < n_sub)
    def _():
        mod = mod_ref[0]
        sh2, sc2 = mod[3:4], mod[4:5]

        @pl.when(i == 0)
        def _():
            spare = n_sub * lc * n_chunk
            h2g_ref[0, spare:spare + n_chunk, :] = jnp.zeros((n_chunk, LANES), F32)

        x = x_ref[...]
        h = x * _rms_scale(x) * g_ref[...] * (1.0 + sc2) + sh2
        for c in range(n_chunk):
            h2g_ref[0, pl.ds(i * lc * n_chunk + c, lc, stride=n_chunk), :] = (
                h[:, c * LANES:(c + 1) * LANES])
        hb = h.astype(BF16)
        act = (_silu(jnp.dot(hb, wsg_ref[...], preferred_element_type=F32))
               * jnp.dot(hb, wsu_ref[...], preferred_element_type=F32))
        shared_ref[...] = jnp.dot(act.astype(BF16), wsd_ref[...], preferred_element_type=F32)
        lg_ref[i] = lax.dot_general(
            wrt_ref[...], hb, (((1,), (1,)), ((), ())), preferred_element_type=F32)

    @pl.when(i == n_sub)
    def _():
        _route_plan(bias_ref, rowl_ref, wl_ref, nch_ref, q0_ref, lg_ref, pre_ref, rho_ref, wd_ref,
                    lc, n_chunk)


def _route_plan(bias_ref, rowl_ref, wl_ref, nch_ref, q0_ref, lg_ref, pre_ref, rho_ref, wd_ref,
                lc, n_chunk):
    n_sub = lg_ref.shape[0]
    tb = n_sub * lc
    gidx = lax.broadcasted_iota(jnp.int32, (N_GROUPS, lc), 0)
    tie = [None] + [jnp.where(gidx >= r, 1, 0) for r in range(1, N_GROUPS)]
    tri = (lax.broadcasted_iota(jnp.int32, (lc, lc), 0)
           < lax.broadcasted_iota(jnp.int32, (lc, lc), 1)).astype(BF16)
    carry = jnp.zeros((N_EXPERTS, 1), F32)
    neg_inf = jnp.float32(-jnp.inf)
    for ci in range(tb // lc):
        c0 = ci * lc
        s_all = jax.nn.sigmoid(lg_ref[ci])
        aff = [s_all[GROUP_SIZE * jj:GROUP_SIZE * (jj + 1), :] for jj in range(GROUP_SIZE)]
        sel = [aff[jj] + bias_ref[GROUP_SIZE * jj:GROUP_SIZE * (jj + 1), :]
               for jj in range(GROUP_SIZE)]
        m1, m2 = sel[0], jnp.full_like(sel[0], neg_inf)
        for jj in range(1, GROUP_SIZE):
            m2 = jnp.maximum(m2, jnp.minimum(m1, sel[jj]))
            m1 = jnp.maximum(m1, sel[jj])
        gs = m1 + m2
        beaten = jnp.zeros((N_GROUPS, lc), jnp.int32)
        for r in range(1, N_GROUPS):
            other = pltpu.roll(gs, r, axis=0)
            beaten = beaten + _wins(other, gs, tie[r])
        gmask = beaten < TOPK_GROUPS
        masked = [jnp.where(gmask, sel[jj], neg_inf) for jj in range(GROUP_SIZE)]
        rolled = [[masked[jj]] + [pltpu.roll(masked[jj], r, axis=0) for r in range(1, N_GROUPS)]
                  for jj in range(GROUP_SIZE)]
        rho = []
        for jj in range(GROUP_SIZE):
            v = masked[jj]
            cnt = jnp.zeros((N_GROUPS, lc), jnp.int32)
            for j2 in range(GROUP_SIZE):
                for r in range(N_GROUPS):
                    if r == 0 and j2 == jj:
                        continue
                    other = rolled[j2][r]
                    if r == 0:
                        wins = (other >= v) if j2 < jj else (other > v)
                        cnt = cnt + jnp.where(wins, 1, 0)
                    else:
                        cnt = cnt + _wins(other, v, tie[r])
            rho.append(cnt)
        chosen = [rho[jj] < TOP_K for jj in range(GROUP_SIZE)]
        ssum = jnp.zeros((N_GROUPS, lc), F32)
        for jj in range(GROUP_SIZE):
            ssum = ssum + jnp.where(chosen[jj], aff[jj], 0.0)
        ssum = jnp.sum(ssum, axis=0, keepdims=True)
        wdense = [jnp.where(chosen[jj], aff[jj] / ssum * ROUTED_SCALE, 0.0)
                  for jj in range(GROUP_SIZE)]
        chosen_f = jnp.concatenate([c.astype(F32) for c in chosen], axis=0)
        prefix = jnp.dot(chosen_f.astype(BF16), tri, preferred_element_type=F32) + carry
        carry = carry + jnp.sum(chosen_f, axis=1, keepdims=True)
        pre_ref[:, c0:c0 + lc] = prefix
        rho_ref[:, c0:c0 + lc] = jnp.concatenate(rho, axis=0)
        wd_ref[:, c0:c0 + lc] = jnp.concatenate(wdense, axis=0)

    m = MOE_CHUNK
    nch_b = jnp.broadcast_to(jnp.floor((carry + (m - 0.5)) * (1.0 / m)), (N_EXPERTS, LANES))
    lower = (lax.broadcasted_iota(jnp.int32, (N_EXPERTS, N_EXPERTS), 1)
             < lax.broadcasted_iota(jnp.int32, (N_EXPERTS, N_EXPERTS), 0)).astype(F32)
    q0_b = jnp.dot(lower, nch_b, preferred_element_type=F32, precision=lax.Precision.HIGHEST)
    nch_ref[0] = nch_b.astype(jnp.int32)
    q0_ref[0] = q0_b.astype(jnp.int32)
    offs_col = q0_b[:, 0:1] * m

    nq = rowl_ref.shape[1]
    iota_q = lax.broadcasted_iota(jnp.int32, (nq, lc), 0).astype(F32)
    iota_r = lax.broadcasted_iota(jnp.int32, (m, lc), 0).astype(F32)
    lists = jnp.zeros((nq, 5 * m), F32)
    for ci in range(tb // lc):
        c0 = ci * lc
        dest_dense = pre_ref[:, c0:c0 + lc] + offs_col
        rho_c = rho_ref[:, c0:c0 + lc]
        w_c = wd_ref[:, c0:c0 + lc]
        tokv = (c0 + 1 + lax.broadcasted_iota(jnp.int32, (1, lc), 1)).astype(F32)
        tok_hi = jnp.floor(tokv * (1.0 / TOKEN_RADIX))
        tok_lo = tokv - TOKEN_RADIX * tok_hi
        for k in range(TOP_K):
            hit = rho_c == k
            dk = jnp.sum(jnp.where(hit, dest_dense, 0.0), axis=0, keepdims=True)
            wk = jnp.sum(jnp.where(hit, w_c, 0.0), axis=0, keepdims=True)
            qk = jnp.floor((dk + 0.5) * (1.0 / m))
            rk = dk - m * qk
            w_hi = wk.astype(BF16).astype(F32)
            w_mid = (wk - w_hi).astype(BF16).astype(F32)
            w_lo = wk - w_hi - w_mid
            onehot_q = jnp.where(iota_q == qk + LIST_LEAD, 1.0, 0.0).astype(BF16)
            rmask = iota_r == rk
            vals = jnp.concatenate(
                [jnp.where(rmask, piece, 0.0) for piece in (tok_hi, tok_lo, w_hi, w_mid, w_lo)],
                axis=0).astype(BF16)
            lists = lists + lax.dot_general(onehot_q, vals, (((1,), (1,)), ((), ())),
                                            preferred_element_type=F32)
    tok = lists[:, 0:m] * TOKEN_RADIX + lists[:, m:2 * m]
    tile = jnp.where(tok == 0.0, float(tb), tok - 1.0)
    rowl_ref[0] = (tile * n_chunk).astype(jnp.int32)
    wl_ref[0] = (lists[:, 2 * m:3 * m] + lists[:, 3 * m:4 * m]) + lists[:, 4 * m:5 * m]


def _route(x1, mod, g_pre, w_router, router_bias, w_sh_gate, w_sh_up, w_sh_down, seq, tb):
    t, d = x1.shape
    nb = t // tb
    n_chunk = d // LANES
    d_sh = w_sh_gate.shape[1]
    perm = jnp.arange(N_EXPERTS).reshape(N_GROUPS, GROUP_SIZE).T.reshape(-1)
    wrt = w_router.T[perm].astype(BF16)
    bias = router_bias[perm].reshape(N_EXPERTS, 1)
    nq = _num_list_rows(tb)
    lc = min(ROUTE_LANES, tb)
    n_sub = tb // lc
    assert tb <= 256 * TOKEN_RADIX and TOKEN_RADIX <= 256, "token digits must be exact in bf16"
    const2 = lambda bi, i: (0, 0)
    per_block = lambda bi, i: (bi, 0, 0)
    sub_tile = lambda bi, i: (bi * n_sub + jnp.minimum(i, n_sub - 1), 0)
    outs = pl.pallas_call(
        _route_kernel,
        grid=(nb, n_sub + 1),
        in_specs=[
            pl.BlockSpec((lc, d), sub_tile),
            pl.BlockSpec((1, 6, d), lambda bi, i: (bi * tb // seq, 0, 0)),
            pl.BlockSpec((1, d), const2),
            pl.BlockSpec((N_EXPERTS, d), const2),
            pl.BlockSpec((N_EXPERTS, 1), const2),
            pl.BlockSpec((d, d_sh), const2),
            pl.BlockSpec((d, d_sh), const2),
            pl.BlockSpec((d_sh, d), const2),
        ],
        out_specs=[
            pl.BlockSpec((1, (tb + 1) * n_chunk, LANES), per_block),
            pl.BlockSpec((lc, d), sub_tile),
            pl.BlockSpec((1, nq, MOE_CHUNK), per_block),
            pl.BlockSpec((1, nq, MOE_CHUNK), per_block),
            pl.BlockSpec((1, N_EXPERTS, LANES), per_block),
            pl.BlockSpec((1, N_EXPERTS, LANES), per_block),
        ],
        out_shape=[
            jax.ShapeDtypeStruct((nb, (tb + 1) * n_chunk, LANES), F32),
            jax.ShapeDtypeStruct((t, d), F32),
            jax.ShapeDtypeStruct((nb, nq, MOE_CHUNK), jnp.int32),
            jax.ShapeDtypeStruct((nb, nq, MOE_CHUNK), F32),
            jax.ShapeDtypeStruct((nb, N_EXPERTS, LANES), jnp.int32),
            jax.ShapeDtypeStruct((nb, N_EXPERTS, LANES), jnp.int32),
        ],
        scratch_shapes=[
            pltpu.VMEM((n_sub, N_EXPERTS, lc), F32),
            pltpu.VMEM((N_EXPERTS, tb), F32),
            pltpu.VMEM((N_EXPERTS, tb), jnp.int32),
            pltpu.VMEM((N_EXPERTS, tb), F32),
        ],
        compiler_params=pltpu.CompilerParams(
            dimension_semantics=("arbitrary", "arbitrary"), vmem_limit_bytes=VMEM_LIMIT),
    )(x1, mod, g_pre.reshape(1, d), wrt, bias, w_sh_gate.astype(BF16), w_sh_up.astype(BF16),
      w_sh_down.astype(BF16))
    return outs


def _num_list_rows(tb):
    rows = -(-TOP_K * tb // MOE_CHUNK) + N_EXPERTS + 2 * LIST_LEAD
    return -(-rows // SUBLANES) * SUBLANES


def _moe_kernel(nch_sm, q0_sm, h2g_hbm, rowl_hbm, wl_ref, wg_ref, wu_ref, wd_ref, o_hbm,
                h2g_ref, acc_ref, xt0_ref, xt1_ref, yt0_ref, yt1_ref, act0_ref, act1_ref,
                wgb_ref, wub_ref, wdb_ref, rowl_sm, experts_sm, sems):
    bi = pl.program_id(0)
    r = pl.program_id(1)
    n_chunk = xt0_ref.shape[0] // (MOE_CHUNK + STAGE_PAD)
    m = MOE_CHUNK
    stride = m + STAGE_PAD
    eye = (lax.broadcasted_iota(jnp.int32, (m, m), 0)
           == lax.broadcasted_iota(jnp.int32, (m, m), 1))
    xts, yts, acts = (xt0_ref, xt1_ref), (yt0_ref, yt1_ref), (act0_ref, act1_ref)

    def gather(lrow, xt_ref):
        base = lrow * m
        for mi in range(m):
            row = pl.multiple_of(rowl_sm[base + mi], n_chunk)
            xt_ref[pl.ds(mi, n_chunk, stride=stride), :] = h2g_ref[pl.ds(row, n_chunk), :]

    def scatter(lrow, yt_ref):
        base = lrow * m
        for g0 in range(0, m, SCATTER_GROUP):
            rows, vals = [], []
            for mi in range(g0, g0 + SCATTER_GROUP):
                row = pl.multiple_of(rowl_sm[base + mi], n_chunk)
                rows.append(row)
                vals.append(acc_ref[pl.ds(row, n_chunk), :]
                            + yt_ref[pl.ds(mi, n_chunk, stride=stride), :])
            for row, val in zip(rows, vals):
                acc_ref[pl.ds(row, n_chunk), :] = val

    def gate_up(xt_ref, act_ref):
        xs = jnp.concatenate([xt_ref[c * stride:c * stride + m, :] for c in range(n_chunk)],
                             axis=1).astype(BF16)
        act = (_silu(jnp.dot(xs, wgb_ref[...], preferred_element_type=F32))
               * jnp.dot(xs, wub_ref[...], preferred_element_type=F32))
        act_ref[...] = act.astype(BF16)

    def down(lrow, act_ref, slot, yt_ref):
        y = jnp.dot(act_ref[...], wdb_ref[slot], preferred_element_type=F32)
        w_row = wl_ref[0, pl.ds(lrow, 1), :]
        w_col = jnp.sum(jnp.where(eye, w_row, 0.0), axis=1, keepdims=True)
        y = y * w_col
        for c in range(n_chunk):
            yt_ref[c * stride:c * stride + m, :] = y[:, c * LANES:(c + 1) * LANES]

    def step(q, par, down_slot):
        cur, oth = par, 1 - par
        gather(q + 1 + LIST_LEAD, xts[oth])
        gate_up(xts[cur], acts[cur])
        down(q - 1 + LIST_LEAD, acts[oth], down_slot, yts[oth])
        scatter(q - 2 + LIST_LEAD, yts[cur])

    def by_parity(q, fn):
        for par in (0, 1):
            @pl.when((q & 1) == par)
            def _():
                fn(par)

    def block_loads():
        return (pltpu.make_async_copy(rowl_hbm.at[bi], rowl_sm, sems.at[0]),
                pltpu.make_async_copy(h2g_hbm.at[bi], h2g_ref, sems.at[1]))

    def block_store():
        return pltpu.make_async_copy(acc_ref, o_hbm.at[bi], sems.at[2])

    @pl.when(r == 0)
    def _():
        for cp in block_loads():
            cp.start()
        acc_ref[...] = jnp.zeros(acc_ref.shape, F32)
        act1_ref[...] = jnp.zeros(act1_ref.shape, BF16)
        yt0_ref[...] = jnp.zeros(yt0_ref.shape, F32)
        wdb_ref[...] = jnp.zeros(wdb_ref.shape, BF16)
        experts_sm[0] = 0
        for cp in block_loads():
            cp.wait()
        gather(LIST_LEAD, xt0_ref)

    n_chunks = nch_sm[bi * N_EXPERTS + r]
    q_first = q0_sm[bi * N_EXPERTS + r]
    slot = experts_sm[0] & 1

    @pl.when(n_chunks > 0)
    def _():
        wgb_ref[...] = wg_ref[0, 0].astype(BF16)
        wub_ref[...] = wu_ref[0, 0].astype(BF16)
        wdb_ref[slot] = wd_ref[0, 0].astype(BF16)
        by_parity(q_first, lambda par: step(q_first, par, 1 - slot))
        experts_sm[0] = experts_sm[0] + 1

    def chunk(ci, carry):
        q = q_first + ci
        by_parity(q, lambda par: step(q, par, slot))
        return carry

    lax.fori_loop(1, n_chunks, chunk, 0)

    @pl.when(r == N_EXPERTS - 1)
    def _():
        q_last = q_first + n_chunks - 1
        last_slot = (experts_sm[0] - 1) & 1

        def drain(par):
            scatter(q_last - 1 + LIST_LEAD, yts[1 - par])
            down(q_last + LIST_LEAD, acts[par], last_slot, yts[par])
            scatter(q_last + LIST_LEAD, yts[par])

        by_parity(q_last, drain)

        block_store().start()
        block_store().wait()


def _moe(h2g, rowl, wl, nch, q0, w_gate, w_up, w_down, layer, t, d, tb):
    nb = t // tb
    n_chunk = d // LANES
    d_e = w_gate.shape[3]
    m = MOE_CHUNK
    stride = m + STAGE_PAD
    nq = _num_list_rows(tb)

    def expert_of(r):
        return (r % N_GROUPS) * GROUP_SIZE + r // N_GROUPS

    cur_w = lambda bi, r, c, o: (layer, expert_of(r), 0, 0)
    per_block = lambda bi, r, c, o: (bi, 0, 0)
    staging = pltpu.VMEM((n_chunk * stride, LANES), F32)
    grid_spec = pltpu.PrefetchScalarGridSpec(
        num_scalar_prefetch=2,
        grid=(nb, N_EXPERTS),
        in_specs=[
            pl.BlockSpec(memory_space=pl.ANY),
            pl.BlockSpec(memory_space=pl.ANY),
            pl.BlockSpec((1, nq, m), per_block),
            pl.BlockSpec((1, 1, d, d_e), cur_w),
            pl.BlockSpec((1, 1, d, d_e), cur_w),
            pl.BlockSpec((1, 1, d_e, d), cur_w),
        ],
        out_specs=pl.BlockSpec(memory_space=pl.ANY),
        scratch_shapes=[
            pltpu.VMEM(((tb + 1) * n_chunk, LANES), F32),
            pltpu.VMEM(((tb + 1) * n_chunk, LANES), F32),
            staging, staging, staging, staging,
            pltpu.VMEM((m, d_e), BF16),
            pltpu.VMEM((m, d_e), BF16),
            pltpu.VMEM((d, d_e), BF16),
            pltpu.VMEM((d, d_e), BF16),
            pltpu.VMEM((2, d_e, d), BF16),
            pltpu.SMEM((nq * m,), jnp.int32),
            pltpu.SMEM((1,), jnp.int32),
            pltpu.SemaphoreType.DMA((3,)),
        ],
    )
    return pl.pallas_call(
        _moe_kernel,
        grid_spec=grid_spec,
        out_shape=jax.ShapeDtypeStruct((nb, (tb + 1) * n_chunk, LANES), F32),
        compiler_params=pltpu.CompilerParams(
            dimension_semantics=("arbitrary", "arbitrary"), vmem_limit_bytes=VMEM_LIMIT),
    )(nch.reshape(-1), q0.reshape(-1), h2g, rowl.reshape(nb, nq * m), wl, w_gate, w_up, w_down)


def _epilogue_kernel(x_ref, routed_ref, shared_ref, mod_ref, g_ref, o_ref):
    o_ref[...] = _ffn_residual(x_ref[...], routed_ref, shared_ref[...], mod_ref[0][5:6], g_ref[...])


def _epilogue(x1, routed, shared, mod, g_post, seq, tb):
    t, d = x1.shape
    n_chunk = d // LANES
    te = min(EPI_TILE, tb)
    n_sub = tb // te
    row = pl.BlockSpec((te, d), lambda bi, i: (bi * n_sub + i, 0))
    return pl.pallas_call(
        _epilogue_kernel,
        grid=(t // tb, n_sub),
        in_specs=[row,
                  pl.BlockSpec((1, te * n_chunk, LANES), lambda bi, i: (bi, i, 0)),
                  row,
                  pl.BlockSpec((1, 6, d), lambda bi, i: (bi * tb // seq, 0, 0)),
                  pl.BlockSpec((1, d), lambda bi, i: (0, 0))],
        out_specs=row,
        out_shape=jax.ShapeDtypeStruct((t, d), F32),
    )(x1, routed, shared, mod, g_post.reshape(1, d))


def kernel(x, c, w_ada, b_ada, g_pre_mix, g_post_mix, g_pre_ffn, g_post_ffn, w_in, conv_w,
           w_conv_out, w_pool_group, pool_scale, w_pool_proj, w_o, w_router, router_bias,
           w_exp_gate, w_exp_up, w_exp_down, w_sh_gate, w_sh_up, w_sh_down):
    b, s, d = x.shape
    depth = w_ada.shape[0]
    t = b * s
    tb = min(MOE_BLOCK, s)
    mods = _ada_mod(c, w_ada, b_ada).reshape(depth, b, 6, d)
    pending_ffn = None
    for l in range(depth):
        mod = mods[l]
        x = _token_mixer(x, pending_ffn, mod, g_pre_mix[l], g_post_mix[l], w_in[l], conv_w[l],
                         w_conv_out[l], w_pool_group[l], pool_scale[l], w_pool_proj[l], w_o[l])
        x1 = x.reshape(t, d)
        h2g, shared, rowl, wl, nch, q0 = _route(
            x1, mod, g_pre_ffn[l], w_router[l], router_bias[l], w_sh_gate[l], w_sh_up[l],
            w_sh_down[l], s, tb)
        routed = _moe(h2g, rowl, wl, nch[:, :, 0], q0[:, :, 0],
                      w_exp_gate, w_exp_up, w_exp_down, l, t, d, tb)
        pending_ffn = (routed, shared, mod, g_post_ffn[l], tb)
    routed, shared, mod, g_post, tb = pending_ffn
    return _epilogue(x.reshape(t, d), routed, shared, mod, g_post, s, tb).reshape(b, s, d)
```
